```python
import jax, jax.numpy as jnp
from jax import lax
import numpy as np

D_MODEL = 1024
BATCH = 16
SEQ = 2048
DEPTH = 2

MEM_LEN = 256
EPS = 1e-6
GLA_HEADS = 4
GLA_KEY = D_MODEL // 2
GLA_VAL = D_MODEL
GLA_RANK = 16
GLA_GATE_NORM = 16.0
GLA_LOG_DECAY_MIN = -1.0
GLA_CHUNK = 64
M_HEADS = 4
M_KEY = D_MODEL // 2
M_VAL = D_MODEL
M_CHUNK = 64
CONV_W = 4
X_HEADS = 4
FFN_DIM = 64 * ((8 * D_MODEL // 3 + 63) // 64)
N_EXPERTS = 8
TOP_K = 2
EXPERT_DIM = 7 * D_MODEL // 2
N_DENSE = (DEPTH + 1) // 2
N_MOE = DEPTH // 2
IN_SPLITS = (GLA_KEY, GLA_KEY, GLA_VAL, GLA_RANK, GLA_VAL,
             M_KEY, M_KEY, M_VAL, M_HEADS, M_HEADS, M_VAL,
             D_MODEL, D_MODEL)
N_IN = sum(IN_SPLITS)

kernel_name = 'hybrid_gla_mlstm_xattn_moe_block'


def rms_norm(x, g):
    xf = x.astype(jnp.float32)
    y = xf * lax.rsqrt(jnp.mean(xf * xf, axis=-1, keepdims=True) + EPS)
    return (y * g.astype(jnp.float32)).astype(x.dtype)


def _heads(t, n_heads):
    return t.reshape(t.shape[0], t.shape[1], n_heads, -1)


def _head_rms_norm(o, g):
    y = o * lax.rsqrt(jnp.mean(o * o, axis=-1, keepdims=True) + EPS)
    return y.reshape(o.shape[0], o.shape[1], -1) * g.astype(jnp.float32)


def _causal_conv(t, w):
    k_w = w.shape[0]
    s = t.shape[1]
    tp = jnp.pad(t, ((0, 0), (k_w - 1, 0), (0, 0)))
    return sum(tp[:, j:j + s] * w[j] for j in range(k_w))


def gla_chunked(q, k, v, log_a):
    f32 = jnp.float32
    b_, s_, h_, dk = q.shape
    dv = v.shape[-1]
    c_ = GLA_CHUNK
    n_ = s_ // c_
    q = q.astype(f32).reshape(b_, n_, c_, h_, dk) * dk ** -0.5
    k = k.astype(f32).reshape(b_, n_, c_, h_, dk)
    v = v.astype(f32).reshape(b_, n_, c_, h_, dv)
    cum = jnp.cumsum(log_a.astype(f32).reshape(b_, n_, c_, h_, dk), axis=2)
    q_dec = q * jnp.exp(cum)
    scores = jnp.einsum('bnthd,bnshd->bnhts', q_dec, k * jnp.exp(-cum))
    causal = jnp.tril(jnp.ones((c_, c_), dtype=bool))
    scores = jnp.where(causal, scores, 0.0)
    o_intra = jnp.einsum('bnhts,bnshv->bnthv', scores, v)
    cum_last = cum[:, :, -1]
    k_end = k * jnp.exp(cum_last[:, :, None] - cum)
    d_state = jnp.einsum('bnshd,bnshv->nbhdv', k_end, v)
    decay = jnp.moveaxis(jnp.exp(cum_last), 1, 0)

    def step(state, inp):
        a, ds = inp
        return a[..., None] * state + ds, state

    _, s_prev = lax.scan(step, jnp.zeros((b_, h_, dk, dv), f32), (decay, d_state))
    o_inter = jnp.einsum('bnthd,nbhdv->bnthv', q_dec, s_prev)
    return (o_intra + o_inter).reshape(b_, s_, h_, dv)


def mlstm_chunked(q, k, v, log_i, log_f):
    f32 = jnp.float32
    b_, s_, h_, dk = q.shape
    dv = v.shape[-1]
    c_ = M_CHUNK
    n_ = s_ // c_
    q = q.astype(f32).reshape(b_, n_, c_, h_, dk) * dk ** -0.5
    k = k.astype(f32).reshape(b_, n_, c_, h_, dk)
    v = v.astype(f32).reshape(b_, n_, c_, h_, dv)
    li = log_i.reshape(b_, n_, c_, h_)
    g = jnp.cumsum(log_f.reshape(b_, n_, c_, h_), axis=2)
    causal = jnp.tril(jnp.ones((c_, c_), dtype=bool))
    dmat = g[:, :, :, None, :] - g[:, :, None, :, :] + li[:, :, None, :, :]
    dmat = jnp.where(causal[:, :, None], dmat, -jnp.inf)
    m_intra = jnp.max(dmat, axis=3)
    w_ts = jnp.exp(dmat - m_intra[:, :, :, None, :]) * jnp.einsum('bnthd,bnshd->bntsh', q, k)
    num_intra = jnp.einsum('bntsh,bnshv->bnthv', w_ts, v)
    den_intra = jnp.sum(w_ts, axis=3)
    g_last = g[:, :, -1]
    e = g_last[:, :, None] - g + li
    m_loc = jnp.max(e, axis=2)
    w_s = jnp.exp(e - m_loc[:, :, None])
    kw = k * w_s[..., None]
    d_c = jnp.einsum('bnshd,bnshv->nbhdv', kw, v)
    d_n = jnp.sum(kw, axis=2).transpose(1, 0, 2, 3)

    def step(carry, inp):
        c_st, n_st, m_st = carry
        gl, ml, dc, dn = inp
        m_new = jnp.maximum(gl + m_st, ml)
        a = jnp.exp(gl + m_st - m_new)
        bb = jnp.exp(ml - m_new)
        c_new = a[..., None, None] * c_st + bb[..., None, None] * dc
        n_new = a[..., None] * n_st + bb[..., None] * dn
        return (c_new, n_new, m_new), (c_st, n_st, m_st)

    init = (jnp.zeros((b_, h_, dk, dv), f32), jnp.zeros((b_, h_, dk), f32), jnp.zeros((b_, h_), f32))
    _, (c_prev, n_prev, m_prev) = lax.scan(
        step, init, (jnp.moveaxis(g_last, 1, 0), jnp.moveaxis(m_loc, 1, 0), d_c, d_n))
    m_inter = g + jnp.moveaxis(m_prev, 0, 1)[:, :, None]
    m_t = jnp.maximum(m_inter, m_intra)
    s_inter = jnp.exp(m_inter - m_t)
    s_intra = jnp.exp(m_intra - m_t)
    num = (s_intra[..., None] * num_intra
           + s_inter[..., None] * jnp.einsum('bnthd,nbhdv->bnthv', q, c_prev))
    den = s_intra * den_intra + s_inter * jnp.einsum('bnthd,nbhd->bnth', q, n_prev)
    h = num / jnp.maximum(jnp.abs(den), jnp.exp(-m_t))[..., None]
    return h.reshape(b_, s_, h_, dv)


def hybrid_mixer(h, w_in, gla_gk_up, gla_gk_bias, gla_norm, m_conv, m_gate_bias, m_norm,
                 gla_proj, m_proj, w_out):
    f32 = jnp.float32
    z = h @ w_in
    offsets = np.cumsum(IN_SPLITS)[:-1].tolist()
    (g_q, g_k, g_v, g_lr, g_g, m_q, m_k, m_v, m_i, m_f, m_o, a_gla, a_m) = jnp.split(z, offsets, axis=-1)
    log_a = jax.nn.log_sigmoid((g_lr @ gla_gk_up + gla_gk_bias).astype(f32)) / GLA_GATE_NORM
    log_a = jnp.maximum(log_a, GLA_LOG_DECAY_MIN)
    o_gla = gla_chunked(_heads(g_q, GLA_HEADS), _heads(g_k, GLA_HEADS),
                        _heads(g_v, GLA_HEADS), _heads(log_a, GLA_HEADS))
    o_gla = (_head_rms_norm(o_gla, gla_norm) * jax.nn.silu(g_g.astype(f32))).astype(h.dtype)
    qk = jax.nn.silu(_causal_conv(jnp.concatenate([m_q, m_k], axis=-1), m_conv))
    m_q, m_k = jnp.split(qk, 2, axis=-1)
    gates = (jnp.concatenate([m_i, m_f], axis=-1) + m_gate_bias).astype(f32)
    log_i = gates[..., :M_HEADS]
    log_f = jax.nn.log_sigmoid(gates[..., M_HEADS:])
    o_m = mlstm_chunked(_heads(m_q, M_HEADS), _heads(m_k, M_HEADS), _heads(m_v, M_HEADS), log_i, log_f)
    o_m = (_head_rms_norm(o_m, m_norm) * jax.nn.sigmoid(m_o.astype(f32))).astype(h.dtype)
    merged = jax.nn.sigmoid(a_gla) * (o_gla @ gla_proj) + jax.nn.sigmoid(a_m) * (o_m @ m_proj)
    return merged @ w_out


def cross_attention(h, mem_n, wq, wkv, wo):
    b_, s_, d_ = h.shape
    dh = d_ // X_HEADS
    q = _heads(h @ wq, X_HEADS)
    k, v = jnp.split(mem_n @ wkv, 2, axis=-1)
    k = _heads(k, X_HEADS)
    v = _heads(v, X_HEADS)
    scores = jnp.einsum('bshd,bmhd->bhsm', q, k).astype(jnp.float32) * dh ** -0.5
    p = jax.nn.softmax(scores, axis=-1).astype(v.dtype)
    o = jnp.einsum('bhsm,bmhd->bshd', p, v).reshape(b_, s_, d_)
    return o @ wo


def swiglu(h, w13, w2):
    gate, up = jnp.split(h @ w13, 2, axis=-1)
    return (jax.nn.silu(gate) * up) @ w2


def moe_swiglu(h, router, w13, w2):
    logits = (h @ router).astype(jnp.float32)
    top_val, top_idx = lax.top_k(logits, TOP_K)
    top_w = jax.nn.softmax(top_val, axis=-1)
    gate = jnp.sum(jax.nn.one_hot(top_idx, N_EXPERTS, dtype=jnp.float32) * top_w[..., None],
                   axis=-2).astype(h.dtype)
    out = jnp.zeros_like(h)
    for e in range(N_EXPERTS):
        out = out + gate[..., e:e + 1] * swiglu(h, w13[e], w2[e])
    return out


def setup_inputs(seed: int = 0) -> dict:
    key = jax.random.key(seed)
    ks = jax.random.split(key, 26)
    f32 = jnp.float32
    L = DEPTH

    def nrm(k, shape, fan_in):
        return jax.random.normal(k, shape, f32) * fan_in ** -0.5

    def gain(k, shape):
        return 1.0 + 0.02 * jax.random.normal(k, shape, f32)

    i_bias = 0.1 * jax.random.normal(ks[24], (L, M_HEADS), f32)
    f_bias = jnp.linspace(3.0, 6.0, M_HEADS, dtype=f32) + 0.1 * jax.random.normal(ks[25], (L, M_HEADS), f32)
    return {
        'x': jax.random.normal(ks[0], (BATCH, SEQ, D_MODEL), f32),
        'mem': jax.random.normal(ks[1], (BATCH, MEM_LEN, D_MODEL), f32),
        'norm_mix': gain(ks[2], (L, D_MODEL)),
        'w_in': nrm(ks[3], (L, D_MODEL, N_IN), D_MODEL),
        'gla_gk_up': nrm(ks[4], (L, GLA_RANK, GLA_KEY), GLA_RANK),
        'gla_gk_bias': 0.1 * jax.random.normal(ks[5], (L, GLA_KEY), f32),
        'gla_norm': gain(ks[6], (L, GLA_VAL)),
        'm_conv': nrm(ks[7], (L, CONV_W, 2 * M_KEY), CONV_W),
        'm_gate_bias': jnp.concatenate([i_bias, f_bias], axis=-1),
        'm_norm': gain(ks[8], (L, M_VAL)),
        'gla_proj': nrm(ks[9], (L, GLA_VAL, D_MODEL), GLA_VAL),
        'm_proj': nrm(ks[10], (L, M_VAL, D_MODEL), M_VAL),
        'w_out': nrm(ks[11], (L, D_MODEL, D_MODEL), D_MODEL),
        'norm_xattn': gain(ks[12], (L, D_MODEL)),
        'norm_mem': gain(ks[13], (L, D_MODEL)),
        'x_wq': nrm(ks[14], (L, D_MODEL, D_MODEL), D_MODEL),
        'x_wkv': nrm(ks[15], (L, D_MODEL, 2 * D_MODEL), D_MODEL),
        'x_wo': nrm(ks[16], (L, D_MODEL, D_MODEL), D_MODEL),
        'norm_ffn': gain(ks[17], (L, D_MODEL)),
        'ffn_w13': nrm(ks[18], (N_DENSE, D_MODEL, 2 * FFN_DIM), D_MODEL),
        'ffn_w2': nrm(ks[19], (N_DENSE, FFN_DIM, D_MODEL), FFN_DIM),
        'moe_router': nrm(ks[20], (N_MOE, D_MODEL, N_EXPERTS), D_MODEL),
        'moe_w13': nrm(ks[21], (N_MOE, N_EXPERTS, D_MODEL, 2 * EXPERT_DIM), D_MODEL),
        'moe_w2': nrm(ks[22], (N_MOE, N_EXPERTS, EXPERT_DIM, D_MODEL), EXPERT_DIM),
        'norm_final': gain(ks[23], (D_MODEL,)),
    }


def reference(x, mem, norm_mix, w_in, gla_gk_up, gla_gk_bias, gla_norm, m_conv, m_gate_bias, m_norm,
              gla_proj, m_proj, w_out, norm_xattn, norm_mem, x_wq, x_wkv, x_wo, norm_ffn,
              ffn_w13, ffn_w2, moe_router, moe_w13, moe_w2, norm_final):
    for l in range(DEPTH):
        h = rms_norm(x, norm_mix[l])
        x = x + hybrid_mixer(h, w_in[l], gla_gk_up[l], gla_gk_bias[l], gla_norm[l], m_conv[l],
                             m_gate_bias[l], m_norm[l], gla_proj[l], m_proj[l], w_out[l])
        h = rms_norm(x, norm_xattn[l])
        mem_n = rms_norm(mem, norm_mem[l])
        x = x + cross_attention(h, mem_n, x_wq[l], x_wkv[l], x_wo[l])
        h = rms_norm(x, norm_ffn[l])
        if l % 2 == 0:
            x = x + swiglu(h, ffn_w13[l // 2], ffn_w2[l // 2])
        else:
            x = x + moe_swiglu(h, moe_router[l // 2], moe_w13[l // 2], moe_w2[l // 2])
    return rms_norm(x, norm_final)
```

```python
import functools

import jax
import jax.numpy as jnp
from jax import lax
from jax.experimental import pallas as pl
from jax.experimental.pallas import tpu as pltpu

F32 = jnp.float32
BF16 = jnp.bfloat16

EPS = 1e-6
D_MODEL = 1024
GLA_HEADS = 4
GLA_DK = 128
GLA_DV = 256
GLA_RANK = 16
GLA_GATE_NORM = 16.0
GLA_LOG_DECAY_MIN = -1.0
GLA_CHUNK = 64
M_HEADS = 4
M_DK = 128
M_DV = 256
M_CHUNK = 128
CONV_W = 4
X_HEADS = 4
N_EXPERTS = 8
LANES = 128
CONV_HALO = 8
VMEM_LIMIT = 48 * 1024 * 1024

Z_GQ, Z_GK, Z_GV, Z_GG = 0, 512, 1024, 2048
Z_MQ, Z_MK, Z_MV, Z_MO = 3072, 3584, 4096, 5120
Z_AG, Z_AM = 6144, 7168
Z_COLS = 8192
ZS_COLS = LANES * (1 + M_HEADS)


def _params(*sem):
    return pltpu.CompilerParams(dimension_semantics=sem, vmem_limit_bytes=VMEM_LIMIT)


def _rms(x, g):
    return x * lax.rsqrt(jnp.mean(x * x, axis=-1, keepdims=True) + EPS) * g


def _log_sigmoid(u):
    return jnp.minimum(u, 0.0) - jnp.log1p(jnp.exp(-jnp.abs(u)))


def _sigmoid(u):
    return 1.0 / (1.0 + jnp.exp(-u))


def _dot(a, b):
    return jnp.dot(a, b, preferred_element_type=F32)


def _dot_nt(a, b):
    return lax.dot_general(a, b, (((1,), (1,)), ((), ())), preferred_element_type=F32)


def _dot_tn(a, b):
    return lax.dot_general(a, b, (((0,), (0,)), ((), ())), preferred_element_type=F32)


def _cumsum_rows(x):
    n = x.shape[0]
    row = lax.broadcasted_iota(jnp.int32, x.shape, 0)
    s = 1
    while s < n:
        x = x + jnp.where(row >= s, pltpu.roll(x, s, 0), 0.0)
        s *= 2
    return x


def _norm_matmul_kernel(*refs, with_small):
    if with_small:
        x_ref, g_ref, w_ref, ws_ref, z_ref, zs_ref, h_scr = refs
    else:
        x_ref, g_ref, w_ref, z_ref, h_scr = refs

    @pl.when(pl.program_id(1) == 0)
    def _():
        h = _rms(x_ref[...], g_ref[...]).astype(BF16)
        h_scr[...] = h
        if with_small:
            zs_ref[...] = _dot(h, ws_ref[...])

    z_ref[...] = _dot(h_scr[...], w_ref[...]).astype(z_ref.dtype)


def _norm_matmul(x, g, w, ws=None, *, tm, tn):
    t, d = x.shape
    n = w.shape[1]
    in_specs = [pl.BlockSpec((tm, d), lambda i, j: (i, 0)),
                pl.BlockSpec((1, d), lambda i, j: (0, 0)),
                pl.BlockSpec((d, tn), lambda i, j: (0, j))]
    out_specs = [pl.BlockSpec((tm, tn), lambda i, j: (i, j))]
    out_shape = [jax.ShapeDtypeStruct((t, n), BF16)]
    args = [x, g, w]
    if ws is not None:
        ns = ws.shape[1]
        in_specs.append(pl.BlockSpec((d, ns), lambda i, j: (0, 0)))
        out_specs.append(pl.BlockSpec((tm, ns), lambda i, j: (i, 0)))
        out_shape.append(jax.ShapeDtypeStruct((t, ns), F32))
        args.append(ws)
    out = pl.pallas_call(
        functools.partial(_norm_matmul_kernel, with_small=ws is not None),
        grid=(t // tm, n // tn),
        in_specs=in_specs, out_specs=out_specs, out_shape=out_shape,
        scratch_shapes=[pltpu.VMEM((tm, d), BF16)],
        compiler_params=_params("parallel", "arbitrary"),
        name="norm_matmul",
    )(*args)
    return out if ws is not None else out[0]


def _gla_kernel(q_ref, k_ref, v_ref, gg_ref, lr_ref, up_ref, gb_ref, nw_ref, o_ref, la_scr, st_scr):
    seq = q_ref.shape[0]
    c = GLA_CHUNK
    scale = GLA_DK ** -0.5

    u = _dot(lr_ref[...].astype(BF16), up_ref[...]) + gb_ref[...]
    la_scr[...] = jnp.maximum(_log_sigmoid(u) * (1.0 / GLA_GATE_NORM), GLA_LOG_DECAY_MIN)
    st_scr[...] = jnp.zeros_like(st_scr)
    causal = (lax.broadcasted_iota(jnp.int32, (c, c), 0) >= lax.broadcasted_iota(jnp.int32, (c, c), 1))

    def body(n, carry):
        r0 = pl.multiple_of(n * c, c)
        rows = pl.ds(r0, c)
        cum = _cumsum_rows(la_scr[rows, :])
        cum_last = cum[c - 1:c, :]
        q = q_ref[rows, :].astype(F32) * scale
        k = k_ref[rows, :].astype(F32)
        v = v_ref[rows, :]
        q_dec = (q * jnp.exp(cum)).astype(BF16)
        k_inv = (k * jnp.exp(-cum)).astype(BF16)
        k_end = (k * jnp.exp(cum_last - cum)).astype(BF16)
        scores = jnp.where(causal, _dot_nt(q_dec, k_inv), 0.0).astype(BF16)
        st = st_scr[...]
        o = _dot(scores, v) + _dot_nt(q_dec, st.astype(BF16))
        st_scr[...] = st * jnp.exp(cum_last) + _dot_tn(v, k_end)
        gate = gg_ref[rows, :].astype(F32)
        o_ref[rows, :] = (_rms(o, nw_ref[...]) * (gate * _sigmoid(gate))).astype(o_ref.dtype)
        return carry

    lax.fori_loop(0, seq // c, body, 0)


def _gla(z3, zs3, up, gb, nw):
    b, s, _ = z3.shape
    qb, kb, vb, gb_ = Z_GQ // GLA_DK, Z_GK // GLA_DK, Z_GV // GLA_DV, Z_GG // GLA_DV
    return pl.pallas_call(
        _gla_kernel,
        grid=(b, GLA_HEADS),
        in_specs=[pl.BlockSpec((None, s, GLA_DK), lambda i, h: (i, 0, qb + h)),
                  pl.BlockSpec((None, s, GLA_DK), lambda i, h: (i, 0, kb + h)),
                  pl.BlockSpec((None, s, GLA_DV), lambda i, h: (i, 0, vb + h)),
                  pl.BlockSpec((None, s, GLA_DV), lambda i, h: (i, 0, gb_ + h)),
                  pl.BlockSpec((None, s, LANES), lambda i, h: (i, 0, 0)),
                  pl.BlockSpec((LANES, GLA_DK), lambda i, h: (0, h)),
                  pl.BlockSpec((1, GLA_DK), lambda i, h: (0, h)),
                  pl.BlockSpec((1, GLA_DV), lambda i, h: (0, h))],
        out_specs=pl.BlockSpec((None, s, GLA_DV), lambda i, h: (i, 0, h)),
        out_shape=jax.ShapeDtypeStruct((b, s, GLA_HEADS * GLA_DV), BF16),
        scratch_shapes=[pltpu.VMEM((s, GLA_DK), F32), pltpu.VMEM((GLA_DV, GLA_DK), F32)],
        compiler_params=_params("parallel", "arbitrary"),
        name="gla",
    )(z3, z3, z3, z3, zs3, up, gb, nw)


def _mlstm_kernel(q_ref, k_ref, v_ref, og_ref, gt_ref, cq_ref, ck_ref, gb_ref, nw_ref, o_ref,
                  qp_scr, kp_scr, c_scr, n_scr, m_scr):
    seq = q_ref.shape[0]
    c = M_CHUNK
    scale = M_DK ** -0.5

    zero_halo = jnp.zeros((CONV_HALO, M_DK), F32)
    qp_scr[0:CONV_HALO, :] = zero_halo
    kp_scr[0:CONV_HALO, :] = zero_halo
    qp_scr[CONV_HALO:CONV_HALO + seq, :] = q_ref[...].astype(F32)
    kp_scr[CONV_HALO:CONV_HALO + seq, :] = k_ref[...].astype(F32)
    c_scr[...] = jnp.zeros_like(c_scr)
    n_scr[...] = jnp.zeros_like(n_scr)
    m_scr[...] = jnp.zeros_like(m_scr)
    causal = (lax.broadcasted_iota(jnp.int32, (c, c), 0) >= lax.broadcasted_iota(jnp.int32, (c, c), 1))

    def conv_silu(p_scr, w_ref, r0):
        blk = p_scr[pl.ds(r0, c + CONV_HALO), :]
        acc = None
        for j in range(CONV_W):
            lo = CONV_HALO - (CONV_W - 1) + j
            term = w_ref[j:j + 1, :] * blk[lo:lo + c, :]
            acc = term if acc is None else acc + term
        return acc * _sigmoid(acc)

    def body(n, carry):
        r0 = pl.multiple_of(n * c, c)
        rows = pl.ds(r0, c)
        q = conv_silu(qp_scr, cq_ref, r0) * scale
        k = conv_silu(kp_scr, ck_ref, r0)
        v = v_ref[rows, :]
        qb = q.astype(BF16)

        gt = gt_ref[rows, :] + gb_ref[...]
        bcum = _cumsum_rows(_log_sigmoid(gt))
        li_col = gt[:, 0:1]
        b_col = bcum[:, 1:2]
        a_col = li_col - b_col
        a_row = gt.T[0:1, :] - bcum.T[1:2, :]
        b_last = b_col[c - 1:c, :]
        m_prev = m_scr[:, 0:1]
        c_prev = c_scr[...]
        n_prev = n_scr[...]

        dmat = jnp.where(causal, b_col + a_row, -jnp.inf)
        m_inter = b_col + m_prev
        m_t = jnp.maximum(m_inter, jnp.max(dmat, axis=1, keepdims=True))
        w_ts = jnp.exp(dmat - m_t) * _dot_nt(qb, k.astype(BF16))
        s_inter = jnp.exp(m_inter - m_t)
        num = _dot(w_ts.astype(BF16), v) + s_inter * _dot(qb, c_prev.astype(BF16))
        den = (jnp.sum(w_ts, axis=1, keepdims=True)
               + s_inter * jnp.sum(q * n_prev, axis=1, keepdims=True))
        h = num / jnp.maximum(jnp.abs(den), jnp.exp(-m_t))

        m_new = jnp.maximum(b_last + m_prev, b_last + jnp.max(a_col, axis=0, keepdims=True))
        kw = k * jnp.exp(b_last + a_col - m_new)
        carry_scale = jnp.exp(b_last + m_prev - m_new)
        c_scr[...] = carry_scale * c_prev + _dot_tn(kw.astype(BF16), v)
        n_scr[...] = carry_scale * n_prev + jnp.sum(kw, axis=0, keepdims=True)
        m_scr[...] = jnp.broadcast_to(m_new, m_scr.shape)

        gate = og_ref[rows, :].astype(F32)
        o_ref[rows, :] = (_rms(h, nw_ref[...]) * _sigmoid(gate)).astype(o_ref.dtype)
        return carry

    lax.fori_loop(0, seq // c, body, 0)


def _mlstm(z3, zs3, conv, gbias, nw):
    b, s, _ = z3.shape
    qb, kb, vb, ob = Z_MQ // M_DK, Z_MK // M_DK, Z_MV // M_DV, Z_MO // M_DV
    return pl.pallas_call(
        _mlstm_kernel,
        grid=(b, M_HEADS),
        in_specs=[pl.BlockSpec((None, s, M_DK), lambda i, h: (i, 0, qb + h)),
                  pl.BlockSpec((None, s, M_DK), lambda i, h: (i, 0, kb + h)),
                  pl.BlockSpec((None, s, M_DV), lambda i, h: (i, 0, vb + h)),
                  pl.BlockSpec((None, s, M_DV), lambda i, h: (i, 0, ob + h)),
                  pl.BlockSpec((None, s, LANES), lambda i, h: (i, 0, 1 + h)),
                  pl.BlockSpec((CONV_W, M_DK), lambda i, h: (0, h)),
                  pl.BlockSpec((CONV_W, M_DK), lambda i, h: (0, M_HEADS + h)),
                  pl.BlockSpec((None, 1, LANES), lambda i, h: (h, 0, 0)),
                  pl.BlockSpec((1, M_DV), lambda i, h: (0, h))],
        out_specs=pl.BlockSpec((None, s, M_DV), lambda i, h: (i, 0, h)),
        out_shape=jax.ShapeDtypeStruct((b, s, M_HEADS * M_DV), BF16),
        scratch_shapes=[pltpu.VMEM((s + CONV_HALO, M_DK), F32), pltpu.VMEM((s + CONV_HALO, M_DK), F32),
                        pltpu.VMEM((M_DK, M_DV), F32), pltpu.VMEM((1, M_DK), F32), pltpu.VMEM((1, LANES), F32)],
        compiler_params=_params("parallel", "arbitrary"),
        name="mlstm",
    )(z3, z3, z3, z3, zs3, conv, conv, gbias, nw)


def _merge_kernel(x_ref, og_ref, om_ref, ag_ref, am_ref, wg_ref, wm_ref, wo_ref, o_ref):
    merged = (_sigmoid(ag_ref[...].astype(F32)) * _dot(og_ref[...], wg_ref[...])
              + _sigmoid(am_ref[...].astype(F32)) * _dot(om_ref[...], wm_ref[...]))
    o_ref[...] = x_ref[...] + _dot(merged.astype(BF16), wo_ref[...])


def _merge(x, o_gla, o_m, z, wg, wm, wo, *, tm):
    t, d = x.shape
    row = lambda i: (i, 0)
    full = lambda i: (0, 0)
    return pl.pallas_call(
        _merge_kernel,
        grid=(t // tm,),
        in_specs=[pl.BlockSpec((tm, d), row), pl.BlockSpec((tm, d), row), pl.BlockSpec((tm, d), row),
                  pl.BlockSpec((tm, d), lambda i: (i, Z_AG // D_MODEL)),
                  pl.BlockSpec((tm, d), lambda i: (i, Z_AM // D_MODEL)),
                  pl.BlockSpec((d, d), full), pl.BlockSpec((d, d), full), pl.BlockSpec((d, d), full)],
        out_specs=pl.BlockSpec((tm, d), row),
        out_shape=jax.ShapeDtypeStruct((t, d), F32),
        compiler_params=_params("parallel"),
        name="merge",
    )(x, o_gla, o_m, z, z, wg, wm, wo)


def _xattn_kernel(x_ref, g_ref, wq_ref, kv_ref, wo_ref, o_ref):
    d = x_ref.shape[1]
    dh = d // X_HEADS
    x = x_ref[...]
    q = _dot(_rms(x, g_ref[...]).astype(BF16), wq_ref[...]).astype(BF16)
    heads = []
    for h in range(X_HEADS):
        k = kv_ref[:, h * dh:(h + 1) * dh]
        v = kv_ref[:, d + h * dh:d + (h + 1) * dh]
        s = _dot_nt(q[:, h * dh:(h + 1) * dh], k) * dh ** -0.5
        p = jnp.exp(s - jnp.max(s, axis=-1, keepdims=True))
        p = p / jnp.sum(p, axis=-1, keepdims=True)
        heads.append(_dot(p.astype(BF16), v).astype(BF16))
    o_ref[...] = x + _dot(jnp.concatenate(heads, axis=-1), wo_ref[...])


def _xattn(x3, g, wq, kv, wo, *, tq):
    b, s, d = x3.shape
    m = kv.shape[1]
    return pl.pallas_call(
        _xattn_kernel,
        grid=(b, s // tq),
        in_specs=[pl.BlockSpec((None, tq, d), lambda i, j: (i, j, 0)),
                  pl.BlockSpec((1, d), lambda i, j: (0, 0)),
                  pl.BlockSpec((d, d), lambda i, j: (0, 0)),
                  pl.BlockSpec((None, m, 2 * d), lambda i, j: (i, 0, 0)),
                  pl.BlockSpec((d, d), lambda i, j: (0, 0))],
        out_specs=pl.BlockSpec((None, tq, d), lambda i, j: (i, j, 0)),
        out_shape=jax.ShapeDtypeStruct((b, s, d), F32),
        compiler_params=_params("parallel", "parallel"),
        name="xattn",
    )(x3, g, wq, kv, wo)


def _top2_gates(logits):
    lane = lax.broadcasted_iota(jnp.int32, logits.shape, 1)
    m1 = jnp.max(logits, axis=-1, keepdims=True)
    i1 = jnp.min(jnp.where(logits == m1, lane, LANES), axis=-1, keepdims=True)
    rest = jnp.where(lane == i1, -jnp.inf, logits)
    m2 = jnp.max(rest, axis=-1, keepdims=True)
    i2 = jnp.min(jnp.where(rest == m2, lane, LANES), axis=-1, keepdims=True)
    e2 = jnp.exp(m2 - m1)
    w1 = 1.0 / (1.0 + e2)
    w2 = e2 / (1.0 + e2)
    return jnp.where(lane == i1, w1, 0.0) + jnp.where(lane == i2, w2, 0.0)


def _ffn_kernel(*refs, moe, final_norm):
    refs = list(refs)
    x_ref, g_ref = refs[:2]
    pos = 2
    if moe:
        router_ref = refs[pos]
        pos += 1
    w1_ref, w3_ref, w2_ref = refs[pos:pos + 3]
    pos += 3
    if final_norm:
        gf_ref = refs[pos]
        pos += 1
    o_ref, h_scr, acc_scr = refs[pos:pos + 3]
    pos += 3
    if moe:
        gate_scr = refs[pos]

    e = pl.program_id(1)
    j = pl.program_id(2)
    first = jnp.logical_and(e == 0, j == 0)
    last = jnp.logical_and(e == pl.num_programs(1) - 1, j == pl.num_programs(2) - 1)

    @pl.when(first)
    def _():
        h = _rms(x_ref[...], g_ref[...])
        h_scr[...] = h.astype(BF16)
        acc_scr[...] = jnp.zeros_like(acc_scr)
        if moe:
            logits = jnp.dot(h, router_ref[...], preferred_element_type=F32, precision=lax.Precision.HIGHEST)
            lane = lax.broadcasted_iota(jnp.int32, logits.shape, 1)
            gate_scr[...] = _top2_gates(jnp.where(lane < N_EXPERTS, logits, -jnp.inf))

    h = h_scr[...]
    a = _dot(h, w1_ref[...])
    act = (a * _sigmoid(a) * _dot(h, w3_ref[...])).astype(BF16)
    y = _dot(act, w2_ref[...])
    if moe:
        lane = lax.broadcasted_iota(jnp.int32, gate_scr.shape, 1)
        y = y * jnp.sum(jnp.where(lane == e, gate_scr[...], 0.0), axis=-1, keepdims=True)
    acc_scr[...] += y

    @pl.when(last)
    def _():
        out = x_ref[...] + acc_scr[...]
        if final_norm:
            out = _rms(out, gf_ref[...])
        o_ref[...] = out


def _ffn(x, g, w1, w3, w2, *, w3_block_offset, tm, tf, router=None, g_final=None):
    t, d = x.shape
    n_e, f, _ = w2.shape
    moe = router is not None
    final_norm = g_final is not None
    in_specs = [pl.BlockSpec((tm, d), lambda i, e, j: (i, 0)),
                pl.BlockSpec((1, d), lambda i, e, j: (0, 0))]
    args = [x, g]
    if moe:
        in_specs.append(pl.BlockSpec((d, LANES), lambda i, e, j: (0, 0)))
        args.append(router)
    in_specs += [pl.BlockSpec((None, d, tf), lambda i, e, j: (e, 0, j)),
                 pl.BlockSpec((None, d, tf), lambda i, e, j: (e, 0, w3_block_offset + j)),
                 pl.BlockSpec((None, tf, d), lambda i, e, j: (e, j, 0))]
    args += [w1, w3, w2]
    if final_norm:
        in_specs.append(pl.BlockSpec((1, d), lambda i, e, j: (0, 0)))
        args.append(g_final)
    scratch = [pltpu.VMEM((tm, d), BF16), pltpu.VMEM((tm, d), F32)]
    if moe:
        scratch.append(pltpu.VMEM((tm, LANES), F32))
    return pl.pallas_call(
        functools.partial(_ffn_kernel, moe=moe, final_norm=final_norm),
        grid=(t // tm, n_e, f // tf),
        in_specs=in_specs,
        out_specs=pl.BlockSpec((tm, d), lambda i, e, j: (i, 0)),
        out_shape=jax.ShapeDtypeStruct((t, d), F32),
        scratch_shapes=scratch,
        compiler_params=_params("parallel", "arbitrary", "arbitrary"),
        name="moe_ffn" if moe else "dense_ffn",
    )(*args)


def _split_w_in(w_in):
    sizes = (512, 512, 1024, GLA_RANK, 1024, 512, 512, 1024, M_HEADS, M_HEADS, 1024, 1024, 1024)
    parts, off = [], 0
    for n in sizes:
        parts.append(w_in[:, off:off + n])
        off += n
    g_q, g_k, g_v, g_lr, g_g, m_q, m_k, m_v, m_i, m_f, m_o, a_g, a_m = parts
    wz = jnp.concatenate([g_q, g_k, g_v, g_g, m_q, m_k, m_v, m_o, a_g, a_m], axis=1).astype(BF16)
    d = w_in.shape[0]
    groups = [jnp.pad(g_lr, ((0, 0), (0, LANES - GLA_RANK)))]
    for h in range(M_HEADS):
        pair = jnp.concatenate([m_i[:, h:h + 1], m_f[:, h:h + 1]], axis=1)
        groups.append(jnp.pad(pair, ((0, 0), (0, LANES - 2))))
    ws = jnp.concatenate(groups, axis=1).astype(BF16)
    assert wz.shape == (d, Z_COLS) and ws.shape == (d, ZS_COLS)
    return wz, ws


def _pick(total, want):
    t = min(total, want)
    while total % t:
        t -= 1
    return t


def kernel(x, mem, norm_mix, w_in, gla_gk_up, gla_gk_bias, gla_norm, m_conv, m_gate_bias, m_norm, gla_proj, m_proj, w_out, norm_xattn, norm_mem, x_wq, x_wkv, x_wo, norm_ffn, ffn_w13, ffn_w2, moe_router, moe_w13, moe_w2, norm_final):
    b, s, d = x.shape
    m_len = mem.shape[1]
    t = b * s
    depth = norm_mix.shape[0]
    tm = _pick(t, 512)
    row2 = lambda a: a.reshape(1, -1)

    xt = x.reshape(t, d)
    mem_t = mem.reshape(b * m_len, d)
    for l in range(depth):
        last_layer = l == depth - 1
        wz, ws = _split_w_in(w_in[l])
        z, zs = _norm_matmul(xt, row2(norm_mix[l]), wz, ws, tm=tm, tn=1024)
        z3 = z.reshape(b, s, Z_COLS)
        zs3 = zs.reshape(b, s, ZS_COLS)
        up = jnp.pad(gla_gk_up[l], ((0, LANES - GLA_RANK), (0, 0))).astype(BF16)
        o_gla = _gla(z3, zs3, up, row2(gla_gk_bias[l]), row2(gla_norm[l]))
        gbias = jnp.pad(jnp.stack([m_gate_bias[l, :M_HEADS], m_gate_bias[l, M_HEADS:]], axis=1),
                        ((0, 0), (0, LANES - 2))).reshape(M_HEADS, 1, LANES)
        o_m = _mlstm(z3, zs3, m_conv[l], gbias, row2(m_norm[l]))
        xt = _merge(xt, o_gla.reshape(t, d), o_m.reshape(t, d), z,
                    gla_proj[l].astype(BF16), m_proj[l].astype(BF16), w_out[l].astype(BF16), tm=tm)
        kv = _norm_matmul(mem_t, row2(norm_mem[l]), x_wkv[l].astype(BF16), tm=_pick(b * m_len, 512), tn=1024)
        xt = _xattn(xt.reshape(b, s, d), row2(norm_xattn[l]), x_wq[l].astype(BF16),
                    kv.reshape(b, m_len, 2 * d), x_wo[l].astype(BF16), tq=_pick(s, 512)).reshape(t, d)
        g_final = row2(norm_final) if last_layer else None
        if l % 2 == 0:
            w13, w2 = ffn_w13[l // 2], ffn_w2[l // 2]
            f = w2.shape[0]
            fp = -(-f // (2 * LANES)) * (2 * LANES)
            w1 = jnp.pad(w13[:, :f], ((0, 0), (0, fp - f))).astype(BF16)[None]
            w3 = jnp.pad(w13[:, f:], ((0, 0), (0, fp - f))).astype(BF16)[None]
            w2p = jnp.pad(w2, ((0, fp - f), (0, 0))).astype(BF16)[None]
            xt = _ffn(xt, row2(norm_ffn[l]), w1, w3, w2p, w3_block_offset=0, tm=tm, tf=fp // 2, g_final=g_final)
        else:
            w13 = moe_w13[l // 2].astype(BF16)
            w2 = moe_w2[l // 2].astype(BF16)
            f = w2.shape[1]
            tf = f // 4
            router = jnp.pad(moe_router[l // 2], ((0, 0), (0, LANES - N_EXPERTS)))
            xt = _ffn(xt, row2(norm_ffn[l]), w13, w13, w2, w3_block_offset=f // tf, tm=tm, tf=tf,
                      router=router, g_final=g_final)
    return xt.reshape(b, s, d)
```

```python
import functools

import jax
import jax.numpy as jnp
from jax import lax
from jax.experimental import pallas as pl
from jax.experimental.pallas import tpu as pltpu
from jax.experimental.pallas import tpu_sc as plsc

F32 = jnp.float32
BF16 = jnp.bfloat16

EPS = 1e-6
D_MODEL = 1024
GLA_HEADS = 4
GLA_DK = 128
GLA_DV = 256
GLA_RANK = 16
GLA_GATE_NORM = 16.0
GLA_LOG_DECAY_MIN = -1.0
GLA_CHUNK = 64
M_HEADS = 4
M_DK = 128
M_DV = 256
M_CHUNK = 128
CONV_W = 4
X_HEADS = 4
N_EXPERTS = 8
LANES = 128
CONV_HALO = 8
VMEM_LIMIT = 48 * 1024 * 1024
SC_GATHER_WINDOW = 128
PACK_HALVES = 2
PACK_W = D_MODEL // 2 // PACK_HALVES

Z_GQ, Z_GK, Z_GV, Z_GG = 0, 512, 1024, 2048
Z_MQ, Z_MK, Z_MV, Z_MO = 3072, 3584, 4096, 5120
Z_AG, Z_AM = 6144, 7168
Z_COLS = 8192
ZS_COLS = LANES * (1 + M_HEADS)


def _params(*sem):
    return pltpu.CompilerParams(dimension_semantics=sem, vmem_limit_bytes=VMEM_LIMIT)


def _rms(x, g):
    return x * lax.rsqrt(jnp.mean(x * x, axis=-1, keepdims=True) + EPS) * g


def _log_sigmoid(u):
    return jnp.minimum(u, 0.0) - jnp.log1p(jnp.exp(-jnp.abs(u)))


def _sigmoid(u):
    return 1.0 / (1.0 + jnp.exp(-u))


def _dot(a, b):
    return jnp.dot(a, b, preferred_element_type=F32)


def _dot_nt(a, b):
    return lax.dot_general(a, b, (((1,), (1,)), ((), ())), preferred_element_type=F32)


def _dot_tn(a, b):
    return lax.dot_general(a, b, (((0,), (0,)), ((), ())), preferred_element_type=F32)


def _cumsum_rows(x):
    n = x.shape[0]
    row = lax.broadcasted_iota(jnp.int32, x.shape, 0)
    s = 1
    while s < n:
        x = x + jnp.where(row >= s, pltpu.roll(x, s, 0), 0.0)
        s *= 2
    return x


def _norm_matmul_kernel(*refs, with_small):
    if with_small:
        x_ref, g_ref, w_ref, ws_ref, z_ref, zs_ref, h_scr = refs
    else:
        x_ref, g_ref, w_ref, z_ref, h_scr = refs

    @pl.when(pl.program_id(1) == 0)
    def _():
        h = _rms(x_ref[...], g_ref[...]).astype(BF16)
        h_scr[...] = h
        if with_small:
            zs_ref[...] = _dot(h, ws_ref[...])

    z_ref[...] = _dot(h_scr[...], w_ref[...]).astype(z_ref.dtype)


def _norm_matmul(x, g, w, ws=None, *, tm, tn):
    t, d = x.shape
    n = w.shape[1]
    in_specs = [pl.BlockSpec((tm, d), lambda i, j: (i, 0)),
                pl.BlockSpec((1, d), lambda i, j: (0, 0)),
                pl.BlockSpec((d, tn), lambda i, j: (0, j))]
    out_specs = [pl.BlockSpec((tm, tn), lambda i, j: (i, j))]
    out_shape = [jax.ShapeDtypeStruct((t, n), BF16)]
    args = [x, g, w]
    if ws is not None:
        ns = ws.shape[1]
        in_specs.append(pl.BlockSpec((d, ns), lambda i, j: (0, 0)))
        out_specs.append(pl.BlockSpec((tm, ns), lambda i, j: (i, 0)))
        out_shape.append(jax.ShapeDtypeStruct((t, ns), F32))
        args.append(ws)
    out = pl.pallas_call(
        functools.partial(_norm_matmul_kernel, with_small=ws is not None),
        grid=(t // tm, n // tn),
        in_specs=in_specs, out_specs=out_specs, out_shape=out_shape,
        scratch_shapes=[pltpu.VMEM((tm, d), BF16)],
        compiler_params=_params("parallel", "arbitrary"),
        name="norm_matmul",
    )(*args)
    return out if ws is not None else out[0]


def _gla_kernel(q_ref, k_ref, v_ref, gg_ref, lr_ref, up_ref, gb_ref, nw_ref, o_ref, la_scr, st_scr):
    seq = q_ref.shape[0]
    c = GLA_CHUNK
    scale = GLA_DK ** -0.5

    u = _dot(lr_ref[...].astype(BF16), up_ref[...]) + gb_ref[...]
    la_scr[...] = jnp.maximum(_log_sigmoid(u) * (1.0 / GLA_GATE_NORM), GLA_LOG_DECAY_MIN)
    st_scr[...] = jnp.zeros_like(st_scr)
    causal = (lax.broadcasted_iota(jnp.int32, (c, c), 0) >= lax.broadcasted_iota(jnp.int32, (c, c), 1))

    def body(n, carry):
        r0 = pl.multiple_of(n * c, c)
        rows = pl.ds(r0, c)
        cum = _cumsum_rows(la_scr[rows, :])
        cum_last = cum[c - 1:c, :]
        q = q_ref[rows, :].astype(F32) * scale
        k = k_ref[rows, :].astype(F32)
        v = v_ref[rows, :]
        q_dec = (q * jnp.exp(cum)).astype(BF16)
        k_inv = (k * jnp.exp(-cum)).astype(BF16)
        k_end = (k * jnp.exp(cum_last - cum)).astype(BF16)
        scores = jnp.where(causal, _dot_nt(q_dec, k_inv), 0.0).astype(BF16)
        st = st_scr[...]
        o = _dot(scores, v) + _dot_nt(q_dec, st.astype(BF16))
        st_scr[...] = st * jnp.exp(cum_last) + _dot_tn(v, k_end)
        gate = gg_ref[rows, :].astype(F32)
        o_ref[rows, :] = (_rms(o, nw_ref[...]) * (gate * _sigmoid(gate))).astype(o_ref.dtype)
        return carry

    lax.fori_loop(0, seq // c, body, 0)


def _gla(z3, zs3, up, gb, nw):
    b, s, _ = z3.shape
    qb, kb, vb, gb_ = Z_GQ // GLA_DK, Z_GK // GLA_DK, Z_GV // GLA_DV, Z_GG // GLA_DV
    return pl.pallas_call(
        _gla_kernel,
        grid=(b, GLA_HEADS),
        in_specs=[pl.BlockSpec((None, s, GLA_DK), lambda i, h: (i, 0, qb + h)),
                  pl.BlockSpec((None, s, GLA_DK), lambda i, h: (i, 0, kb + h)),
                  pl.BlockSpec((None, s, GLA_DV), lambda i, h: (i, 0, vb + h)),
                  pl.BlockSpec((None, s, GLA_DV), lambda i, h: (i, 0, gb_ + h)),
                  pl.BlockSpec((None, s, LANES), lambda i, h: (i, 0, 0)),
                  pl.BlockSpec((LANES, GLA_DK), lambda i, h: (0, h)),
                  pl.BlockSpec((1, GLA_DK), lambda i, h: (0, h)),
                  pl.BlockSpec((1, GLA_DV), lambda i, h: (0, h))],
        out_specs=pl.BlockSpec((None, s, GLA_DV), lambda i, h: (i, 0, h)),
        out_shape=jax.ShapeDtypeStruct((b, s, GLA_HEADS * GLA_DV), BF16),
        scratch_shapes=[pltpu.VMEM((s, GLA_DK), F32), pltpu.VMEM((GLA_DV, GLA_DK), F32)],
        compiler_params=_params("parallel", "arbitrary"),
        name="gla",
    )(z3, z3, z3, z3, zs3, up, gb, nw)


def _mlstm_kernel(q_ref, k_ref, v_ref, og_ref, gt_ref, cq_ref, ck_ref, gb_ref, nw_ref, o_ref,
                  qp_scr, kp_scr, c_scr, n_scr, m_scr):
    seq = q_ref.shape[0]
    c = M_CHUNK
    scale = M_DK ** -0.5

    zero_halo = jnp.zeros((CONV_HALO, M_DK), F32)
    qp_scr[0:CONV_HALO, :] = zero_halo
    kp_scr[0:CONV_HALO, :] = zero_halo
    qp_scr[CONV_HALO:CONV_HALO + seq, :] = q_ref[...].astype(F32)
    kp_scr[CONV_HALO:CONV_HALO + seq, :] = k_ref[...].astype(F32)
    c_scr[...] = jnp.zeros_like(c_scr)
    n_scr[...] = jnp.zeros_like(n_scr)
    m_scr[...] = jnp.zeros_like(m_scr)
    causal = (lax.broadcasted_iota(jnp.int32, (c, c), 0) >= lax.broadcasted_iota(jnp.int32, (c, c), 1))

    def conv_silu(p_scr, w_ref, r0):
        blk = p_scr[pl.ds(r0, c + CONV_HALO), :]
        acc = None
        for j in range(CONV_W):
            lo = CONV_HALO - (CONV_W - 1) + j
            term = w_ref[j:j + 1, :] * blk[lo:lo + c, :]
            acc = term if acc is None else acc + term
        return acc * _sigmoid(acc)

    def body(n, carry):
        r0 = pl.multiple_of(n * c, c)
        rows = pl.ds(r0, c)
        q = conv_silu(qp_scr, cq_ref, r0) * scale
        k = conv_silu(kp_scr, ck_ref, r0)
        v = v_ref[rows, :]
        qb = q.astype(BF16)

        gt = gt_ref[rows, :] + gb_ref[...]
        bcum = _cumsum_rows(_log_sigmoid(gt))
        li_col = gt[:, 0:1]
        b_col = bcum[:, 1:2]
        a_col = li_col - b_col
        a_row = gt.T[0:1, :] - bcum.T[1:2, :]
        b_last = b_col[c - 1:c, :]
        m_prev = m_scr[:, 0:1]
        c_prev = c_scr[...]
        n_prev = n_scr[...]

        dmat = jnp.where(causal, b_col + a_row, -jnp.inf)
        m_inter = b_col + m_prev
        m_t = jnp.maximum(m_inter, jnp.max(dmat, axis=1, keepdims=True))
        w_ts = jnp.exp(dmat - m_t) * _dot_nt(qb, k.astype(BF16))
        s_inter = jnp.exp(m_inter - m_t)
        num = _dot(w_ts.astype(BF16), v) + s_inter * _dot(qb, c_prev.astype(BF16))
        den = (jnp.sum(w_ts, axis=1, keepdims=True)
               + s_inter * jnp.sum(q * n_prev, axis=1, keepdims=True))
        h = num / jnp.maximum(jnp.abs(den), jnp.exp(-m_t))

        m_new = jnp.maximum(b_last + m_prev, b_last + jnp.max(a_col, axis=0, keepdims=True))
        kw = k * jnp.exp(b_last + a_col - m_new)
        carry_scale = jnp.exp(b_last + m_prev - m_new)
        c_scr[...] = carry_scale * c_prev + _dot_tn(kw.astype(BF16), v)
        n_scr[...] = carry_scale * n_prev + jnp.sum(kw, axis=0, keepdims=True)
        m_scr[...] = jnp.broadcast_to(m_new, m_scr.shape)

        gate = og_ref[rows, :].astype(F32)
        o_ref[rows, :] = (_rms(h, nw_ref[...]) * _sigmoid(gate)).astype(o_ref.dtype)
        return carry

    lax.fori_loop(0, seq // c, body, 0)


def _mlstm(z3, zs3, conv, gbias, nw):
    b, s, _ = z3.shape
    qb, kb, vb, ob = Z_MQ // M_DK, Z_MK // M_DK, Z_MV // M_DV, Z_MO // M_DV
    return pl.pallas_call(
        _mlstm_kernel,
        grid=(b, M_HEADS),
        in_specs=[pl.BlockSpec((None, s, M_DK), lambda i, h: (i, 0, qb + h)),
                  pl.BlockSpec((None, s, M_DK), lambda i, h: (i, 0, kb + h)),
                  pl.BlockSpec((None, s, M_DV), lambda i, h: (i, 0, vb + h)),
                  pl.BlockSpec((None, s, M_DV), lambda i, h: (i, 0, ob + h)),
                  pl.BlockSpec((None, s, LANES), lambda i, h: (i, 0, 1 + h)),
                  pl.BlockSpec((CONV_W, M_DK), lambda i, h: (0, h)),
                  pl.BlockSpec((CONV_W, M_DK), lambda i, h: (0, M_HEADS + h)),
                  pl.BlockSpec((None, 1, LANES), lambda i, h: (h, 0, 0)),
                  pl.BlockSpec((1, M_DV), lambda i, h: (0, h))],
        out_specs=pl.BlockSpec((None, s, M_DV), lambda i, h: (i, 0, h)),
        out_shape=jax.ShapeDtypeStruct((b, s, M_HEADS * M_DV), BF16),
        scratch_shapes=[pltpu.VMEM((s + CONV_HALO, M_DK), F32), pltpu.VMEM((s + CONV_HALO, M_DK), F32),
                        pltpu.VMEM((M_DK, M_DV), F32), pltpu.VMEM((1, M_DK), F32), pltpu.VMEM((1, LANES), F32)],
        compiler_params=_params("parallel", "arbitrary"),
        name="mlstm",
    )(z3, z3, z3, z3, zs3, conv, conv, gbias, nw)


def _merge_kernel(x_ref, og_ref, om_ref, ag_ref, am_ref, wg_ref, wm_ref, wo_ref, o_ref):
    merged = (_sigmoid(ag_ref[...].astype(F32)) * _dot(og_ref[...], wg_ref[...])
              + _sigmoid(am_ref[...].astype(F32)) * _dot(om_ref[...], wm_ref[...]))
    o_ref[...] = x_ref[...] + _dot(merged.astype(BF16), wo_ref[...])


def _merge(x, o_gla, o_m, z, wg, wm, wo, *, tm):
    t, d = x.shape
    row = lambda i: (i, 0)
    full = lambda i: (0, 0)
    return pl.pallas_call(
        _merge_kernel,
        grid=(t // tm,),
        in_specs=[pl.BlockSpec((tm, d), row), pl.BlockSpec((tm, d), row), pl.BlockSpec((tm, d), row),
                  pl.BlockSpec((tm, d), lambda i: (i, Z_AG // D_MODEL)),
                  pl.BlockSpec((tm, d), lambda i: (i, Z_AM // D_MODEL)),
                  pl.BlockSpec((d, d), full), pl.BlockSpec((d, d), full), pl.BlockSpec((d, d), full)],
        out_specs=pl.BlockSpec((tm, d), row),
        out_shape=jax.ShapeDtypeStruct((t, d), F32),
        compiler_params=_params("parallel"),
        name="merge",
    )(x, o_gla, o_m, z, z, wg, wm, wo)


def _xattn_kernel(x_ref, g_ref, wq_ref, kv_ref, wo_ref, o_ref):
    d = x_ref.shape[1]
    dh = d // X_HEADS
    x = x_ref[...]
    q = _dot(_rms(x, g_ref[...]).astype(BF16), wq_ref[...]).astype(BF16)
    heads = []
    for h in range(X_HEADS):
        k = kv_ref[:, h * dh:(h + 1) * dh]
        v = kv_ref[:, d + h * dh:d + (h + 1) * dh]
        s = _dot_nt(q[:, h * dh:(h + 1) * dh], k) * dh ** -0.5
        p = jnp.exp(s - jnp.max(s, axis=-1, keepdims=True))
        p = p / jnp.sum(p, axis=-1, keepdims=True)
        heads.append(_dot(p.astype(BF16), v).astype(BF16))
    o_ref[...] = x + _dot(jnp.concatenate(heads, axis=-1), wo_ref[...])


def _xattn(x3, g, wq, kv, wo, *, tq):
    b, s, d = x3.shape
    m = kv.shape[1]
    return pl.pallas_call(
        _xattn_kernel,
        grid=(b, s // tq),
        in_specs=[pl.BlockSpec((None, tq, d), lambda i, j: (i, j, 0)),
                  pl.BlockSpec((1, d), lambda i, j: (0, 0)),
                  pl.BlockSpec((d, d), lambda i, j: (0, 0)),
                  pl.BlockSpec((None, m, 2 * d), lambda i, j: (i, 0, 0)),
                  pl.BlockSpec((d, d), lambda i, j: (0, 0))],
        out_specs=pl.BlockSpec((None, tq, d), lambda i, j: (i, j, 0)),
        out_shape=jax.ShapeDtypeStruct((b, s, d), F32),
        compiler_params=_params("parallel", "parallel"),
        name="xattn",
    )(x3, g, wq, kv, wo)


def _ffn_kernel(*refs, final_norm):
    if final_norm:
        x_ref, g_ref, w1_ref, w3_ref, w2_ref, gf_ref, o_ref, h_scr, acc_scr = refs
    else:
        x_ref, g_ref, w1_ref, w3_ref, w2_ref, o_ref, h_scr, acc_scr = refs
    j = pl.program_id(1)

    @pl.when(j == 0)
    def _():
        h_scr[...] = _rms(x_ref[...], g_ref[...]).astype(BF16)
        acc_scr[...] = jnp.zeros_like(acc_scr)

    h = h_scr[...]
    a = _dot(h, w1_ref[...])
    act = (a * _sigmoid(a) * _dot(h, w3_ref[...])).astype(BF16)
    acc_scr[...] += _dot(act, w2_ref[...])

    @pl.when(j == pl.num_programs(1) - 1)
    def _():
        out = x_ref[...] + acc_scr[...]
        if final_norm:
            out = _rms(out, gf_ref[...])
        o_ref[...] = out


def _ffn(x, g, w1, w3, w2, g_final, *, tm, tf):
    t, d = x.shape
    f = w2.shape[0]
    in_specs = [pl.BlockSpec((tm, d), lambda i, j: (i, 0)),
                pl.BlockSpec((1, d), lambda i, j: (0, 0)),
                pl.BlockSpec((d, tf), lambda i, j: (0, j)),
                pl.BlockSpec((d, tf), lambda i, j: (0, j)),
                pl.BlockSpec((tf, d), lambda i, j: (j, 0))]
    args = [x, g, w1, w3, w2]
    if g_final is not None:
        in_specs.append(pl.BlockSpec((1, d), lambda i, j: (0, 0)))
        args.append(g_final)
    return pl.pallas_call(
        functools.partial(_ffn_kernel, final_norm=g_final is not None),
        grid=(t // tm, f // tf),
        in_specs=in_specs,
        out_specs=pl.BlockSpec((tm, d), lambda i, j: (i, 0)),
        out_shape=jax.ShapeDtypeStruct((t, d), F32),
        scratch_shapes=[pltpu.VMEM((tm, d), BF16), pltpu.VMEM((tm, d), F32)],
        compiler_params=_params("parallel", "arbitrary"),
        name="dense_ffn",
    )(*args)


def _store_packed(ref, y):
    half = y.shape[1] // 2
    bits = lax.bitcast_convert_type(y.astype(BF16).astype(F32), jnp.int32)
    lo = lax.shift_right_logical(bits[:, :half], 16)
    hi = jnp.bitwise_and(bits[:, half:], jnp.int32(-65536))
    packed = jnp.bitwise_or(hi, lo)
    w = half // PACK_HALVES
    for p in range(PACK_HALVES):
        ref[p] = packed[:, p * w:(p + 1) * w]


def _load_packed(ref):
    packed = jnp.concatenate([ref[p] for p in range(PACK_HALVES)], axis=1)
    lo = lax.bitcast_convert_type(lax.shift_left(packed, 16), F32)
    hi = lax.bitcast_convert_type(jnp.bitwise_and(packed, jnp.int32(-65536)), F32)
    return jnp.concatenate([lo, hi], axis=1)


def _router_kernel(x_ref, g_ref, r_ref, hp_ref, mi_ref, mw_ref, cnt_ref, carry_scr):
    @pl.when(pl.program_id(0) == 0)
    def _():
        carry_scr[...] = jnp.zeros_like(carry_scr)

    rows = x_ref.shape[0]
    h = _rms(x_ref[...], g_ref[...])
    _store_packed(hp_ref, h)
    logits = jnp.dot(h, r_ref[...], preferred_element_type=F32, precision=lax.Precision.HIGHEST)
    lane = lax.broadcasted_iota(jnp.int32, logits.shape, 1)
    logits = jnp.where(lane < N_EXPERTS, logits, -jnp.inf)
    m1 = jnp.max(logits, axis=-1, keepdims=True)
    i1 = jnp.min(jnp.where(logits == m1, lane, LANES), axis=-1, keepdims=True)
    rest = jnp.where(lane == i1, -jnp.inf, logits)
    m2 = jnp.max(rest, axis=-1, keepdims=True)
    i2 = jnp.min(jnp.where(rest == m2, lane, LANES), axis=-1, keepdims=True)
    e2 = jnp.exp(m2 - m1)
    w1 = 1.0 / (1.0 + e2)
    w2 = e2 / (1.0 + e2)
    oh1 = (lane == i1).astype(F32)
    oh2 = (lane == i2).astype(F32)
    oh = oh1 + oh2
    earlier = (lax.broadcasted_iota(jnp.int32, (rows, rows), 0) > lax.broadcasted_iota(jnp.int32, (rows, rows), 1))
    before = _dot(earlier.astype(BF16), oh.astype(BF16)) + carry_scr[...]
    rank1 = jnp.sum(oh1 * before, axis=-1, keepdims=True).astype(jnp.int32)
    rank2 = jnp.sum(oh2 * before, axis=-1, keepdims=True).astype(jnp.int32)
    carry_scr[...] += jnp.sum(oh, axis=0, keepdims=True)
    cnt_ref[...] = carry_scr[...]
    mi_ref[...] = jnp.where(lane == 0, i1, jnp.where(lane == 1, i2, jnp.where(lane == 2, rank1,
                            jnp.where(lane == 3, rank2, 0))))
    mw_ref[...] = jnp.where(lane == 0, w1, jnp.where(lane == 1, w2, 0.0))


def _router(x, g, router, *, tm):
    t, d = x.shape
    row = lambda i: (i, 0)
    fix = lambda i: (0, 0)
    return pl.pallas_call(
        _router_kernel,
        grid=(t // tm,),
        in_specs=[pl.BlockSpec((tm, d), row), pl.BlockSpec((1, d), fix), pl.BlockSpec((d, LANES), fix)],
        out_specs=[pl.BlockSpec((PACK_HALVES, tm, PACK_W), lambda i: (0, i, 0)), pl.BlockSpec((tm, LANES), row),
                   pl.BlockSpec((tm, LANES), row), pl.BlockSpec((1, LANES), fix)],
        out_shape=[jax.ShapeDtypeStruct((PACK_HALVES, t, PACK_W), jnp.int32),
                   jax.ShapeDtypeStruct((t, LANES), jnp.int32),
                   jax.ShapeDtypeStruct((t, LANES), F32), jax.ShapeDtypeStruct((1, LANES), F32)],
        scratch_shapes=[pltpu.VMEM((1, LANES), F32)],
        compiler_params=_params("arbitrary"),
        name="router",
    )(x, g, router)


def _grouped_kernel(te_ref, na_ref, xs_ref, w1_ref, w3_ref, w2_ref, ys_ref, h_scr, acc_scr):
    i = pl.program_id(0)
    j = pl.program_id(1)
    last = pl.num_programs(1) - 1
    active = i < na_ref[0]

    @pl.when(jnp.logical_and(active, j == 0))
    def _():
        h_scr[...] = _load_packed(xs_ref).astype(BF16)
        acc_scr[...] = jnp.zeros_like(acc_scr)

    @pl.when(active)
    def _():
        h = h_scr[...]
        a = _dot(h, w1_ref[...])
        act = (a * _sigmoid(a) * _dot(h, w3_ref[...])).astype(BF16)
        acc_scr[...] += _dot(act, w2_ref[...])

    @pl.when(jnp.logical_and(active, j == last))
    def _():
        _store_packed(ys_ref, acc_scr[...])

    @pl.when(jnp.logical_and(jnp.logical_not(active), j == last))
    def _():
        ys_ref[...] = jnp.zeros_like(ys_ref)


def _grouped_swiglu(tile_expert, n_active, xs, w13, w2, *, tmg, tf):
    _, p, _ = xs.shape
    d = w2.shape[2]
    f = w2.shape[1]
    nj = f // tf

    def jj(i, j, na):
        return jnp.where(i < na[0], j, nj - 1)

    grid_spec = pltpu.PrefetchScalarGridSpec(
        num_scalar_prefetch=2,
        grid=(p // tmg, nj),
        in_specs=[pl.BlockSpec((PACK_HALVES, tmg, PACK_W), lambda i, j, te, na: (0, i, 0)),
                  pl.BlockSpec((None, d, tf), lambda i, j, te, na: (te[i], 0, jj(i, j, na))),
                  pl.BlockSpec((None, d, tf), lambda i, j, te, na: (te[i], 0, nj + jj(i, j, na))),
                  pl.BlockSpec((None, tf, d), lambda i, j, te, na: (te[i], jj(i, j, na), 0))],
        out_specs=pl.BlockSpec((PACK_HALVES, tmg, PACK_W), lambda i, j, te, na: (0, i, 0)),
        scratch_shapes=[pltpu.VMEM((tmg, d), BF16), pltpu.VMEM((tmg, d), F32)],
    )
    return pl.pallas_call(
        _grouped_kernel,
        grid_spec=grid_spec,
        out_shape=jax.ShapeDtypeStruct((PACK_HALVES, p, PACK_W), jnp.int32),
        compiler_params=_params("arbitrary", "arbitrary"),
        name="grouped_swiglu",
    )(tile_expert, n_active, xs, w13, w13, w2)


def _combine_kernel(*refs, final_norm):
    if final_norm:
        x_ref, mw_ref, y1_ref, y2_ref, gf_ref, o_ref = refs
    else:
        x_ref, mw_ref, y1_ref, y2_ref, o_ref = refs
    mw = mw_ref[...]
    out = x_ref[...] + mw[:, 0:1] * _load_packed(y1_ref) + mw[:, 1:2] * _load_packed(y2_ref)
    if final_norm:
        out = _rms(out, gf_ref[...])
    o_ref[...] = out


def _combine(x, mw, y12, g_final, *, tm):
    t, d = x.shape
    row = lambda i: (i, 0)
    in_specs = [pl.BlockSpec((tm, d), row), pl.BlockSpec((tm, LANES), row),
                pl.BlockSpec((PACK_HALVES, tm, PACK_W), lambda i: (0, i, 0)),
                pl.BlockSpec((PACK_HALVES, tm, PACK_W), lambda i: (0, t // tm + i, 0))]
    args = [x, mw, y12, y12]
    if g_final is not None:
        in_specs.append(pl.BlockSpec((1, d), lambda i: (0, 0)))
        args.append(g_final)
    return pl.pallas_call(
        functools.partial(_combine_kernel, final_norm=g_final is not None),
        grid=(t // tm,),
        in_specs=in_specs,
        out_specs=pl.BlockSpec((tm, d), row),
        out_shape=jax.ShapeDtypeStruct((t, d), F32),
        compiler_params=_params("parallel"),
        name="combine",
    )(*args)


def _gather_rows(table, idx):
    halves, n_rows, width = table.shape
    n = halves * idx.shape[0]
    assert n % SC_GATHER_WINDOW == 0
    flat_idx = jnp.concatenate([idx + h * n_rows for h in range(halves)]).reshape(1, n)
    mesh = plsc.VectorSubcoreMesh(core_axis_name="core", subcore_axis_name="subcore")

    @pl.kernel(out_type=jax.ShapeDtypeStruct((n, width), table.dtype), mesh=mesh)
    def gather(table_hbm, idx_hbm, out_hbm):
        def body(idx_vmem, out_vmem):
            pltpu.sync_copy(table_hbm.at[idx_vmem.at[0]], out_vmem)

        pltpu.emit_pipeline(
            body,
            grid=(n // SC_GATHER_WINDOW,),
            in_specs=[pl.BlockSpec((1, SC_GATHER_WINDOW), lambda i: (0, i))],
            out_specs=[pl.BlockSpec((SC_GATHER_WINDOW, width), lambda i: (i, 0))],
            core_axis_name=("core", "subcore"),
            dimension_semantics=(pltpu.PARALLEL,),
        )(idx_hbm, out_hbm)

    return gather(table.reshape(halves * n_rows, width), flat_idx).reshape(halves, idx.shape[0], width)


def _moe_sparse(x, g, router, w13, w2, g_final, *, tm, tmg, tf):
    t, d = x.shape
    hp, mi, mw, cnt = _router(x, g, router, tm=tm)
    counts = cnt[0, :N_EXPERTS].astype(jnp.int32)
    tiles_per_expert = (counts + tmg - 1) // tmg
    tile_end = jnp.cumsum(tiles_per_expert)
    group_start = (tile_end - tiles_per_expert) * tmg
    e1, e2, r1, r2 = mi[:, 0], mi[:, 1], mi[:, 2], mi[:, 3]
    pos1 = group_start[e1] + r1
    pos2 = group_start[e2] + r2
    n_tiles = 2 * t // tmg + N_EXPERTS
    token = jnp.arange(t, dtype=jnp.int32)
    src = jnp.zeros((n_tiles * tmg,), jnp.int32).at[pos1].set(token).at[pos2].set(token)
    tile_expert = jnp.minimum(jnp.searchsorted(tile_end, jnp.arange(n_tiles, dtype=jnp.int32), side="right"),
                              N_EXPERTS - 1).astype(jnp.int32)
    n_active = tile_end[-1:].astype(jnp.int32)

    xs = _gather_rows(hp, src)
    ys = _grouped_swiglu(tile_expert, n_active, xs, w13, w2, tmg=tmg, tf=tf)
    y12 = _gather_rows(ys, jnp.concatenate([pos1, pos2]))
    return _combine(x, mw, y12, g_final, tm=tm)


def _split_w_in(w_in):
    sizes = (512, 512, 1024, GLA_RANK, 1024, 512, 512, 1024, M_HEADS, M_HEADS, 1024, 1024, 1024)
    parts, off = [], 0
    for n in sizes:
        parts.append(w_in[:, off:off + n])
        off += n
    g_q, g_k, g_v, g_lr, g_g, m_q, m_k, m_v, m_i, m_f, m_o, a_g, a_m = parts
    wz = jnp.concatenate([g_q, g_k, g_v, g_g, m_q, m_k, m_v, m_o, a_g, a_m], axis=1).astype(BF16)
    d = w_in.shape[0]
    groups = [jnp.pad(g_lr, ((0, 0), (0, LANES - GLA_RANK)))]
    for h in range(M_HEADS):
        pair = jnp.concatenate([m_i[:, h:h + 1], m_f[:, h:h + 1]], axis=1)
        groups.append(jnp.pad(pair, ((0, 0), (0, LANES - 2))))
    ws = jnp.concatenate(groups, axis=1).astype(BF16)
    assert wz.shape == (d, Z_COLS) and ws.shape == (d, ZS_COLS)
    return wz, ws


def _pick(total, want):
    t = min(total, want)
    while total % t:
        t -= 1
    return t


def kernel(x, mem, norm_mix, w_in, gla_gk_up, gla_gk_bias, gla_norm, m_conv, m_gate_bias, m_norm, gla_proj, m_proj, w_out, norm_xattn, norm_mem, x_wq, x_wkv, x_wo, norm_ffn, ffn_w13, ffn_w2, moe_router, moe_w13, moe_w2, norm_final):
    b, s, d = x.shape
    m_len = mem.shape[1]
    t = b * s
    depth = norm_mix.shape[0]
    tm = _pick(t, 512)
    row2 = lambda a: a.reshape(1, -1)

    xt = x.reshape(t, d)
    mem_t = mem.reshape(b * m_len, d)
    for l in range(depth):
        last_layer = l == depth - 1
        wz, ws = _split_w_in(w_in[l])
        z, zs = _norm_matmul(xt, row2(norm_mix[l]), wz, ws, tm=tm, tn=1024)
        z3 = z.reshape(b, s, Z_COLS)
        zs3 = zs.reshape(b, s, ZS_COLS)
        up = jnp.pad(gla_gk_up[l], ((0, LANES - GLA_RANK), (0, 0))).astype(BF16)
        o_gla = _gla(z3, zs3, up, row2(gla_gk_bias[l]), row2(gla_norm[l]))
        gbias = jnp.pad(jnp.stack([m_gate_bias[l, :M_HEADS], m_gate_bias[l, M_HEADS:]], axis=1),
                        ((0, 0), (0, LANES - 2))).reshape(M_HEADS, 1, LANES)
        o_m = _mlstm(z3, zs3, m_conv[l], gbias, row2(m_norm[l]))
        xt = _merge(xt, o_gla.reshape(t, d), o_m.reshape(t, d), z,
                    gla_proj[l].astype(BF16), m_proj[l].astype(BF16), w_out[l].astype(BF16), tm=tm)
        kv = _norm_matmul(mem_t, row2(norm_mem[l]), x_wkv[l].astype(BF16), tm=_pick(b * m_len, 512), tn=1024)
        xt = _xattn(xt.reshape(b, s, d), row2(norm_xattn[l]), x_wq[l].astype(BF16),
                    kv.reshape(b, m_len, 2 * d), x_wo[l].astype(BF16), tq=_pick(s, 512)).reshape(t, d)
        g_final = row2(norm_final) if last_layer else None
        if l % 2 == 0:
            w13, w2 = ffn_w13[l // 2], ffn_w2[l // 2]
            f = w2.shape[0]
            fp = -(-f // (2 * LANES)) * (2 * LANES)
            w1 = jnp.pad(w13[:, :f], ((0, 0), (0, fp - f))).astype(BF16)
            w3 = jnp.pad(w13[:, f:], ((0, 0), (0, fp - f))).astype(BF16)
            w2p = jnp.pad(w2, ((0, fp - f), (0, 0))).astype(BF16)
            xt = _ffn(xt, row2(norm_ffn[l]), w1, w3, w2p, g_final, tm=tm, tf=fp // 2)
        else:
            router = jnp.pad(moe_router[l // 2], ((0, 0), (0, LANES - N_EXPERTS)))
            w2 = moe_w2[l // 2].astype(BF16)
            xt = _moe_sparse(xt, row2(norm_ffn[l]), router, moe_w13[l // 2].astype(BF16), w2, g_final,
                             tm=tm, tmg=tm, tf=w2.shape[1] // 4)
    return xt.reshape(b, s, d)
```

```python
import functools

import jax
import jax.numpy as jnp
from jax import lax
from jax.experimental import pallas as pl
from jax.experimental.pallas import tpu as pltpu
from jax.experimental.pallas import tpu_sc as plsc

F32 = jnp.float32
BF16 = jnp.bfloat16

EPS = 1e-6
D_MODEL = 1024
GLA_HEADS = 4
GLA_DK = 128
GLA_DV = 256
GLA_RANK = 16
GLA_GATE_NORM = 16.0
GLA_LOG_DECAY_MIN = -1.0
GLA_CHUNK = 64
M_HEADS = 4
M_DK = 128
M_DV = 256
M_CHUNK = 128
CONV_W = 4
X_HEADS = 4
N_EXPERTS = 8
LANES = 128
CONV_HALO = 8
VMEM_LIMIT = 48 * 1024 * 1024
SC_GATHER_WINDOW = 128
PACK_HALVES = 2
PACK_W = D_MODEL // 2 // PACK_HALVES

Z_GQ, Z_GK, Z_GV, Z_GG = 0, 512, 1024, 2048
Z_MQ, Z_MK, Z_MV, Z_MO = 3072, 3584, 4096, 5120
Z_AG, Z_AM = 6144, 7168
Z_COLS = 8192
ZS_COLS = 2 * LANES


def _params(*sem):
    return pltpu.CompilerParams(dimension_semantics=sem, vmem_limit_bytes=VMEM_LIMIT)


def _rms(x, g):
    return x * lax.rsqrt(jnp.mean(x * x, axis=-1, keepdims=True) + EPS) * g


def _log_sigmoid(u):
    return jnp.minimum(u, 0.0) - jnp.log(1.0 + jnp.exp(-jnp.abs(u)))


def _sigmoid(u):
    return 1.0 / (1.0 + jnp.exp(-u))


def _dot(a, b):
    return jnp.dot(a, b, preferred_element_type=F32)


def _dot_nt(a, b):
    return lax.dot_general(a, b, (((1,), (1,)), ((), ())), preferred_element_type=F32)


def _dot_tn(a, b):
    return lax.dot_general(a, b, (((0,), (0,)), ((), ())), preferred_element_type=F32)


def _cumsum_rows(x):
    n = x.shape[0]
    row = lax.broadcasted_iota(jnp.int32, x.shape, 0)
    s = 1
    while s < n:
        x = x + jnp.where(row >= s, pltpu.roll(x, s, 0), 0.0)
        s *= 2
    return x


def _norm_matmul_kernel(*refs, with_small):
    if with_small:
        x_ref, g_ref, w_ref, ws_ref, z_ref, zs_ref, h_scr = refs
    else:
        x_ref, g_ref, w_ref, z_ref, h_scr = refs

    @pl.when(pl.program_id(1) == 0)
    def _():
        h = _rms(x_ref[...], g_ref[...]).astype(BF16)
        h_scr[...] = h
        if with_small:
            zs_ref[...] = _dot(h, ws_ref[...])

    z_ref[...] = _dot(h_scr[...], w_ref[...]).astype(z_ref.dtype)


def _norm_matmul(x, g, w, ws=None, *, tm, tn):
    t, d = x.shape
    n = w.shape[1]
    in_specs = [pl.BlockSpec((tm, d), lambda i, j: (i, 0)),
                pl.BlockSpec((1, d), lambda i, j: (0, 0)),
                pl.BlockSpec((d, tn), lambda i, j: (0, j))]
    out_specs = [pl.BlockSpec((tm, tn), lambda i, j: (i, j))]
    out_shape = [jax.ShapeDtypeStruct((t, n), BF16)]
    args = [x, g, w]
    if ws is not None:
        ns = ws.shape[1]
        in_specs.append(pl.BlockSpec((d, ns), lambda i, j: (0, 0)))
        out_specs.append(pl.BlockSpec((tm, ns), lambda i, j: (i, 0)))
        out_shape.append(jax.ShapeDtypeStruct((t, ns), F32))
        args.append(ws)
    out = pl.pallas_call(
        functools.partial(_norm_matmul_kernel, with_small=ws is not None),
        grid=(t // tm, n // tn),
        in_specs=in_specs, out_specs=out_specs, out_shape=out_shape,
        scratch_shapes=[pltpu.VMEM((tm, d), BF16)],
        compiler_params=_params("parallel", "arbitrary"),
        name="norm_matmul",
    )(*args)
    return out if ws is not None else out[0]


def _gla_kernel(q_ref, k_ref, v_ref, gg_ref, lr_ref, up_ref, gb_ref, nw_ref, o_ref, la_scr, st_scr):
    seq = q_ref.shape[0]
    c = GLA_CHUNK
    scale = GLA_DK ** -0.5

    u = _dot(lr_ref[...].astype(BF16), up_ref[...]) + gb_ref[...]
    la_scr[...] = jnp.maximum(_log_sigmoid(u) * (1.0 / GLA_GATE_NORM), GLA_LOG_DECAY_MIN)
    st_scr[...] = jnp.zeros_like(st_scr)
    causal = (lax.broadcasted_iota(jnp.int32, (c, c), 0) >= lax.broadcasted_iota(jnp.int32, (c, c), 1))

    def body(n, carry):
        r0 = pl.multiple_of(n * c, c)
        rows = pl.ds(r0, c)
        cum = _cumsum_rows(la_scr[rows, :])
        cum_last = cum[c - 1:c, :]
        q = q_ref[rows, :].astype(F32) * scale
        k = k_ref[rows, :].astype(F32)
        v = v_ref[rows, :]
        q_dec = (q * jnp.exp(cum)).astype(BF16)
        k_inv = (k * jnp.exp(-cum)).astype(BF16)
        k_end = (k * jnp.exp(cum_last - cum)).astype(BF16)
        scores = jnp.where(causal, _dot_nt(q_dec, k_inv), 0.0).astype(BF16)
        st = st_scr[...]
        o = _dot(scores, v) + _dot_nt(q_dec, st.astype(BF16))
        st_scr[...] = st * jnp.exp(cum_last) + _dot_tn(v, k_end)
        gate = gg_ref[rows, :].astype(F32)
        o_ref[rows, :] = (_rms(o, nw_ref[...]) * (gate * _sigmoid(gate))).astype(o_ref.dtype)
        return carry

    lax.fori_loop(0, seq // c, body, 0, unroll=4)


def _gla(z3, zs3, up, gb, nw):
    b, s, _ = z3.shape
    qb, kb, vb, gb_ = Z_GQ // GLA_DK, Z_GK // GLA_DK, Z_GV // GLA_DV, Z_GG // GLA_DV
    return pl.pallas_call(
        _gla_kernel,
        grid=(b, GLA_HEADS),
        in_specs=[pl.BlockSpec((None, s, GLA_DK), lambda i, h: (i, 0, qb + h)),
                  pl.BlockSpec((None, s, GLA_DK), lambda i, h: (i, 0, kb + h)),
                  pl.BlockSpec((None, s, GLA_DV), lambda i, h: (i, 0, vb + h)),
                  pl.BlockSpec((None, s, GLA_DV), lambda i, h: (i, 0, gb_ + h)),
                  pl.BlockSpec((None, s, LANES), lambda i, h: (i, 0, 0)),
                  pl.BlockSpec((LANES, GLA_DK), lambda i, h: (0, h)),
                  pl.BlockSpec((1, GLA_DK), lambda i, h: (0, h)),
                  pl.BlockSpec((1, GLA_DV), lambda i, h: (0, h))],
        out_specs=pl.BlockSpec((None, s, GLA_DV), lambda i, h: (i, 0, h)),
        out_shape=jax.ShapeDtypeStruct((b, s, GLA_HEADS * GLA_DV), BF16),
        scratch_shapes=[pltpu.VMEM((s, GLA_DK), F32), pltpu.VMEM((GLA_DV, GLA_DK), F32)],
        compiler_params=_params("parallel", "arbitrary"),
        name="gla",
    )(z3, z3, z3, z3, zs3, up, gb, nw)


def _mlstm_gates_kernel(gt_ref, gb_ref, grow_ref, gcol_ref):
    seq = gt_ref.shape[0]
    c = M_CHUNK
    gates_t = (gt_ref[...] + gb_ref[...]).T
    top = gates_t[0:2 * M_HEADS, :]
    is_forget = lax.broadcasted_iota(jnp.int32, top.shape, 0) % 2 == 1
    top = jnp.where(is_forget, _log_sigmoid(top), top)
    lane_in_chunk = lax.broadcasted_iota(jnp.int32, top.shape, 1) % c
    step = 1
    while step < c:
        shifted = jnp.where(lane_in_chunk >= step, pltpu.roll(top, step, 1), 0.0)
        top = jnp.where(is_forget, top + shifted, top)
        step *= 2
    for h in range(M_HEADS):
        slab = jnp.concatenate([top[2 * h:2 * h + 2, :], jnp.zeros((LANES - 2, seq), F32)], axis=0)
        grow_ref[h] = slab[0:CONV_HALO, :]
        gcol_ref[h] = slab.T


def _mlstm_gates(zs3, gbias):
    b, s, _ = zs3.shape
    return pl.pallas_call(
        _mlstm_gates_kernel,
        grid=(b,),
        in_specs=[pl.BlockSpec((None, s, LANES), lambda i: (i, 0, 1)),
                  pl.BlockSpec((1, LANES), lambda i: (0, 0))],
        out_specs=[pl.BlockSpec((None, M_HEADS, CONV_HALO, s), lambda i: (i, 0, 0, 0)),
                   pl.BlockSpec((None, M_HEADS, s, LANES), lambda i: (i, 0, 0, 0))],
        out_shape=[jax.ShapeDtypeStruct((b, M_HEADS, CONV_HALO, s), F32),
                   jax.ShapeDtypeStruct((b, M_HEADS, s, LANES), F32)],
        compiler_params=_params("parallel"),
        name="mlstm_gates",
    )(zs3, gbias)


def _mlstm_kernel(q_ref, k_ref, v_ref, og_ref, grow_ref, gcol_ref, cq_ref, ck_ref, nw_ref, o_ref,
                  qp_scr, kp_scr, c_scr, n_scr, m_scr):
    seq = q_ref.shape[0]
    c = M_CHUNK
    scale = M_DK ** -0.5

    zero_halo = jnp.zeros((CONV_HALO, M_DK), F32)
    qp_scr[0:CONV_HALO, :] = zero_halo
    kp_scr[0:CONV_HALO, :] = zero_halo
    qp_scr[CONV_HALO:CONV_HALO + seq, :] = q_ref[...].astype(F32)
    kp_scr[CONV_HALO:CONV_HALO + seq, :] = k_ref[...].astype(F32)
    c_scr[...] = jnp.zeros_like(c_scr)
    n_scr[...] = jnp.zeros_like(n_scr)
    m_scr[...] = jnp.zeros_like(m_scr)
    causal = (lax.broadcasted_iota(jnp.int32, (c, c), 0) >= lax.broadcasted_iota(jnp.int32, (c, c), 1))

    def conv_silu(p_scr, w_ref, r0):
        blk = p_scr[pl.ds(r0, c + CONV_HALO), :]
        acc = None
        for j in range(CONV_W):
            lo = CONV_HALO - (CONV_W - 1) + j
            term = w_ref[j:j + 1, :] * blk[lo:lo + c, :]
            acc = term if acc is None else acc + term
        return acc * _sigmoid(acc)

    def body(n, carry):
        r0 = pl.multiple_of(n * c, c)
        rows = pl.ds(r0, c)
        q = conv_silu(qp_scr, cq_ref, r0) * scale
        k = conv_silu(kp_scr, ck_ref, r0)
        v = v_ref[rows, :]
        qb = q.astype(BF16)

        g_rows = grow_ref[:, rows]
        g_cols = gcol_ref[rows, :]
        b_col = g_cols[:, 1:2]
        a_col = g_cols[:, 0:1] - b_col
        a_row = g_rows[0:1, :] - g_rows[1:2, :]
        b_last = b_col[c - 1:c, :]
        m_prev = m_scr[:, 0:1]
        c_prev = c_scr[...]
        n_prev = n_scr[...]

        dmat = jnp.where(causal, b_col + a_row, -jnp.inf)
        m_inter = b_col + m_prev
        m_t = jnp.maximum(m_inter, jnp.max(dmat, axis=1, keepdims=True))
        w_ts = jnp.exp(dmat - m_t) * _dot_nt(qb, k.astype(BF16))
        s_inter = jnp.exp(m_inter - m_t)
        num = _dot(w_ts.astype(BF16), v) + s_inter * _dot(qb, c_prev.astype(BF16))
        den = (jnp.sum(w_ts, axis=1, keepdims=True)
               + s_inter * jnp.sum(q * n_prev, axis=1, keepdims=True))
        h = num / jnp.maximum(jnp.abs(den), jnp.exp(-m_t))

        m_new = jnp.maximum(b_last + m_prev, b_last + jnp.max(a_col, axis=0, keepdims=True))
        kw = k * jnp.exp(b_last + a_col - m_new)
        carry_scale = jnp.exp(b_last + m_prev - m_new)
        c_scr[...] = carry_scale * c_prev + _dot_tn(kw.astype(BF16), v)
        n_scr[...] = carry_scale * n_prev + jnp.sum(kw, axis=0, keepdims=True)
        m_scr[...] = jnp.broadcast_to(m_new, m_scr.shape)

        gate = og_ref[rows, :].astype(F32)
        o_ref[rows, :] = (_rms(h, nw_ref[...]) * _sigmoid(gate)).astype(o_ref.dtype)
        return carry

    lax.fori_loop(0, seq // c, body, 0, unroll=2)


def _mlstm(z3, grow, gcol, conv, nw):
    b, s, _ = z3.shape
    qb, kb, vb, ob = Z_MQ // M_DK, Z_MK // M_DK, Z_MV // M_DV, Z_MO // M_DV
    return pl.pallas_call(
        _mlstm_kernel,
        grid=(b, M_HEADS),
        in_specs=[pl.BlockSpec((None, s, M_DK), lambda i, h: (i, 0, qb + h)),
                  pl.BlockSpec((None, s, M_DK), lambda i, h: (i, 0, kb + h)),
                  pl.BlockSpec((None, s, M_DV), lambda i, h: (i, 0, vb + h)),
                  pl.BlockSpec((None, s, M_DV), lambda i, h: (i, 0, ob + h)),
                  pl.BlockSpec((None, None, CONV_HALO, s), lambda i, h: (i, h, 0, 0)),
                  pl.BlockSpec((None, None, s, LANES), lambda i, h: (i, h, 0, 0)),
                  pl.BlockSpec((CONV_W, M_DK), lambda i, h: (0, h)),
                  pl.BlockSpec((CONV_W, M_DK), lambda i, h: (0, M_HEADS + h)),
                  pl.BlockSpec((1, M_DV), lambda i, h: (0, h))],
        out_specs=pl.BlockSpec((None, s, M_DV), lambda i, h: (i, 0, h)),
        out_shape=jax.ShapeDtypeStruct((b, s, M_HEADS * M_DV), BF16),
        scratch_shapes=[pltpu.VMEM((s + CONV_HALO, M_DK), F32), pltpu.VMEM((s + CONV_HALO, M_DK), F32),
                        pltpu.VMEM((M_DK, M_DV), F32), pltpu.VMEM((1, M_DK), F32), pltpu.VMEM((1, LANES), F32)],
        compiler_params=_params("parallel", "arbitrary"),
        name="mlstm",
    )(z3, z3, z3, z3, grow, gcol, conv, conv, nw)


def _merge_kernel(x_ref, og_ref, om_ref, ag_ref, am_ref, wg_ref, wm_ref, wo_ref, o_ref):
    merged = (_sigmoid(ag_ref[...].astype(F32)) * _dot(og_ref[...], wg_ref[...])
              + _sigmoid(am_ref[...].astype(F32)) * _dot(om_ref[...], wm_ref[...]))
    o_ref[...] = x_ref[...] + _dot(merged.astype(BF16), wo_ref[...])


def _merge(x, o_gla, o_m, z, wg, wm, wo, *, tm):
    t, d = x.shape
    row = lambda i: (i, 0)
    full = lambda i: (0, 0)
    return pl.pallas_call(
        _merge_kernel,
        grid=(t // tm,),
        in_specs=[pl.BlockSpec((tm, d), row), pl.BlockSpec((tm, d), row), pl.BlockSpec((tm, d), row),
                  pl.BlockSpec((tm, d), lambda i: (i, Z_AG // D_MODEL)),
                  pl.BlockSpec((tm, d), lambda i: (i, Z_AM // D_MODEL)),
                  pl.BlockSpec((d, d), full), pl.BlockSpec((d, d), full), pl.BlockSpec((d, d), full)],
        out_specs=pl.BlockSpec((tm, d), row),
        out_shape=jax.ShapeDtypeStruct((t, d), F32),
        compiler_params=_params("parallel"),
        name="merge",
    )(x, o_gla, o_m, z, z, wg, wm, wo)


def _xattn_kernel(x_ref, g_ref, wq_ref, kv_ref, wo_ref, o_ref):
    d = x_ref.shape[1]
    dh = d // X_HEADS
    x = x_ref[...]
    q = _dot(_rms(x, g_ref[...]).astype(BF16), wq_ref[...]).astype(BF16)
    heads = []
    for h in range(X_HEADS):
        k = kv_ref[:, h * dh:(h + 1) * dh]
        v = kv_ref[:, d + h * dh:d + (h + 1) * dh]
        s = _dot_nt(q[:, h * dh:(h + 1) * dh], k) * dh ** -0.5
        p = jnp.exp(s - jnp.max(s, axis=-1, keepdims=True))
        p = p / jnp.sum(p, axis=-1, keepdims=True)
        heads.append(_dot(p.astype(BF16), v).astype(BF16))
    o_ref[...] = x + _dot(jnp.concatenate(heads, axis=-1), wo_ref[...])


def _xattn(x3, g, wq, kv, wo, *, tq):
    b, s, d = x3.shape
    m = kv.shape[1]
    return pl.pallas_call(
        _xattn_kernel,
        grid=(b, s // tq),
        in_specs=[pl.BlockSpec((None, tq, d), lambda i, j: (i, j, 0)),
                  pl.BlockSpec((1, d), lambda i, j: (0, 0)),
                  pl.BlockSpec((d, d), lambda i, j: (0, 0)),
                  pl.BlockSpec((None, m, 2 * d), lambda i, j: (i, 0, 0)),
                  pl.BlockSpec((d, d), lambda i, j: (0, 0))],
        out_specs=pl.BlockSpec((None, tq, d), lambda i, j: (i, j, 0)),
        out_shape=jax.ShapeDtypeStruct((b, s, d), F32),
        compiler_params=_params("parallel", "parallel"),
        name="xattn",
    )(x3, g, wq, kv, wo)


def _ffn_kernel(*refs, final_norm):
    if final_norm:
        x_ref, g_ref, w1_ref, w3_ref, w2_ref, gf_ref, o_ref, h_scr, acc_scr = refs
    else:
        x_ref, g_ref, w1_ref, w3_ref, w2_ref, o_ref, h_scr, acc_scr = refs
    j = pl.program_id(1)

    @pl.when(j == 0)
    def _():
        h_scr[...] = _rms(x_ref[...], g_ref[...]).astype(BF16)
        acc_scr[...] = jnp.zeros_like(acc_scr)

    h = h_scr[...]
    a = _dot(h, w1_ref[...])
    act = (a * _sigmoid(a) * _dot(h, w3_ref[...])).astype(BF16)
    acc_scr[...] += _dot(act, w2_ref[...])

    @pl.when(j == pl.num_programs(1) - 1)
    def _():
        out = x_ref[...] + acc_scr[...]
        if final_norm:
            out = _rms(out, gf_ref[...])
        o_ref[...] = out


def _ffn(x, g, w1, w3, w2, g_final, *, tm, tf):
    t, d = x.shape
    f = w2.shape[0]
    in_specs = [pl.BlockSpec((tm, d), lambda i, j: (i, 0)),
                pl.BlockSpec((1, d), lambda i, j: (0, 0)),
                pl.BlockSpec((d, tf), lambda i, j: (0, j)),
                pl.BlockSpec((d, tf), lambda i, j: (0, j)),
                pl.BlockSpec((tf, d), lambda i, j: (j, 0))]
    args = [x, g, w1, w3, w2]
    if g_final is not None:
        in_specs.append(pl.BlockSpec((1, d), lambda i, j: (0, 0)))
        args.append(g_final)
    return pl.pallas_call(
        functools.partial(_ffn_kernel, final_norm=g_final is not None),
        grid=(t // tm, f // tf),
        in_specs=in_specs,
        out_specs=pl.BlockSpec((tm, d), lambda i, j: (i, 0)),
        out_shape=jax.ShapeDtypeStruct((t, d), F32),
        scratch_shapes=[pltpu.VMEM((tm, d), BF16), pltpu.VMEM((tm, d), F32)],
        compiler_params=_params("parallel", "arbitrary"),
        name="dense_ffn",
    )(*args)


def _store_packed(ref, y):
    half = y.shape[1] // 2
    bits = lax.bitcast_convert_type(y.astype(BF16).astype(F32), jnp.int32)
    lo = lax.shift_right_logical(bits[:, :half], 16)
    hi = jnp.bitwise_and(bits[:, half:], jnp.int32(-65536))
    packed = jnp.bitwise_or(hi, lo)
    w = half // PACK_HALVES
    for p in range(PACK_HALVES):
        ref[p] = packed[:, p * w:(p + 1) * w]


def _load_packed(ref):
    packed = jnp.concatenate([ref[p] for p in range(PACK_HALVES)], axis=1)
    lo = lax.bitcast_convert_type(lax.shift_left(packed, 16), F32)
    hi = lax.bitcast_convert_type(jnp.bitwise_and(packed, jnp.int32(-65536)), F32)
    return jnp.concatenate([lo, hi], axis=1)


def _router_kernel(x_ref, g_ref, r_ref, hp_ref, mi_ref, mw_ref, cnt_ref, carry_scr):
    @pl.when(pl.program_id(0) == 0)
    def _():
        carry_scr[...] = jnp.zeros_like(carry_scr)

    rows = x_ref.shape[0]
    h = _rms(x_ref[...], g_ref[...])
    _store_packed(hp_ref, h)
    logits = jnp.dot(h, r_ref[...], preferred_element_type=F32, precision=lax.Precision.HIGHEST)
    lane = lax.broadcasted_iota(jnp.int32, logits.shape, 1)
    logits = jnp.where(lane < N_EXPERTS, logits, -jnp.inf)
    m1 = jnp.max(logits, axis=-1, keepdims=True)
    i1 = jnp.min(jnp.where(logits == m1, lane, LANES), axis=-1, keepdims=True)
    rest = jnp.where(lane == i1, -jnp.inf, logits)
    m2 = jnp.max(rest, axis=-1, keepdims=True)
    i2 = jnp.min(jnp.where(rest == m2, lane, LANES), axis=-1, keepdims=True)
    e2 = jnp.exp(m2 - m1)
    w1 = 1.0 / (1.0 + e2)
    w2 = e2 / (1.0 + e2)
    oh1 = (lane == i1).astype(F32)
    oh2 = (lane == i2).astype(F32)
    oh = oh1 + oh2
    earlier = (lax.broadcasted_iota(jnp.int32, (rows, rows), 0) > lax.broadcasted_iota(jnp.int32, (rows, rows), 1))
    before = _dot(earlier.astype(BF16), oh.astype(BF16)) + carry_scr[...]
    rank1 = jnp.sum(oh1 * before, axis=-1, keepdims=True).astype(jnp.int32)
    rank2 = jnp.sum(oh2 * before, axis=-1, keepdims=True).astype(jnp.int32)
    carry_scr[...] += jnp.sum(oh, axis=0, keepdims=True)
    cnt_ref[...] = carry_scr[...]
    mi_ref[...] = jnp.where(lane == 0, i1, jnp.where(lane == 1, i2, jnp.where(lane == 2, rank1,
                            jnp.where(lane == 3, rank2, 0))))
    mw_ref[...] = jnp.where(lane == 0, w1, jnp.where(lane == 1, w2, 0.0))


def _router(x, g, router, *, tm):
    t, d = x.shape
    row = lambda i: (i, 0)
    fix = lambda i: (0, 0)
    return pl.pallas_call(
        _router_kernel,
        grid=(t // tm,),
        in_specs=[pl.BlockSpec((tm, d), row), pl.BlockSpec((1, d), fix), pl.BlockSpec((d, LANES), fix)],
        out_specs=[pl.BlockSpec((PACK_HALVES, tm, PACK_W), lambda i: (0, i, 0)), pl.BlockSpec((tm, LANES), row),
                   pl.BlockSpec((tm, LANES), row), pl.BlockSpec((1, LANES), fix)],
        out_shape=[jax.ShapeDtypeStruct((PACK_HALVES, t, PACK_W), jnp.int32),
                   jax.ShapeDtypeStruct((t, LANES), jnp.int32),
                   jax.ShapeDtypeStruct((t, LANES), F32), jax.ShapeDtypeStruct((1, LANES), F32)],
        scratch_shapes=[pltpu.VMEM((1, LANES), F32)],
        compiler_params=_params("arbitrary"),
        name="router",
    )(x, g, router)


def _grouped_kernel(te_ref, tv_ref, xs_ref, w1_ref, w3_ref, w2_ref, ys_ref, h_scr, acc_scr):
    i = pl.program_id(0)
    j = pl.program_id(1)
    last = pl.num_programs(1) - 1
    valid = tv_ref[i]
    active = valid > 0

    @pl.when(jnp.logical_and(active, j == 0))
    def _():
        h = _load_packed(xs_ref)
        row = lax.broadcasted_iota(jnp.int32, h.shape, 0)
        h_scr[...] = jnp.where(row < valid, h, 0.0).astype(BF16)
        acc_scr[...] = jnp.zeros_like(acc_scr)

    @pl.when(active)
    def _():
        h = h_scr[...]
        a = _dot(h, w1_ref[...])
        act = (a * _sigmoid(a) * _dot(h, w3_ref[...])).astype(BF16)
        acc_scr[...] += _dot(act, w2_ref[...])

    @pl.when(jnp.logical_and(active, j == last))
    def _():
        _store_packed(ys_ref, acc_scr[...])

    @pl.when(jnp.logical_and(jnp.logical_not(active), j == last))
    def _():
        ys_ref[...] = jnp.zeros_like(ys_ref)


def _grouped_swiglu(tile_expert, tile_valid, xs, w13, w2, *, tmg, tf):
    _, p, _ = xs.shape
    d = w2.shape[2]
    f = w2.shape[1]
    nj = f // tf

    def jj(i, j, tv):
        return jnp.where(tv[i] > 0, j, nj - 1)

    grid_spec = pltpu.PrefetchScalarGridSpec(
        num_scalar_prefetch=2,
        grid=(p // tmg, nj),
        in_specs=[pl.BlockSpec((PACK_HALVES, tmg, PACK_W), lambda i, j, te, tv: (0, i, 0)),
                  pl.BlockSpec((None, d, tf), lambda i, j, te, tv: (te[i], 0, jj(i, j, tv))),
                  pl.BlockSpec((None, d, tf), lambda i, j, te, tv: (te[i], 0, nj + jj(i, j, tv))),
                  pl.BlockSpec((None, tf, d), lambda i, j, te, tv: (te[i], jj(i, j, tv), 0))],
        out_specs=pl.BlockSpec((PACK_HALVES, tmg, PACK_W), lambda i, j, te, tv: (0, i, 0)),
        scratch_shapes=[pltpu.VMEM((tmg, d), BF16), pltpu.VMEM((tmg, d), F32)],
    )
    return pl.pallas_call(
        _grouped_kernel,
        grid_spec=grid_spec,
        out_shape=jax.ShapeDtypeStruct((PACK_HALVES, p, PACK_W), jnp.int32),
        compiler_params=_params("arbitrary", "arbitrary"),
        name="grouped_swiglu",
    )(tile_expert, tile_valid, xs, w13, w13, w2)


def _combine_kernel(*refs, final_norm):
    if final_norm:
        x_ref, mw_ref, y1_ref, y2_ref, gf_ref, o_ref = refs
    else:
        x_ref, mw_ref, y1_ref, y2_ref, o_ref = refs
    mw = mw_ref[...]
    out = x_ref[...] + mw[:, 0:1] * _load_packed(y1_ref) + mw[:, 1:2] * _load_packed(y2_ref)
    if final_norm:
        out = _rms(out, gf_ref[...])
    o_ref[...] = out


def _combine(x, mw, y12, g_final, *, tm):
    t, d = x.shape
    row = lambda i: (i, 0)
    in_specs = [pl.BlockSpec((tm, d), row), pl.BlockSpec((tm, LANES), row),
                pl.BlockSpec((PACK_HALVES, tm, PACK_W), lambda i: (0, i, 0)),
                pl.BlockSpec((PACK_HALVES, tm, PACK_W), lambda i: (0, t // tm + i, 0))]
    args = [x, mw, y12, y12]
    if g_final is not None:
        in_specs.append(pl.BlockSpec((1, d), lambda i: (0, 0)))
        args.append(g_final)
    return pl.pallas_call(
        functools.partial(_combine_kernel, final_norm=g_final is not None),
        grid=(t // tm,),
        in_specs=in_specs,
        out_specs=pl.BlockSpec((tm, d), row),
        out_shape=jax.ShapeDtypeStruct((t, d), F32),
        compiler_params=_params("parallel"),
        name="combine",
    )(*args)


def _gather_rows(table, idx):
    halves, n_rows, width = table.shape
    n = halves * idx.shape[0]
    assert n % SC_GATHER_WINDOW == 0
    flat_idx = jnp.concatenate([idx + h * n_rows for h in range(halves)]).reshape(1, n)
    mesh = plsc.VectorSubcoreMesh(core_axis_name="core", subcore_axis_name="subcore")

    @pl.kernel(out_type=jax.ShapeDtypeStruct((n, width), table.dtype), mesh=mesh)
    def gather(table_hbm, idx_hbm, out_hbm):
        def body(idx_vmem, out_vmem):
            pltpu.sync_copy(table_hbm.at[idx_vmem.at[0]], out_vmem)

        pltpu.emit_pipeline(
            body,
            grid=(n // SC_GATHER_WINDOW,),
            in_specs=[pl.BlockSpec((1, SC_GATHER_WINDOW), lambda i: (0, i))],
            out_specs=[pl.BlockSpec((SC_GATHER_WINDOW, width), lambda i: (i, 0))],
            core_axis_name=("core", "subcore"),
            dimension_semantics=(pltpu.PARALLEL,),
        )(idx_hbm, out_hbm)

    return gather(table.reshape(halves * n_rows, width), flat_idx).reshape(halves, idx.shape[0], width)


def _dispatch_rows(table, pos_list, n_out):
    halves, t, width = table.shape
    copies = len(pos_list)
    idx = jnp.concatenate([pos + h * n_out for h in range(halves) for pos in pos_list])
    n = idx.shape[0]
    win_t = t // SC_GATHER_WINDOW
    assert t % SC_GATHER_WINDOW == 0
    mesh = plsc.VectorSubcoreMesh(core_axis_name="core", subcore_axis_name="subcore")

    @pl.kernel(out_type=jax.ShapeDtypeStruct((halves * n_out, width), table.dtype), mesh=mesh)
    def scatter(table_hbm, idx_hbm, out_hbm):
        def body(rows_vmem, idx_vmem):
            pltpu.sync_copy(rows_vmem, out_hbm.at[idx_vmem.at[0]])

        pltpu.emit_pipeline(
            body,
            grid=(n // SC_GATHER_WINDOW,),
            in_specs=[pl.BlockSpec((SC_GATHER_WINDOW, width),
                                   lambda i: ((i // (copies * win_t)) * win_t + i % win_t, 0)),
                      pl.BlockSpec((1, SC_GATHER_WINDOW), lambda i: (0, i))],
            out_specs=[],
            core_axis_name=("core", "subcore"),
            dimension_semantics=(pltpu.PARALLEL,),
        )(table_hbm, idx_hbm)

    return scatter(table.reshape(halves * t, width), idx.reshape(1, n)).reshape(halves, n_out, width)


def _moe_sparse(x, g, router, w13, w2, g_final, *, tm, tmg, tf):
    t, d = x.shape
    hp, mi, mw, cnt = _router(x, g, router, tm=tm)
    counts = cnt[0, :N_EXPERTS].astype(jnp.int32)
    tiles_per_expert = (counts + tmg - 1) // tmg
    tile_end = jnp.cumsum(tiles_per_expert)
    group_start = (tile_end - tiles_per_expert) * tmg
    e1, e2, r1, r2 = mi[:, 0], mi[:, 1], mi[:, 2], mi[:, 3]
    pos1 = group_start[e1] + r1
    pos2 = group_start[e2] + r2
    n_tiles = 2 * t // tmg + N_EXPERTS
    tile = jnp.arange(n_tiles, dtype=jnp.int32)
    tile_expert = jnp.minimum(jnp.sum(tile[:, None] >= tile_end[None, :], axis=1), N_EXPERTS - 1).astype(jnp.int32)
    rows_left = counts[tile_expert] - (tile - (tile_end - tiles_per_expert)[tile_expert]) * tmg
    tile_valid = jnp.where(tile < tile_end[-1], jnp.clip(rows_left, 0, tmg), 0).astype(jnp.int32)

    xs = _dispatch_rows(hp, [pos1, pos2], n_tiles * tmg)
    ys = _grouped_swiglu(tile_expert, tile_valid, xs, w13, w2, tmg=tmg, tf=tf)
    y12 = _gather_rows(ys, jnp.concatenate([pos1, pos2]))
    return _combine(x, mw, y12, g_final, tm=tm)


def _cast_kernel(w_ref, o_ref):
    o_ref[...] = w_ref[...].astype(o_ref.dtype)


def _cast_bf16(w, *, rows):
    shape = w.shape
    w2 = w.reshape(-1, shape[-1])
    r, c = w2.shape
    out = pl.pallas_call(
        _cast_kernel,
        grid=(r // rows,),
        in_specs=[pl.BlockSpec((rows, c), lambda i: (i, 0))],
        out_specs=pl.BlockSpec((rows, c), lambda i: (i, 0)),
        out_shape=jax.ShapeDtypeStruct((r, c), BF16),
        compiler_params=_params("parallel"),
        name="cast_bf16",
    )(w2)
    return out.reshape(shape)


def _split_w_in(w_in):
    sizes = (512, 512, 1024, GLA_RANK, 1024, 512, 512, 1024, M_HEADS, M_HEADS, 1024, 1024, 1024)
    parts, off = [], 0
    for n in sizes:
        parts.append(w_in[:, off:off + n])
        off += n
    g_q, g_k, g_v, g_lr, g_g, m_q, m_k, m_v, m_i, m_f, m_o, a_g, a_m = parts
    wz = jnp.concatenate([g_q, g_k, g_v, g_g, m_q, m_k, m_v, m_o, a_g, a_m], axis=1).astype(BF16)
    d = w_in.shape[0]
    gate_cols = jnp.stack([m_i, m_f], axis=2).reshape(d, 2 * M_HEADS)
    ws = jnp.concatenate([jnp.pad(g_lr, ((0, 0), (0, LANES - GLA_RANK))),
                          jnp.pad(gate_cols, ((0, 0), (0, LANES - 2 * M_HEADS)))], axis=1).astype(BF16)
    assert wz.shape == (d, Z_COLS) and ws.shape == (d, ZS_COLS)
    return wz, ws


def _pick(total, want):
    t = min(total, want)
    while total % t:
        t -= 1
    return t


def kernel(x, mem, norm_mix, w_in, gla_gk_up, gla_gk_bias, gla_norm, m_conv, m_gate_bias, m_norm, gla_proj, m_proj, w_out, norm_xattn, norm_mem, x_wq, x_wkv, x_wo, norm_ffn, ffn_w13, ffn_w2, moe_router, moe_w13, moe_w2, norm_final):
    b, s, d = x.shape
    m_len = mem.shape[1]
    t = b * s
    depth = norm_mix.shape[0]
    tm = _pick(t, 512)
    row2 = lambda a: a.reshape(1, -1)

    xt = x.reshape(t, d)
    mem_t = mem.reshape(b * m_len, d)
    for l in range(depth):
        last_layer = l == depth - 1
        wz, ws = _split_w_in(w_in[l])
        z, zs = _norm_matmul(xt, row2(norm_mix[l]), wz, ws, tm=_pick(t, 1024), tn=1024)
        z3 = z.reshape(b, s, Z_COLS)
        zs3 = zs.reshape(b, s, ZS_COLS)
        up = jnp.pad(gla_gk_up[l], ((0, LANES - GLA_RANK), (0, 0))).astype(BF16)
        o_gla = _gla(z3, zs3, up, row2(gla_gk_bias[l]), row2(gla_norm[l]))
        gbias = jnp.pad(jnp.stack([m_gate_bias[l, :M_HEADS], m_gate_bias[l, M_HEADS:]], axis=1).reshape(1, -1),
                        ((0, 0), (0, LANES - 2 * M_HEADS)))
        grow, gcol = _mlstm_gates(zs3, gbias)
        o_m = _mlstm(z3, grow, gcol, m_conv[l], row2(m_norm[l]))
        xt = _merge(xt, o_gla.reshape(t, d), o_m.reshape(t, d), z,
                    gla_proj[l].astype(BF16), m_proj[l].astype(BF16), w_out[l].astype(BF16), tm=tm)
        kv = _norm_matmul(mem_t, row2(norm_mem[l]), x_wkv[l].astype(BF16), tm=_pick(b * m_len, 512), tn=1024)
        xt = _xattn(xt.reshape(b, s, d), row2(norm_xattn[l]), x_wq[l].astype(BF16),
                    kv.reshape(b, m_len, 2 * d), x_wo[l].astype(BF16), tq=_pick(s, 512)).reshape(t, d)
        g_final = row2(norm_final) if last_layer else None
        if l % 2 == 0:
            w13, w2 = ffn_w13[l // 2], ffn_w2[l // 2]
            f = w2.shape[0]
            fp = -(-f // (2 * LANES)) * (2 * LANES)
            w1 = jnp.pad(w13[:, :f], ((0, 0), (0, fp - f))).astype(BF16)
            w3 = jnp.pad(w13[:, f:], ((0, 0), (0, fp - f))).astype(BF16)
            w2p = jnp.pad(w2, ((0, fp - f), (0, 0))).astype(BF16)
            xt = _ffn(xt, row2(norm_ffn[l]), w1, w3, w2p, g_final, tm=tm, tf=fp // 2)
        else:
            router = jnp.pad(moe_router[l // 2], ((0, 0), (0, LANES - N_EXPERTS)))
            w13 = _cast_bf16(moe_w13[l // 2], rows=256)
            w2 = _cast_bf16(moe_w2[l // 2], rows=512)
            xt = _moe_sparse(xt, row2(norm_ffn[l]), router, w13, w2, g_final, tm=tm, tmg=tm, tf=w2.shape[1] // 4)
    return xt.reshape(b, s, d)
```

```python
import functools

import jax
import jax.numpy as jnp
from jax import lax
from jax.experimental import pallas as pl
from jax.experimental.pallas import tpu as pltpu
from jax.experimental.pallas import tpu_sc as plsc

F32 = jnp.float32
BF16 = jnp.bfloat16

EPS = 1e-6
D_MODEL = 1024
GLA_HEADS = 4
GLA_DK = 128
GLA_DV = 256
GLA_RANK = 16
GLA_GATE_NORM = 16.0
GLA_LOG_DECAY_MIN = -1.0
GLA_CHUNK = 64
M_HEADS = 4
M_DK = 128
M_DV = 256
M_CHUNK = 128
CONV_W = 4
X_HEADS = 4
N_EXPERTS = 8
LANES = 128
CONV_HALO = 8
VMEM_LIMIT = 48 * 1024 * 1024
SC_GATHER_WINDOW = 128
PACK_HALVES = 2
PACK_W = D_MODEL // 2 // PACK_HALVES

Z_GQ, Z_GK, Z_GV, Z_GG = 0, 512, 1024, 2048
Z_MQ, Z_MK, Z_MV, Z_MO = 3072, 3584, 4096, 5120
Z_AG, Z_AM = 6144, 7168
Z_COLS = 8192
ZS_COLS = 2 * LANES


def _params(*sem):
    return pltpu.CompilerParams(dimension_semantics=sem, vmem_limit_bytes=VMEM_LIMIT)


def _rms(x, g):
    return x * lax.rsqrt(jnp.mean(x * x, axis=-1, keepdims=True) + EPS) * g


def _log_sigmoid(u):
    return jnp.minimum(u, 0.0) - jnp.log(1.0 + jnp.exp(-jnp.abs(u)))


def _sigmoid(u):
    return 1.0 / (1.0 + jnp.exp(-u))


def _dot(a, b):
    return jnp.dot(a, b, preferred_element_type=F32)


def _dot_nt(a, b):
    return lax.dot_general(a, b, (((1,), (1,)), ((), ())), preferred_element_type=F32)


def _dot_tn(a, b):
    return lax.dot_general(a, b, (((0,), (0,)), ((), ())), preferred_element_type=F32)


def _cumsum_rows(x):
    n = x.shape[0]
    row = lax.broadcasted_iota(jnp.int32, x.shape, 0)
    s = 1
    while s < n:
        x = x + jnp.where(row >= s, pltpu.roll(x, s, 0), 0.0)
        s *= 2
    return x


def _norm_matmul_kernel(*refs, with_small):
    if with_small:
        x_ref, g_ref, w_ref, ws_ref, z_ref, zs_ref, h_scr = refs
    else:
        x_ref, g_ref, w_ref, z_ref, h_scr = refs

    @pl.when(pl.program_id(1) == 0)
    def _():
        h = _rms(x_ref[...], g_ref[...]).astype(BF16)
        h_scr[...] = h
        if with_small:
            zs_ref[...] = _dot(h, ws_ref[...])

    z_ref[...] = _dot(h_scr[...], w_ref[...]).astype(z_ref.dtype)


def _norm_matmul(x, g, w, ws=None, *, tm, tn):
    t, d = x.shape
    n = w.shape[1]
    in_specs = [pl.BlockSpec((tm, d), lambda i, j: (i, 0)),
                pl.BlockSpec((1, d), lambda i, j: (0, 0)),
                pl.BlockSpec((d, tn), lambda i, j: (0, j))]
    out_specs = [pl.BlockSpec((tm, tn), lambda i, j: (i, j))]
    out_shape = [jax.ShapeDtypeStruct((t, n), BF16)]
    args = [x, g, w]
    if ws is not None:
        ns = ws.shape[1]
        in_specs.append(pl.BlockSpec((d, ns), lambda i, j: (0, 0)))
        out_specs.append(pl.BlockSpec((tm, ns), lambda i, j: (i, 0)))
        out_shape.append(jax.ShapeDtypeStruct((t, ns), F32))
        args.append(ws)
    out = pl.pallas_call(
        functools.partial(_norm_matmul_kernel, with_small=ws is not None),
        grid=(t // tm, n // tn),
        in_specs=in_specs, out_specs=out_specs, out_shape=out_shape,
        scratch_shapes=[pltpu.VMEM((tm, d), BF16)],
        compiler_params=_params("parallel", "arbitrary"),
        name="norm_matmul",
    )(*args)
    return out if ws is not None else out[0]


def _gla_kernel(q_ref, k_ref, v_ref, gg_ref, lr_ref, up_ref, gb_ref, nw_ref, o_ref, la_scr, st_scr):
    seq = q_ref.shape[0]
    c = GLA_CHUNK
    scale = GLA_DK ** -0.5

    u = _dot(lr_ref[...].astype(BF16), up_ref[...]) + gb_ref[...]
    la_scr[...] = jnp.maximum(_log_sigmoid(u) * (1.0 / GLA_GATE_NORM), GLA_LOG_DECAY_MIN)
    st_scr[...] = jnp.zeros_like(st_scr)
    causal = (lax.broadcasted_iota(jnp.int32, (c, c), 0) >= lax.broadcasted_iota(jnp.int32, (c, c), 1))

    def body(n, carry):
        r0 = pl.multiple_of(n * c, c)
        rows = pl.ds(r0, c)
        cum = _cumsum_rows(la_scr[rows, :])
        cum_last = cum[c - 1:c, :]
        q = q_ref[rows, :].astype(F32) * scale
        k = k_ref[rows, :].astype(F32)
        v = v_ref[rows, :]
        q_dec = (q * jnp.exp(cum)).astype(BF16)
        k_inv = (k * jnp.exp(-cum)).astype(BF16)
        k_end = (k * jnp.exp(cum_last - cum)).astype(BF16)
        scores = jnp.where(causal, _dot_nt(q_dec, k_inv), 0.0).astype(BF16)
        st = st_scr[...]
        o = _dot(scores, v) + _dot_nt(q_dec, st.astype(BF16))
        st_scr[...] = st * jnp.exp(cum_last) + _dot_tn(v, k_end)
        gate = gg_ref[rows, :].astype(F32)
        o_ref[rows, :] = (_rms(o, nw_ref[...]) * (gate * _sigmoid(gate))).astype(o_ref.dtype)
        return carry

    lax.fori_loop(0, seq // c, body, 0, unroll=8)


def _gla(z3, zs3, up, gb, nw):
    b, s, _ = z3.shape
    qb, kb, vb, gb_ = Z_GQ // GLA_DK, Z_GK // GLA_DK, Z_GV // GLA_DV, Z_GG // GLA_DV
    return pl.pallas_call(
        _gla_kernel,
        grid=(b, GLA_HEADS),
        in_specs=[pl.BlockSpec((None, s, GLA_DK), lambda i, h: (i, 0, qb + h)),
                  pl.BlockSpec((None, s, GLA_DK), lambda i, h: (i, 0, kb + h)),
                  pl.BlockSpec((None, s, GLA_DV), lambda i, h: (i, 0, vb + h)),
                  pl.BlockSpec((None, s, GLA_DV), lambda i, h: (i, 0, gb_ + h)),
                  pl.BlockSpec((None, s, LANES), lambda i, h: (i, 0, 0)),
                  pl.BlockSpec((LANES, GLA_DK), lambda i, h: (0, h)),
                  pl.BlockSpec((1, GLA_DK), lambda i, h: (0, h)),
                  pl.BlockSpec((1, GLA_DV), lambda i, h: (0, h))],
        out_specs=pl.BlockSpec((None, s, GLA_DV), lambda i, h: (i, 0, h)),
        out_shape=jax.ShapeDtypeStruct((b, s, GLA_HEADS * GLA_DV), BF16),
        scratch_shapes=[pltpu.VMEM((s, GLA_DK), F32), pltpu.VMEM((GLA_DV, GLA_DK), F32)],
        compiler_params=_params("parallel", "arbitrary"),
        name="gla",
    )(z3, z3, z3, z3, zs3, up, gb, nw)


def _mlstm_gates_kernel(gt_ref, gb_ref, grow_ref, gcol_ref):
    seq = gt_ref.shape[0]
    c = M_CHUNK
    gates_t = (gt_ref[...] + gb_ref[...]).T
    top = gates_t[0:2 * M_HEADS, :]
    is_forget = lax.broadcasted_iota(jnp.int32, top.shape, 0) % 2 == 1
    top = jnp.where(is_forget, _log_sigmoid(top), top)
    lane_in_chunk = lax.broadcasted_iota(jnp.int32, top.shape, 1) % c
    step = 1
    while step < c:
        shifted = jnp.where(lane_in_chunk >= step, pltpu.roll(top, step, 1), 0.0)
        top = jnp.where(is_forget, top + shifted, top)
        step *= 2
    for h in range(M_HEADS):
        slab = jnp.concatenate([top[2 * h:2 * h + 2, :], jnp.zeros((LANES - 2, seq), F32)], axis=0)
        grow_ref[h] = slab[0:CONV_HALO, :]
        gcol_ref[h] = slab.T


def _mlstm_gates(zs3, gbias):
    b, s, _ = zs3.shape
    return pl.pallas_call(
        _mlstm_gates_kernel,
        grid=(b,),
        in_specs=[pl.BlockSpec((None, s, LANES), lambda i: (i, 0, 1)),
                  pl.BlockSpec((1, LANES), lambda i: (0, 0))],
        out_specs=[pl.BlockSpec((None, M_HEADS, CONV_HALO, s), lambda i: (i, 0, 0, 0)),
                   pl.BlockSpec((None, M_HEADS, s, LANES), lambda i: (i, 0, 0, 0))],
        out_shape=[jax.ShapeDtypeStruct((b, M_HEADS, CONV_HALO, s), F32),
                   jax.ShapeDtypeStruct((b, M_HEADS, s, LANES), F32)],
        compiler_params=_params("parallel"),
        name="mlstm_gates",
    )(zs3, gbias)


def _mlstm_kernel(q_ref, k_ref, v_ref, og_ref, grow_ref, gcol_ref, cq_ref, ck_ref, nw_ref, o_ref,
                  qp_scr, kp_scr, c_scr, n_scr, m_scr):
    seq = q_ref.shape[0]
    c = M_CHUNK
    scale = M_DK ** -0.5

    zero_halo = jnp.zeros((CONV_HALO, M_DK), F32)
    qp_scr[0:CONV_HALO, :] = zero_halo
    kp_scr[0:CONV_HALO, :] = zero_halo
    qp_scr[CONV_HALO:CONV_HALO + seq, :] = q_ref[...].astype(F32)
    kp_scr[CONV_HALO:CONV_HALO + seq, :] = k_ref[...].astype(F32)
    c_scr[...] = jnp.zeros_like(c_scr)
    n_scr[...] = jnp.zeros_like(n_scr)
    m_scr[...] = jnp.zeros_like(m_scr)
    causal = (lax.broadcasted_iota(jnp.int32, (c, c), 0) >= lax.broadcasted_iota(jnp.int32, (c, c), 1))

    def conv_silu(p_scr, w_ref, r0):
        blk = p_scr[pl.ds(r0, c + CONV_HALO), :]
        acc = None
        for j in range(CONV_W):
            lo = CONV_HALO - (CONV_W - 1) + j
            term = w_ref[j:j + 1, :] * blk[lo:lo + c, :]
            acc = term if acc is None else acc + term
        return acc * _sigmoid(acc)

    def body(n, carry):
        r0 = pl.multiple_of(n * c, c)
        rows = pl.ds(r0, c)
        q = conv_silu(qp_scr, cq_ref, r0) * scale
        k = conv_silu(kp_scr, ck_ref, r0)
        v = v_ref[rows, :]
        qb = q.astype(BF16)

        g_rows = grow_ref[:, rows]
        g_cols = gcol_ref[rows, :]
        b_col = g_cols[:, 1:2]
        a_col = g_cols[:, 0:1] - b_col
        a_row = g_rows[0:1, :] - g_rows[1:2, :]
        b_last = b_col[c - 1:c, :]
        m_prev = m_scr[:, 0:1]
        c_prev = c_scr[...]
        n_prev = n_scr[...]

        dmat = jnp.where(causal, b_col + a_row, -jnp.inf)
        m_inter = b_col + m_prev
        m_t = jnp.maximum(m_inter, jnp.max(dmat, axis=1, keepdims=True))
        w_ts = jnp.exp(dmat - m_t) * _dot_nt(qb, k.astype(BF16))
        s_inter = jnp.exp(m_inter - m_t)
        num = _dot(w_ts.astype(BF16), v) + s_inter * _dot(qb, c_prev.astype(BF16))
        den = (jnp.sum(w_ts, axis=1, keepdims=True)
               + s_inter * jnp.sum(q * n_prev, axis=1, keepdims=True))
        h = num / jnp.maximum(jnp.abs(den), jnp.exp(-m_t))

        m_new = jnp.maximum(b_last + m_prev, b_last + jnp.max(a_col, axis=0, keepdims=True))
        kw = k * jnp.exp(b_last + a_col - m_new)
        carry_scale = jnp.exp(b_last + m_prev - m_new)
        c_scr[...] = carry_scale * c_prev + _dot_tn(kw.astype(BF16), v)
        n_scr[...] = carry_scale * n_prev + jnp.sum(kw, axis=0, keepdims=True)
        m_scr[...] = jnp.broadcast_to(m_new, m_scr.shape)

        gate = og_ref[rows, :].astype(F32)
        o_ref[rows, :] = (_rms(h, nw_ref[...]) * _sigmoid(gate)).astype(o_ref.dtype)
        return carry

    lax.fori_loop(0, seq // c, body, 0, unroll=2)


def _mlstm(z3, grow, gcol, conv, nw):
    b, s, _ = z3.shape
    qb, kb, vb, ob = Z_MQ // M_DK, Z_MK // M_DK, Z_MV // M_DV, Z_MO // M_DV
    return pl.pallas_call(
        _mlstm_kernel,
        grid=(b, M_HEADS),
        in_specs=[pl.BlockSpec((None, s, M_DK), lambda i, h: (i, 0, qb + h)),
                  pl.BlockSpec((None, s, M_DK), lambda i, h: (i, 0, kb + h)),
                  pl.BlockSpec((None, s, M_DV), lambda i, h: (i, 0, vb + h)),
                  pl.BlockSpec((None, s, M_DV), lambda i, h: (i, 0, ob + h)),
                  pl.BlockSpec((None, None, CONV_HALO, s), lambda i, h: (i, h, 0, 0)),
                  pl.BlockSpec((None, None, s, LANES), lambda i, h: (i, h, 0, 0)),
                  pl.BlockSpec((CONV_W, M_DK), lambda i, h: (0, h)),
                  pl.BlockSpec((CONV_W, M_DK), lambda i, h: (0, M_HEADS + h)),
                  pl.BlockSpec((1, M_DV), lambda i, h: (0, h))],
        out_specs=pl.BlockSpec((None, s, M_DV), lambda i, h: (i, 0, h)),
        out_shape=jax.ShapeDtypeStruct((b, s, M_HEADS * M_DV), BF16),
        scratch_shapes=[pltpu.VMEM((s + CONV_HALO, M_DK), F32), pltpu.VMEM((s + CONV_HALO, M_DK), F32),
                        pltpu.VMEM((M_DK, M_DV), F32), pltpu.VMEM((1, M_DK), F32), pltpu.VMEM((1, LANES), F32)],
        compiler_params=_params("parallel", "arbitrary"),
        name="mlstm",
    )(z3, z3, z3, z3, grow, gcol, conv, conv, nw)


def _merge_kernel(x_ref, og_ref, om_ref, ag_ref, am_ref, wg_ref, wm_ref, wo_ref, o_ref):
    merged = (_sigmoid(ag_ref[...].astype(F32)) * _dot(og_ref[...], wg_ref[...])
              + _sigmoid(am_ref[...].astype(F32)) * _dot(om_ref[...], wm_ref[...]))
    o_ref[...] = x_ref[...] + _dot(merged.astype(BF16), wo_ref[...])


def _merge(x, o_gla, o_m, z, wg, wm, wo, *, tm):
    t, d = x.shape
    row = lambda i: (i, 0)
    full = lambda i: (0, 0)
    return pl.pallas_call(
        _merge_kernel,
        grid=(t // tm,),
        in_specs=[pl.BlockSpec((tm, d), row), pl.BlockSpec((tm, d), row), pl.BlockSpec((tm, d), row),
                  pl.BlockSpec((tm, d), lambda i: (i, Z_AG // D_MODEL)),
                  pl.BlockSpec((tm, d), lambda i: (i, Z_AM // D_MODEL)),
                  pl.BlockSpec((d, d), full), pl.BlockSpec((d, d), full), pl.BlockSpec((d, d), full)],
        out_specs=pl.BlockSpec((tm, d), row),
        out_shape=jax.ShapeDtypeStruct((t, d), F32),
        compiler_params=_params("parallel"),
        name="merge",
    )(x, o_gla, o_m, z, z, wg, wm, wo)


def _xattn_kernel(x_ref, g_ref, wq_ref, kv_ref, wo_ref, o_ref):
    d = x_ref.shape[1]
    dh = d // X_HEADS
    x = x_ref[...]
    q = _dot(_rms(x, g_ref[...]).astype(BF16), wq_ref[...]).astype(BF16)
    heads = []
    for h in range(X_HEADS):
        k = kv_ref[:, h * dh:(h + 1) * dh]
        v = kv_ref[:, d + h * dh:d + (h + 1) * dh]
        s = _dot_nt(q[:, h * dh:(h + 1) * dh], k) * dh ** -0.5
        p = jnp.exp(s - jnp.max(s, axis=-1, keepdims=True))
        p = p / jnp.sum(p, axis=-1, keepdims=True)
        heads.append(_dot(p.astype(BF16), v).astype(BF16))
    o_ref[...] = x + _dot(jnp.concatenate(heads, axis=-1), wo_ref[...])


def _xattn(x3, g, wq, kv, wo, *, tq):
    b, s, d = x3.shape
    m = kv.shape[1]
    return pl.pallas_call(
        _xattn_kernel,
        grid=(b, s // tq),
        in_specs=[pl.BlockSpec((None, tq, d), lambda i, j: (i, j, 0)),
                  pl.BlockSpec((1, d), lambda i, j: (0, 0)),
                  pl.BlockSpec((d, d), lambda i, j: (0, 0)),
                  pl.BlockSpec((None, m, 2 * d), lambda i, j: (i, 0, 0)),
                  pl.BlockSpec((d, d), lambda i, j: (0, 0))],
        out_specs=pl.BlockSpec((None, tq, d), lambda i, j: (i, j, 0)),
        out_shape=jax.ShapeDtypeStruct((b, s, d), F32),
        compiler_params=_params("parallel", "parallel"),
        name="xattn",
    )(x3, g, wq, kv, wo)


def _ffn_kernel(*refs, final_norm):
    if final_norm:
        x_ref, g_ref, w1_ref, w3_ref, w2_ref, gf_ref, o_ref, h_scr, acc_scr = refs
    else:
        x_ref, g_ref, w1_ref, w3_ref, w2_ref, o_ref, h_scr, acc_scr = refs
    j = pl.program_id(1)

    @pl.when(j == 0)
    def _():
        h_scr[...] = _rms(x_ref[...], g_ref[...]).astype(BF16)
        acc_scr[...] = jnp.zeros_like(acc_scr)

    h = h_scr[...]
    a = _dot(h, w1_ref[...])
    act = (a * _sigmoid(a) * _dot(h, w3_ref[...])).astype(BF16)
    acc_scr[...] += _dot(act, w2_ref[...])

    @pl.when(j == pl.num_programs(1) - 1)
    def _():
        out = x_ref[...] + acc_scr[...]
        if final_norm:
            out = _rms(out, gf_ref[...])
        o_ref[...] = out


def _ffn(x, g, w1, w3, w2, g_final, *, tm, tf):
    t, d = x.shape
    f = w2.shape[0]
    in_specs = [pl.BlockSpec((tm, d), lambda i, j: (i, 0)),
                pl.BlockSpec((1, d), lambda i, j: (0, 0)),
                pl.BlockSpec((d, tf), lambda i, j: (0, j)),
                pl.BlockSpec((d, tf), lambda i, j: (0, j)),
                pl.BlockSpec((tf, d), lambda i, j: (j, 0))]
    args = [x, g, w1, w3, w2]
    if g_final is not None:
        in_specs.append(pl.BlockSpec((1, d), lambda i, j: (0, 0)))
        args.append(g_final)
    return pl.pallas_call(
        functools.partial(_ffn_kernel, final_norm=g_final is not None),
        grid=(t // tm, f // tf),
        in_specs=in_specs,
        out_specs=pl.BlockSpec((tm, d), lambda i, j: (i, 0)),
        out_shape=jax.ShapeDtypeStruct((t, d), F32),
        scratch_shapes=[pltpu.VMEM((tm, d), BF16), pltpu.VMEM((tm, d), F32)],
        compiler_params=_params("parallel", "arbitrary"),
        name="dense_ffn",
    )(*args)


def _store_packed(ref, y):
    half = y.shape[1] // 2
    bits = lax.bitcast_convert_type(y.astype(BF16).astype(F32), jnp.int32)
    lo = lax.shift_right_logical(bits[:, :half], 16)
    hi = jnp.bitwise_and(bits[:, half:], jnp.int32(-65536))
    packed = jnp.bitwise_or(hi, lo)
    w = half // PACK_HALVES
    for p in range(PACK_HALVES):
        ref[p] = packed[:, p * w:(p + 1) * w]


def _load_packed(ref):
    packed = jnp.concatenate([ref[p] for p in range(PACK_HALVES)], axis=1)
    lo = lax.bitcast_convert_type(lax.shift_left(packed, 16), F32)
    hi = lax.bitcast_convert_type(jnp.bitwise_and(packed, jnp.int32(-65536)), F32)
    return jnp.concatenate([lo, hi], axis=1)


def _router_kernel(x_ref, g_ref, r_ref, hp_ref, mi_ref, mw_ref, cnt_ref, carry_scr):
    @pl.when(pl.program_id(0) == 0)
    def _():
        carry_scr[...] = jnp.zeros_like(carry_scr)

    rows = x_ref.shape[0]
    h = _rms(x_ref[...], g_ref[...])
    _store_packed(hp_ref, h)
    r = r_ref[...]
    h_hi = h.astype(BF16)
    h_lo = (h - h_hi.astype(F32)).astype(BF16)
    r_hi = r.astype(BF16)
    r_lo = (r - r_hi.astype(F32)).astype(BF16)
    logits = _dot(h_hi, r_hi) + (_dot(h_hi, r_lo) + _dot(h_lo, r_hi))
    lane = lax.broadcasted_iota(jnp.int32, logits.shape, 1)
    logits = jnp.where(lane < N_EXPERTS, logits, -jnp.inf)
    m1 = jnp.max(logits, axis=-1, keepdims=True)
    i1 = jnp.min(jnp.where(logits == m1, lane, LANES), axis=-1, keepdims=True)
    rest = jnp.where(lane == i1, -jnp.inf, logits)
    m2 = jnp.max(rest, axis=-1, keepdims=True)
    i2 = jnp.min(jnp.where(rest == m2, lane, LANES), axis=-1, keepdims=True)
    e2 = jnp.exp(m2 - m1)
    w1 = 1.0 / (1.0 + e2)
    w2 = e2 / (1.0 + e2)
    oh1 = (lane == i1).astype(F32)
    oh2 = (lane == i2).astype(F32)
    oh = oh1 + oh2
    earlier = (lax.broadcasted_iota(jnp.int32, (rows, rows), 0) > lax.broadcasted_iota(jnp.int32, (rows, rows), 1))
    before = _dot(earlier.astype(BF16), oh.astype(BF16)) + carry_scr[...]
    rank1 = jnp.sum(oh1 * before, axis=-1, keepdims=True).astype(jnp.int32)
    rank2 = jnp.sum(oh2 * before, axis=-1, keepdims=True).astype(jnp.int32)
    carry_scr[...] += jnp.sum(oh, axis=0, keepdims=True)
    cnt_ref[...] = carry_scr[...]
    mi_ref[...] = jnp.where(lane == 0, i1, jnp.where(lane == 1, i2, jnp.where(lane == 2, rank1,
                            jnp.where(lane == 3, rank2, 0))))
    mw_ref[...] = jnp.where(lane == 0, w1, jnp.where(lane == 1, w2, 0.0))


def _router(x, g, router, *, tm):
    t, d = x.shape
    row = lambda i: (i, 0)
    fix = lambda i: (0, 0)
    return pl.pallas_call(
        _router_kernel,
        grid=(t // tm,),
        in_specs=[pl.BlockSpec((tm, d), row), pl.BlockSpec((1, d), fix), pl.BlockSpec((d, LANES), fix)],
        out_specs=[pl.BlockSpec((PACK_HALVES, tm, PACK_W), lambda i: (0, i, 0)), pl.BlockSpec((tm, LANES), row),
                   pl.BlockSpec((tm, LANES), row), pl.BlockSpec((1, LANES), fix)],
        out_shape=[jax.ShapeDtypeStruct((PACK_HALVES, t, PACK_W), jnp.int32),
                   jax.ShapeDtypeStruct((t, LANES), jnp.int32),
                   jax.ShapeDtypeStruct((t, LANES), F32), jax.ShapeDtypeStruct((1, LANES), F32)],
        scratch_shapes=[pltpu.VMEM((1, LANES), F32)],
        compiler_params=_params("arbitrary"),
        name="router",
    )(x, g, router)


def _grouped_kernel(te_ref, tv_ref, xs_ref, w1_ref, w3_ref, w2_ref, ys_ref, h_scr, acc_scr):
    i = pl.program_id(0)
    j = pl.program_id(1)
    last = pl.num_programs(1) - 1
    valid = tv_ref[i]
    active = valid > 0

    @pl.when(jnp.logical_and(active, j == 0))
    def _():
        h = _load_packed(xs_ref)
        row = lax.broadcasted_iota(jnp.int32, h.shape, 0)
        h_scr[...] = jnp.where(row < valid, h, 0.0).astype(BF16)
        acc_scr[...] = jnp.zeros_like(acc_scr)

    @pl.when(active)
    def _():
        h = h_scr[...]
        a = _dot(h, w1_ref[...])
        act = (a * _sigmoid(a) * _dot(h, w3_ref[...])).astype(BF16)
        acc_scr[...] += _dot(act, w2_ref[...])

    @pl.when(jnp.logical_and(active, j == last))
    def _():
        _store_packed(ys_ref, acc_scr[...])

    @pl.when(jnp.logical_and(jnp.logical_not(active), j == last))
    def _():
        ys_ref[...] = jnp.zeros_like(ys_ref)


def _grouped_swiglu(tile_expert, tile_valid, xs, w13, w2, *, tmg, tf):
    _, p, _ = xs.shape
    d = w2.shape[2]
    f = w2.shape[1]
    nj = f // tf

    def jj(i, j, tv):
        return jnp.where(tv[i] > 0, j, nj - 1)

    grid_spec = pltpu.PrefetchScalarGridSpec(
        num_scalar_prefetch=2,
        grid=(p // tmg, nj),
        in_specs=[pl.BlockSpec((PACK_HALVES, tmg, PACK_W), lambda i, j, te, tv: (0, i, 0)),
                  pl.BlockSpec((None, d, tf), lambda i, j, te, tv: (te[i], 0, jj(i, j, tv))),
                  pl.BlockSpec((None, d, tf), lambda i, j, te, tv: (te[i], 0, nj + jj(i, j, tv))),
                  pl.BlockSpec((None, tf, d), lambda i, j, te, tv: (te[i], jj(i, j, tv), 0))],
        out_specs=pl.BlockSpec((PACK_HALVES, tmg, PACK_W), lambda i, j, te, tv: (0, i, 0)),
        scratch_shapes=[pltpu.VMEM((tmg, d), BF16), pltpu.VMEM((tmg, d), F32)],
    )
    return pl.pallas_call(
        _grouped_kernel,
        grid_spec=grid_spec,
        out_shape=jax.ShapeDtypeStruct((PACK_HALVES, p, PACK_W), jnp.int32),
        compiler_params=_params("arbitrary", "arbitrary"),
        name="grouped_swiglu",
    )(tile_expert, tile_valid, xs, w13, w13, w2)


def _combine_kernel(*refs, final_norm):
    if final_norm:
        x_ref, mw_ref, y1_ref, y2_ref, gf_ref, o_ref = refs
    else:
        x_ref, mw_ref, y1_ref, y2_ref, o_ref = refs
    mw = mw_ref[...]
    out = x_ref[...] + mw[:, 0:1] * _load_packed(y1_ref) + mw[:, 1:2] * _load_packed(y2_ref)
    if final_norm:
        out = _rms(out, gf_ref[...])
    o_ref[...] = out


def _combine(x, mw, y12, g_final, *, tm):
    t, d = x.shape
    row = lambda i: (i, 0)
    in_specs = [pl.BlockSpec((tm, d), row), pl.BlockSpec((tm, LANES), row),
                pl.BlockSpec((PACK_HALVES, tm, PACK_W), lambda i: (0, i, 0)),
                pl.BlockSpec((PACK_HALVES, tm, PACK_W), lambda i: (0, t // tm + i, 0))]
    args = [x, mw, y12, y12]
    if g_final is not None:
        in_specs.append(pl.BlockSpec((1, d), lambda i: (0, 0)))
        args.append(g_final)
    return pl.pallas_call(
        functools.partial(_combine_kernel, final_norm=g_final is not None),
        grid=(t // tm,),
        in_specs=in_specs,
        out_specs=pl.BlockSpec((tm, d), row),
        out_shape=jax.ShapeDtypeStruct((t, d), F32),
        compiler_params=_params("parallel"),
        name="combine",
    )(*args)


def _gather_rows(table, idx):
    halves, n_rows, width = table.shape
    n = halves * idx.shape[0]
    assert n % SC_GATHER_WINDOW == 0
    flat_idx = jnp.concatenate([idx + h * n_rows for h in range(halves)]).reshape(1, n)
    mesh = plsc.VectorSubcoreMesh(core_axis_name="core", subcore_axis_name="subcore")

    @pl.kernel(out_type=jax.ShapeDtypeStruct((n, width), table.dtype), mesh=mesh)
    def gather(table_hbm, idx_hbm, out_hbm):
        def body(idx_vmem, out_vmem):
            pltpu.sync_copy(table_hbm.at[idx_vmem.at[0]], out_vmem)

        pltpu.emit_pipeline(
            body,
            grid=(n // SC_GATHER_WINDOW,),
            in_specs=[pl.BlockSpec((1, SC_GATHER_WINDOW), lambda i: (0, i))],
            out_specs=[pl.BlockSpec((SC_GATHER_WINDOW, width), lambda i: (i, 0))],
            core_axis_name=("core", "subcore"),
            dimension_semantics=(pltpu.PARALLEL,),
        )(idx_hbm, out_hbm)

    return gather(table.reshape(halves * n_rows, width), flat_idx).reshape(halves, idx.shape[0], width)


def _dispatch_rows(table, pos_list, n_out):
    halves, t, width = table.shape
    copies = len(pos_list)
    idx = jnp.concatenate([pos + h * n_out for h in range(halves) for pos in pos_list])
    n = idx.shape[0]
    win_t = t // SC_GATHER_WINDOW
    assert t % SC_GATHER_WINDOW == 0
    mesh = plsc.VectorSubcoreMesh(core_axis_name="core", subcore_axis_name="subcore")

    @pl.kernel(out_type=jax.ShapeDtypeStruct((halves * n_out, width), table.dtype), mesh=mesh)
    def scatter(table_hbm, idx_hbm, out_hbm):
        def body(rows_vmem, idx_vmem):
            pltpu.sync_copy(rows_vmem, out_hbm.at[idx_vmem.at[0]])

        pltpu.emit_pipeline(
            body,
            grid=(n // SC_GATHER_WINDOW,),
            in_specs=[pl.BlockSpec((SC_GATHER_WINDOW, width),
                                   lambda i: ((i // (copies * win_t)) * win_t + i % win_t, 0)),
                      pl.BlockSpec((1, SC_GATHER_WINDOW), lambda i: (0, i))],
            out_specs=[],
            core_axis_name=("core", "subcore"),
            dimension_semantics=(pltpu.PARALLEL,),
        )(table_hbm, idx_hbm)

    return scatter(table.reshape(halves * t, width), idx.reshape(1, n)).reshape(halves, n_out, width)


def _moe_sparse(x, g, router, w13, w2, g_final, *, tm, tmg, tf):
    t, d = x.shape
    hp, mi, mw, cnt = _router(x, g, router, tm=tm)
    counts = cnt[0, :N_EXPERTS].astype(jnp.int32)
    tiles_per_expert = (counts + tmg - 1) // tmg
    tile_end = jnp.cumsum(tiles_per_expert)
    group_start = (tile_end - tiles_per_expert) * tmg
    e1, e2, r1, r2 = mi[:, 0], mi[:, 1], mi[:, 2], mi[:, 3]
    pos1 = group_start[e1] + r1
    pos2 = group_start[e2] + r2
    n_tiles = 2 * t // tmg + N_EXPERTS
    tile = jnp.arange(n_tiles, dtype=jnp.int32)
    tile_expert = jnp.minimum(jnp.sum(tile[:, None] >= tile_end[None, :], axis=1), N_EXPERTS - 1).astype(jnp.int32)
    rows_left = counts[tile_expert] - (tile - (tile_end - tiles_per_expert)[tile_expert]) * tmg
    tile_valid = jnp.where(tile < tile_end[-1], jnp.clip(rows_left, 0, tmg), 0).astype(jnp.int32)

    xs = _dispatch_rows(hp, [pos1, pos2], n_tiles * tmg)
    ys = _grouped_swiglu(tile_expert, tile_valid, xs, w13, w2, tmg=tmg, tf=tf)
    y12 = _gather_rows(ys, jnp.concatenate([pos1, pos2]))
    return _combine(x, mw, y12, g_final, tm=tm)


def _cast_kernel(w_ref, o_ref):
    o_ref[...] = w_ref[...].astype(o_ref.dtype)


def _cast_bf16(w, *, rows):
    shape = w.shape
    w2 = w.reshape(-1, shape[-1])
    r, c = w2.shape
    out = pl.pallas_call(
        _cast_kernel,
        grid=(r // rows,),
        in_specs=[pl.BlockSpec((rows, c), lambda i: (i, 0))],
        out_specs=pl.BlockSpec((rows, c), lambda i: (i, 0)),
        out_shape=jax.ShapeDtypeStruct((r, c), BF16),
        compiler_params=_params("parallel"),
        name="cast_bf16",
    )(w2)
    return out.reshape(shape)


def _split_w_in(w_in):
    sizes = (512, 512, 1024, GLA_RANK, 1024, 512, 512, 1024, M_HEADS, M_HEADS, 1024, 1024, 1024)
    parts, off = [], 0
    for n in sizes:
        parts.append(w_in[:, off:off + n])
        off += n
    g_q, g_k, g_v, g_lr, g_g, m_q, m_k, m_v, m_i, m_f, m_o, a_g, a_m = parts
    wz = jnp.concatenate([g_q, g_k, g_v, g_g, m_q, m_k, m_v, m_o, a_g, a_m], axis=1).astype(BF16)
    d = w_in.shape[0]
    gate_cols = jnp.stack([m_i, m_f], axis=2).reshape(d, 2 * M_HEADS)
    ws = jnp.concatenate([jnp.pad(g_lr, ((0, 0), (0, LANES - GLA_RANK))),
                          jnp.pad(gate_cols, ((0, 0), (0, LANES - 2 * M_HEADS)))], axis=1).astype(BF16)
    assert wz.shape == (d, Z_COLS) and ws.shape == (d, ZS_COLS)
    return wz, ws


def _pick(total, want):
    t = min(total, want)
    while total % t:
        t -= 1
    return t


def kernel(x, mem, norm_mix, w_in, gla_gk_up, gla_gk_bias, gla_norm, m_conv, m_gate_bias, m_norm, gla_proj, m_proj, w_out, norm_xattn, norm_mem, x_wq, x_wkv, x_wo, norm_ffn, ffn_w13, ffn_w2, moe_router, moe_w13, moe_w2, norm_final):
    b, s, d = x.shape
    m_len = mem.shape[1]
    t = b * s
    depth = norm_mix.shape[0]
    tm = _pick(t, 512)
    row2 = lambda a: a.reshape(1, -1)

    xt = x.reshape(t, d)
    mem_t = mem.reshape(b * m_len, d)
    for l in range(depth):
        last_layer = l == depth - 1
        wz, ws = _split_w_in(w_in[l])
        z, zs = _norm_matmul(xt, row2(norm_mix[l]), wz, ws, tm=_pick(t, 1024), tn=1024)
        z3 = z.reshape(b, s, Z_COLS)
        zs3 = zs.reshape(b, s, ZS_COLS)
        up = jnp.pad(gla_gk_up[l], ((0, LANES - GLA_RANK), (0, 0))).astype(BF16)
        o_gla = _gla(z3, zs3, up, row2(gla_gk_bias[l]), row2(gla_norm[l]))
        gbias = jnp.pad(jnp.stack([m_gate_bias[l, :M_HEADS], m_gate_bias[l, M_HEADS:]], axis=1).reshape(1, -1),
                        ((0, 0), (0, LANES - 2 * M_HEADS)))
        grow, gcol = _mlstm_gates(zs3, gbias)
        o_m = _mlstm(z3, grow, gcol, m_conv[l], row2(m_norm[l]))
        xt = _merge(xt, o_gla.reshape(t, d), o_m.reshape(t, d), z,
                    gla_proj[l].astype(BF16), m_proj[l].astype(BF16), w_out[l].astype(BF16), tm=tm)
        kv = _norm_matmul(mem_t, row2(norm_mem[l]), x_wkv[l].astype(BF16), tm=_pick(b * m_len, 512), tn=1024)
        xt = _xattn(xt.reshape(b, s, d), row2(norm_xattn[l]), x_wq[l].astype(BF16),
                    kv.reshape(b, m_len, 2 * d), x_wo[l].astype(BF16), tq=_pick(s, 1024)).reshape(t, d)
        g_final = row2(norm_final) if last_layer else None
        if l % 2 == 0:
            w13, w2 = ffn_w13[l // 2], ffn_w2[l // 2]
            f = w2.shape[0]
            fp = -(-f // (2 * LANES)) * (2 * LANES)
            w1 = jnp.pad(w13[:, :f], ((0, 0), (0, fp - f))).astype(BF16)
            w3 = jnp.pad(w13[:, f:], ((0, 0), (0, fp - f))).astype(BF16)
            w2p = jnp.pad(w2, ((0, fp - f), (0, 0))).astype(BF16)
            xt = _ffn(xt, row2(norm_ffn[l]), w1, w3, w2p, g_final, tm=tm, tf=fp // 2)
        else:
            router = jnp.pad(moe_router[l // 2], ((0, 0), (0, LANES - N_EXPERTS)))
            w13 = _cast_bf16(moe_w13[l // 2], rows=256)
            w2 = _cast_bf16(moe_w2[l // 2], rows=512)
            xt = _moe_sparse(xt, row2(norm_ffn[l]), router, w13, w2, g_final, tm=tm, tmg=tm, tf=w2.shape[1] // 2)
    return xt.reshape(b, s, d)
```

```python
import functools

import jax
import jax.numpy as jnp
from jax import lax
from jax.experimental import pallas as pl
from jax.experimental.pallas import tpu as pltpu
from jax.experimental.pallas import tpu_sc as plsc

F32 = jnp.float32
BF16 = jnp.bfloat16

EPS = 1e-6
D_MODEL = 1024
GLA_HEADS = 4
GLA_DK = 128
GLA_DV = 256
GLA_RANK = 16
GLA_GATE_NORM = 16.0
GLA_LOG_DECAY_MIN = -1.0
GLA_CHUNK = 64
M_HEADS = 4
M_DK = 128
M_DV = 256
M_CHUNK = 128
CONV_W = 4
X_HEADS = 4
N_EXPERTS = 8
LANES = 128
SUBLANES = 8
VMEM_LIMIT = 48 * 1024 * 1024
SC_GATHER_WINDOW = 128
PACK_HALVES = 2
PACK_W = D_MODEL // 2 // PACK_HALVES

Z_GQ, Z_GK, Z_GV, Z_GG = 0, 512, 1024, 2048
Z_MQ, Z_MK, Z_MV, Z_MO = 3072, 3584, 4096, 5120
Z_AG, Z_AM = 6144, 7168
Z_COLS = 8192
ZS_COLS = 2 * LANES


def _params(*sem):
    return pltpu.CompilerParams(dimension_semantics=sem, vmem_limit_bytes=VMEM_LIMIT)


def _rms(x, g):
    return x * lax.rsqrt(jnp.mean(x * x, axis=-1, keepdims=True) + EPS) * g


def _log_sigmoid(u):
    return jnp.minimum(u, 0.0) - jnp.log(1.0 + jnp.exp(-jnp.abs(u)))


def _sigmoid(u):
    return 1.0 / (1.0 + jnp.exp(-u))


def _dot(a, b):
    return jnp.dot(a, b, preferred_element_type=F32)


def _dot_nt(a, b):
    return lax.dot_general(a, b, (((1,), (1,)), ((), ())), preferred_element_type=F32)


def _dot_tn(a, b):
    return lax.dot_general(a, b, (((0,), (0,)), ((), ())), preferred_element_type=F32)


def _cumsum_rows(x):
    n = x.shape[0]
    row = lax.broadcasted_iota(jnp.int32, x.shape, 0)
    s = 1
    while s < n:
        x = x + jnp.where(row >= s, pltpu.roll(x, s, 0), 0.0)
        s *= 2
    return x


def _norm_matmul_kernel(*refs, with_small):
    if with_small:
        x_ref, g_ref, w_ref, ws_ref, z_ref, zs_ref, h_scr = refs
    else:
        x_ref, g_ref, w_ref, z_ref, h_scr = refs

    @pl.when(pl.program_id(1) == 0)
    def _():
        h = _rms(x_ref[...], g_ref[...]).astype(BF16)
        h_scr[...] = h
        if with_small:
            zs_ref[...] = _dot(h, ws_ref[...])

    z_ref[...] = _dot(h_scr[...], w_ref[...]).astype(z_ref.dtype)


def _norm_matmul(x, g, w, ws=None, *, tm, tn):
    t, d = x.shape
    n = w.shape[1]
    in_specs = [pl.BlockSpec((tm, d), lambda i, j: (i, 0)),
                pl.BlockSpec((1, d), lambda i, j: (0, 0)),
                pl.BlockSpec((d, tn), lambda i, j: (0, j))]
    out_specs = [pl.BlockSpec((tm, tn), lambda i, j: (i, j))]
    out_shape = [jax.ShapeDtypeStruct((t, n), BF16)]
    args = [x, g, w]
    if ws is not None:
        ns = ws.shape[1]
        in_specs.append(pl.BlockSpec((d, ns), lambda i, j: (0, 0)))
        out_specs.append(pl.BlockSpec((tm, ns), lambda i, j: (i, 0)))
        out_shape.append(jax.ShapeDtypeStruct((t, ns), F32))
        args.append(ws)
    out = pl.pallas_call(
        functools.partial(_norm_matmul_kernel, with_small=ws is not None),
        grid=(t // tm, n // tn),
        in_specs=in_specs, out_specs=out_specs, out_shape=out_shape,
        scratch_shapes=[pltpu.VMEM((tm, d), BF16)],
        compiler_params=_params("parallel", "arbitrary"),
        name="norm_matmul",
    )(*args)
    return out if ws is not None else out[0]


def _gla_kernel(q_ref, k_ref, v_ref, gg_ref, lr_ref, up_ref, gb_ref, nw_ref, o_ref, la_scr, st_scr):
    seq = q_ref.shape[0]
    c = GLA_CHUNK
    scale = GLA_DK ** -0.5

    u = _dot(lr_ref[...].astype(BF16), up_ref[...]) + gb_ref[...]
    la_scr[...] = jnp.maximum(_log_sigmoid(u) * (1.0 / GLA_GATE_NORM), GLA_LOG_DECAY_MIN)
    st_scr[...] = jnp.zeros_like(st_scr)
    causal = (lax.broadcasted_iota(jnp.int32, (c, c), 0) >= lax.broadcasted_iota(jnp.int32, (c, c), 1))

    def body(n, carry):
        r0 = pl.multiple_of(n * c, c)
        rows = pl.ds(r0, c)
        cum = _cumsum_rows(la_scr[rows, :])
        cum_last = cum[c - 1:c, :]
        q = q_ref[rows, :].astype(F32) * scale
        k = k_ref[rows, :].astype(F32)
        v = v_ref[rows, :]
        q_dec = (q * jnp.exp(cum)).astype(BF16)
        k_inv = (k * jnp.exp(-cum)).astype(BF16)
        k_end = (k * jnp.exp(cum_last - cum)).astype(BF16)
        scores = jnp.where(causal, _dot_nt(q_dec, k_inv), 0.0).astype(BF16)
        st = st_scr[...]
        o = _dot(scores, v) + _dot_nt(q_dec, st.astype(BF16))
        st_scr[...] = st * jnp.exp(cum_last) + _dot_tn(v, k_end)
        gate = gg_ref[rows, :].astype(F32)
        o_ref[rows, :] = (_rms(o, nw_ref[...]) * (gate * _sigmoid(gate))).astype(o_ref.dtype)
        return carry

    lax.fori_loop(0, seq // c, body, 0, unroll=8)


def _gla(z3, zs3, up, gb, nw):
    b, s, _ = z3.shape
    qb, kb, vb, gb_ = Z_GQ // GLA_DK, Z_GK // GLA_DK, Z_GV // GLA_DV, Z_GG // GLA_DV
    return pl.pallas_call(
        _gla_kernel,
        grid=(b, GLA_HEADS),
        in_specs=[pl.BlockSpec((None, s, GLA_DK), lambda i, h: (i, 0, qb + h)),
                  pl.BlockSpec((None, s, GLA_DK), lambda i, h: (i, 0, kb + h)),
                  pl.BlockSpec((None, s, GLA_DV), lambda i, h: (i, 0, vb + h)),
                  pl.BlockSpec((None, s, GLA_DV), lambda i, h: (i, 0, gb_ + h)),
                  pl.BlockSpec((None, s, LANES), lambda i, h: (i, 0, 0)),
                  pl.BlockSpec((LANES, GLA_DK), lambda i, h: (0, h)),
                  pl.BlockSpec((1, GLA_DK), lambda i, h: (0, h)),
                  pl.BlockSpec((1, GLA_DV), lambda i, h: (0, h))],
        out_specs=pl.BlockSpec((None, s, GLA_DV), lambda i, h: (i, 0, h)),
        out_shape=jax.ShapeDtypeStruct((b, s, GLA_HEADS * GLA_DV), BF16),
        scratch_shapes=[pltpu.VMEM((s, GLA_DK), F32), pltpu.VMEM((GLA_DV, GLA_DK), F32)],
        compiler_params=_params("parallel", "arbitrary"),
        name="gla",
    )(z3, z3, z3, z3, zs3, up, gb, nw)


def _chunk_scan_lanes(x, combine, identity, chunk):
    lane_in_chunk = lax.broadcasted_iota(jnp.int32, x.shape, 1) % chunk
    step = 1
    while step < chunk:
        x = combine(x, jnp.where(lane_in_chunk >= step, pltpu.roll(x, step, 1), identity))
        step *= 2
    return x


def _replicate_rows(x, row):
    hi = x.astype(BF16)
    r1 = x - hi.astype(F32)
    mid = r1.astype(BF16)
    lo = (r1 - mid.astype(F32)).astype(BF16)
    parts = jnp.concatenate([hi, mid, lo, jnp.zeros_like(hi)], axis=0)
    pick = lax.broadcasted_iota(jnp.int32, (4 * SUBLANES, LANES), 0) % SUBLANES == row
    return _dot_tn(parts, jnp.where(pick, 1.0, 0.0).astype(BF16))


def _mlstm_gates_kernel(gt_ref, gb_ref, arow_ref, rep_ref):
    c = M_CHUNK
    gates_t = (gt_ref[...] + gb_ref[...]).T
    top = gates_t[0:SUBLANES, :]
    odd = lax.broadcasted_iota(jnp.int32, top.shape, 0) % 2 == 1
    lf = jnp.where(odd, _log_sigmoid(top), 0.0)
    b_odd = _chunk_scan_lanes(lf, jnp.add, 0.0, c)
    a_even = top - pltpu.roll(b_odd, SUBLANES - 1, 0)
    amax_even = _chunk_scan_lanes(jnp.where(odd, -jnp.inf, a_even), jnp.maximum, -jnp.inf, c)
    for h in range(M_HEADS):
        arow_ref[h] = pltpu.roll(a_even, (SUBLANES - 2 * h) % SUBLANES, 0)
        rep_ref[h, 0] = _replicate_rows(b_odd, 2 * h + 1)
        rep_ref[h, 1] = _replicate_rows(a_even, 2 * h)
        rep_ref[h, 2] = _replicate_rows(jnp.where(odd, 0.0, amax_even), 2 * h)


def _mlstm_gates(zs3, gbias):
    b, s, _ = zs3.shape
    return pl.pallas_call(
        _mlstm_gates_kernel,
        grid=(b,),
        in_specs=[pl.BlockSpec((None, s, LANES), lambda i: (i, 0, 1)),
                  pl.BlockSpec((1, LANES), lambda i: (0, 0))],
        out_specs=[pl.BlockSpec((None, M_HEADS, SUBLANES, s), lambda i: (i, 0, 0, 0)),
                   pl.BlockSpec((None, M_HEADS, 3, s, LANES), lambda i: (i, 0, 0, 0, 0))],
        out_shape=[jax.ShapeDtypeStruct((b, M_HEADS, SUBLANES, s), F32),
                   jax.ShapeDtypeStruct((b, M_HEADS, 3, s, LANES), F32)],
        compiler_params=_params("parallel"),
        name="mlstm_gates",
    )(zs3, gbias)


def _mlstm_conv_kernel(cur_ref, prev_ref, w_ref, shift_ref, o_ref):
    c = M_CHUNK
    d = cur_ref.shape[1]
    prev = jnp.where(pl.program_id(1) > 0, prev_ref[...], jnp.zeros_like(prev_ref))
    lane = lax.broadcasted_iota(jnp.int32, (1, d), 1)
    col_scale = jnp.where(lane < d // 2, M_DK ** -0.5, 1.0)
    for i in range(cur_ref.shape[0] // c):
        cur = cur_ref[i * c:(i + 1) * c, :]
        before = prev if i == 0 else cur_ref[(i - 1) * c:i * c, :]
        shifted = _dot(shift_ref[...], jnp.concatenate([before, cur], axis=0))
        acc = w_ref[CONV_W - 1:CONV_W, :] * cur.astype(F32)
        for s in range(1, CONV_W):
            acc = acc + w_ref[CONV_W - 1 - s:CONV_W - s, :] * shifted[(s - 1) * c:s * c, :]
        o_ref[i * c:(i + 1) * c, :] = (acc * _sigmoid(acc) * col_scale).astype(o_ref.dtype)


def _mlstm_conv(z3, conv, *, ts):
    b, s, _ = z3.shape
    c = M_CHUNK
    d = conv.shape[1]
    blk = Z_MQ // d
    assert Z_MQ % d == 0 and Z_MK == Z_MQ + d // 2
    t_idx = jnp.arange(c)[:, None]
    col = jnp.arange(2 * c)[None, :]
    shift = jnp.concatenate([(col == c + t_idx - s) for s in range(1, CONV_W)], axis=0).astype(BF16)
    return pl.pallas_call(
        _mlstm_conv_kernel,
        grid=(b, s // ts),
        in_specs=[pl.BlockSpec((None, ts, d), lambda i, j: (i, j, blk)),
                  pl.BlockSpec((None, c, d), lambda i, j: (i, jnp.maximum(j * (ts // c) - 1, 0), blk)),
                  pl.BlockSpec((CONV_W, d), lambda i, j: (0, 0)),
                  pl.BlockSpec(((CONV_W - 1) * c, 2 * c), lambda i, j: (0, 0))],
        out_specs=pl.BlockSpec((None, ts, d), lambda i, j: (i, j, 0)),
        out_shape=jax.ShapeDtypeStruct((b, s, d), BF16),
        compiler_params=_params("parallel", "parallel"),
        name="mlstm_conv",
    )(z3, z3, conv, shift)


def _mlstm_kernel(q_ref, k_ref, v_ref, og_ref, arow_ref, rep_ref, nw_ref, o_ref, st_scr, m_scr):
    seq = q_ref.shape[0]
    c = M_CHUNK
    st_scr[...] = jnp.zeros_like(st_scr)
    m_scr[...] = jnp.zeros_like(m_scr)
    causal = (lax.broadcasted_iota(jnp.int32, (c, c), 0) >= lax.broadcasted_iota(jnp.int32, (c, c), 1))
    ones_cols = jnp.ones((c, LANES), BF16)
    ones_dv = jnp.ones((M_DV, LANES), BF16)

    def wide(x, n):
        return jnp.concatenate([x] * n, axis=1)

    def body(n, carry):
        r0 = pl.multiple_of(n * c, c)
        rows = pl.ds(r0, c)
        qb = q_ref[rows, :]
        kb = k_ref[rows, :]
        v_ext = jnp.concatenate([v_ref[rows, :], ones_cols], axis=1)
        b = rep_ref[0, rows, :]
        a = rep_ref[1, rows, :]
        amax = rep_ref[2, rows, :]
        a_row = arow_ref[0:1, rows]
        m_prev = m_scr[...]
        st = st_scr[...]

        dmat = jnp.where(causal, b + a_row, -jnp.inf)
        m_inter = b + m_prev
        m_t = jnp.maximum(m_inter, b + amax)
        w_ts = jnp.exp(dmat - m_t) * _dot_nt(qb, kb)
        s_inter = jnp.exp(m_inter - m_t)
        nd = _dot(w_ts.astype(BF16), v_ext) + wide(s_inter, 3) * _dot(qb, st.astype(BF16))
        inv = 1.0 / jnp.maximum(jnp.abs(nd[:, M_DV:]), jnp.exp(-m_t))
        h = nd[:, :M_DV] * wide(inv, 2)

        b_last = b[c - 1:c, :]
        m_new = jnp.maximum(b_last + m_prev, b_last + amax[c - 1:c, :])
        kw = kb.astype(F32) * jnp.exp(b_last + a - m_new)
        st_scr[...] = wide(jnp.exp(b_last + m_prev - m_new), 3) * st + _dot_tn(kw.astype(BF16), v_ext)
        m_scr[...] = m_new

        mean_sq = _dot((h * h).astype(BF16), ones_dv) * (1.0 / M_DV)
        gate = og_ref[rows, :].astype(F32)
        y = h * wide(lax.rsqrt(mean_sq + EPS), 2) * nw_ref[...]
        o_ref[rows, :] = (y * _sigmoid(gate)).astype(o_ref.dtype)
        return carry

    lax.fori_loop(0, seq // c, body, 0, unroll=8)


def _mlstm(qk3, z3, arow, rep, nw):
    b, s, _ = z3.shape
    vb, ob = Z_MV // M_DV, Z_MO // M_DV
    return pl.pallas_call(
        _mlstm_kernel,
        grid=(b, M_HEADS),
        in_specs=[pl.BlockSpec((None, s, M_DK), lambda i, h: (i, 0, h)),
                  pl.BlockSpec((None, s, M_DK), lambda i, h: (i, 0, M_HEADS + h)),
                  pl.BlockSpec((None, s, M_DV), lambda i, h: (i, 0, vb + h)),
                  pl.BlockSpec((None, s, M_DV), lambda i, h: (i, 0, ob + h)),
                  pl.BlockSpec((None, None, SUBLANES, s), lambda i, h: (i, h, 0, 0)),
                  pl.BlockSpec((None, None, 3, s, LANES), lambda i, h: (i, h, 0, 0, 0)),
                  pl.BlockSpec((1, M_DV), lambda i, h: (0, h))],
        out_specs=pl.BlockSpec((None, s, M_DV), lambda i, h: (i, 0, h)),
        out_shape=jax.ShapeDtypeStruct((b, s, M_HEADS * M_DV), BF16),
        scratch_shapes=[pltpu.VMEM((M_DK, M_DV + LANES), F32), pltpu.VMEM((1, LANES), F32)],
        compiler_params=_params("parallel", "arbitrary"),
        name="mlstm",
    )(qk3, qk3, z3, z3, arow, rep, nw)


def _merge_kernel(x_ref, og_ref, om_ref, ag_ref, am_ref, wg_ref, wm_ref, wo_ref, o_ref):
    merged = (_sigmoid(ag_ref[...].astype(F32)) * _dot(og_ref[...], wg_ref[...])
              + _sigmoid(am_ref[...].astype(F32)) * _dot(om_ref[...], wm_ref[...]))
    o_ref[...] = x_ref[...] + _dot(merged.astype(BF16), wo_ref[...])


def _merge(x, o_gla, o_m, z, wg, wm, wo, *, tm):
    t, d = x.shape
    row = lambda i: (i, 0)
    full = lambda i: (0, 0)
    return pl.pallas_call(
        _merge_kernel,
        grid=(t // tm,),
        in_specs=[pl.BlockSpec((tm, d), row), pl.BlockSpec((tm, d), row), pl.BlockSpec((tm, d), row),
                  pl.BlockSpec((tm, d), lambda i: (i, Z_AG // D_MODEL)),
                  pl.BlockSpec((tm, d), lambda i: (i, Z_AM // D_MODEL)),
                  pl.BlockSpec((d, d), full), pl.BlockSpec((d, d), full), pl.BlockSpec((d, d), full)],
        out_specs=pl.BlockSpec((tm, d), row),
        out_shape=jax.ShapeDtypeStruct((t, d), F32),
        compiler_params=_params("parallel"),
        name="merge",
    )(x, o_gla, o_m, z, z, wg, wm, wo)


def _xattn_kernel(x_ref, g_ref, wq_ref, kv_ref, wo_ref, o_ref):
    d = x_ref.shape[1]
    dh = d // X_HEADS
    x = x_ref[...]
    q = _dot(_rms(x, g_ref[...]).astype(BF16), wq_ref[...]).astype(BF16)
    heads = []
    for h in range(X_HEADS):
        k = kv_ref[:, h * dh:(h + 1) * dh]
        v = kv_ref[:, d + h * dh:d + (h + 1) * dh]
        s = _dot_nt(q[:, h * dh:(h + 1) * dh], k) * dh ** -0.5
        p = jnp.exp(s - jnp.max(s, axis=-1, keepdims=True))
        p = p / jnp.sum(p, axis=-1, keepdims=True)
        heads.append(_dot(p.astype(BF16), v).astype(BF16))
    o_ref[...] = x + _dot(jnp.concatenate(heads, axis=-1), wo_ref[...])


def _xattn(x3, g, wq, kv, wo, *, tq):
    b, s, d = x3.shape
    m = kv.shape[1]
    return pl.pallas_call(
        _xattn_kernel,
        grid=(b, s // tq),
        in_specs=[pl.BlockSpec((None, tq, d), lambda i, j: (i, j, 0)),
                  pl.BlockSpec((1, d), lambda i, j: (0, 0)),
                  pl.BlockSpec((d, d), lambda i, j: (0, 0)),
                  pl.BlockSpec((None, m, 2 * d), lambda i, j: (i, 0, 0)),
                  pl.BlockSpec((d, d), lambda i, j: (0, 0))],
        out_specs=pl.BlockSpec((None, tq, d), lambda i, j: (i, j, 0)),
        out_shape=jax.ShapeDtypeStruct((b, s, d), F32),
        compiler_params=_params("parallel", "parallel"),
        name="xattn",
    )(x3, g, wq, kv, wo)


def _ffn_kernel(*refs, final_norm):
    if final_norm:
        x_ref, g_ref, w1_ref, w3_ref, w2_ref, gf_ref, o_ref, h_scr, acc_scr = refs
    else:
        x_ref, g_ref, w1_ref, w3_ref, w2_ref, o_ref, h_scr, acc_scr = refs
    j = pl.program_id(1)

    @pl.when(j == 0)
    def _():
        h_scr[...] = _rms(x_ref[...], g_ref[...]).astype(BF16)
        acc_scr[...] = jnp.zeros_like(acc_scr)

    h = h_scr[...]
    a = _dot(h, w1_ref[...])
    act = (a * _sigmoid(a) * _dot(h, w3_ref[...])).astype(BF16)
    acc_scr[...] += _dot(act, w2_ref[...])

    @pl.when(j == pl.num_programs(1) - 1)
    def _():
        out = x_ref[...] + acc_scr[...]
        if final_norm:
            out = _rms(out, gf_ref[...])
        o_ref[...] = out


def _ffn(x, g, w1, w3, w2, g_final, *, tm, tf):
    t, d = x.shape
    f = w2.shape[0]
    in_specs = [pl.BlockSpec((tm, d), lambda i, j: (i, 0)),
                pl.BlockSpec((1, d), lambda i, j: (0, 0)),
                pl.BlockSpec((d, tf), lambda i, j: (0, j)),
                pl.BlockSpec((d, tf), lambda i, j: (0, j)),
                pl.BlockSpec((tf, d), lambda i, j: (j, 0))]
    args = [x, g, w1, w3, w2]
    if g_final is not None:
        in_specs.append(pl.BlockSpec((1, d), lambda i, j: (0, 0)))
        args.append(g_final)
    return pl.pallas_call(
        functools.partial(_ffn_kernel, final_norm=g_final is not None),
        grid=(t // tm, f // tf),
        in_specs=in_specs,
        out_specs=pl.BlockSpec((tm, d), lambda i, j: (i, 0)),
        out_shape=jax.ShapeDtypeStruct((t, d), F32),
        scratch_shapes=[pltpu.VMEM((tm, d), BF16), pltpu.VMEM((tm, d), F32)],
        compiler_params=_params("parallel", "arbitrary"),
        name="dense_ffn",
    )(*args)


def _store_packed(ref, y):
    half = y.shape[1] // 2
    bits = lax.bitcast_convert_type(y.astype(BF16).astype(F32), jnp.int32)
    lo = lax.shift_right_logical(bits[:, :half], 16)
    hi = jnp.bitwise_and(bits[:, half:], jnp.int32(-65536))
    packed = jnp.bitwise_or(hi, lo)
    w = half // PACK_HALVES
    for p in range(PACK_HALVES):
        ref[p] = packed[:, p * w:(p + 1) * w]


def _load_packed(ref):
    packed = jnp.concatenate([ref[p] for p in range(PACK_HALVES)], axis=1)
    lo = lax.bitcast_convert_type(lax.shift_left(packed, 16), F32)
    hi = lax.bitcast_convert_type(jnp.bitwise_and(packed, jnp.int32(-65536)), F32)
    return jnp.concatenate([lo, hi], axis=1)


def _router_kernel(x_ref, g_ref, r_ref, hp_ref, mi_ref, mw_ref, cnt_ref, carry_scr):
    @pl.when(pl.program_id(0) == 0)
    def _():
        carry_scr[...] = jnp.zeros_like(carry_scr)

    rows = x_ref.shape[0]
    h = _rms(x_ref[...], g_ref[...])
    _store_packed(hp_ref, h)
    r = r_ref[...]
    h_hi = h.astype(BF16)
    h_lo = (h - h_hi.astype(F32)).astype(BF16)
    r_hi = r.astype(BF16)
    r_lo = (r - r_hi.astype(F32)).astype(BF16)
    logits = _dot(h_hi, r_hi) + (_dot(h_hi, r_lo) + _dot(h_lo, r_hi))
    lane = lax.broadcasted_iota(jnp.int32, logits.shape, 1)
    logits = jnp.where(lane < N_EXPERTS, logits, -jnp.inf)
    m1 = jnp.max(logits, axis=-1, keepdims=True)
    i1 = jnp.min(jnp.where(logits == m1, lane, LANES), axis=-1, keepdims=True)
    rest = jnp.where(lane == i1, -jnp.inf, logits)
    m2 = jnp.max(rest, axis=-1, keepdims=True)
    i2 = jnp.min(jnp.where(rest == m2, lane, LANES), axis=-1, keepdims=True)
    e2 = jnp.exp(m2 - m1)
    w1 = 1.0 / (1.0 + e2)
    w2 = e2 / (1.0 + e2)
    oh1 = (lane == i1).astype(F32)
    oh2 = (lane == i2).astype(F32)
    oh = oh1 + oh2
    earlier = (lax.broadcasted_iota(jnp.int32, (rows, rows), 0) > lax.broadcasted_iota(jnp.int32, (rows, rows), 1))
    before = _dot(earlier.astype(BF16), oh.astype(BF16)) + carry_scr[...]
    rank1 = jnp.sum(oh1 * before, axis=-1, keepdims=True).astype(jnp.int32)
    rank2 = jnp.sum(oh2 * before, axis=-1, keepdims=True).astype(jnp.int32)
    carry_scr[...] += jnp.sum(oh, axis=0, keepdims=True)
    cnt_ref[...] = carry_scr[...]
    mi_ref[...] = jnp.where(lane == 0, i1, jnp.where(lane == 1, i2, jnp.where(lane == 2, rank1,
                            jnp.where(lane == 3, rank2, 0))))
    mw_ref[...] = jnp.where(lane == 0, w1, jnp.where(lane == 1, w2, 0.0))


def _router(x, g, router, *, tm):
    t, d = x.shape
    row = lambda i: (i, 0)
    fix = lambda i: (0, 0)
    return pl.pallas_call(
        _router_kernel,
        grid=(t // tm,),
        in_specs=[pl.BlockSpec((tm, d), row), pl.BlockSpec((1, d), fix), pl.BlockSpec((d, LANES), fix)],
        out_specs=[pl.BlockSpec((PACK_HALVES, tm, PACK_W), lambda i: (0, i, 0)), pl.BlockSpec((tm, LANES), row),
                   pl.BlockSpec((tm, LANES), row), pl.BlockSpec((1, LANES), fix)],
        out_shape=[jax.ShapeDtypeStruct((PACK_HALVES, t, PACK_W), jnp.int32),
                   jax.ShapeDtypeStruct((t, LANES), jnp.int32),
                   jax.ShapeDtypeStruct((t, LANES), F32), jax.ShapeDtypeStruct((1, LANES), F32)],
        scratch_shapes=[pltpu.VMEM((1, LANES), F32)],
        compiler_params=_params("arbitrary"),
        name="router",
    )(x, g, router)


def _grouped_kernel(te_ref, tv_ref, xs_ref, w1_ref, w3_ref, w2_ref, ys_ref, h_scr, acc_scr):
    i = pl.program_id(0)
    j = pl.program_id(1)
    last = pl.num_programs(1) - 1
    valid = tv_ref[i]
    active = valid > 0

    @pl.when(jnp.logical_and(active, j == 0))
    def _():
        h = _load_packed(xs_ref)
        row = lax.broadcasted_iota(jnp.int32, h.shape, 0)
        h_scr[...] = jnp.where(row < valid, h, 0.0).astype(BF16)
        acc_scr[...] = jnp.zeros_like(acc_scr)

    @pl.when(active)
    def _():
        h = h_scr[...]
        a = _dot(h, w1_ref[...])
        act = (a * _sigmoid(a) * _dot(h, w3_ref[...])).astype(BF16)
        acc_scr[...] += _dot(act, w2_ref[...])

    @pl.when(jnp.logical_and(active, j == last))
    def _():
        _store_packed(ys_ref, acc_scr[...])

    @pl.when(jnp.logical_and(jnp.logical_not(active), j == last))
    def _():
        ys_ref[...] = jnp.zeros_like(ys_ref)


def _grouped_swiglu(tile_expert, tile_valid, xs, w13, w2, *, tmg, tf):
    _, p, _ = xs.shape
    d = w2.shape[2]
    f = w2.shape[1]
    nj = f // tf

    def jj(i, j, tv):
        return jnp.where(tv[i] > 0, j, nj - 1)

    grid_spec = pltpu.PrefetchScalarGridSpec(
        num_scalar_prefetch=2,
        grid=(p // tmg, nj),
        in_specs=[pl.BlockSpec((PACK_HALVES, tmg, PACK_W), lambda i, j, te, tv: (0, i, 0)),
                  pl.BlockSpec((None, d, tf), lambda i, j, te, tv: (te[i], 0, jj(i, j, tv))),
                  pl.BlockSpec((None, d, tf), lambda i, j, te, tv: (te[i], 0, nj + jj(i, j, tv))),
                  pl.BlockSpec((None, tf, d), lambda i, j, te, tv: (te[i], jj(i, j, tv), 0))],
        out_specs=pl.BlockSpec((PACK_HALVES, tmg, PACK_W), lambda i, j, te, tv: (0, i, 0)),
        scratch_shapes=[pltpu.VMEM((tmg, d), BF16), pltpu.VMEM((tmg, d), F32)],
    )
    return pl.pallas_call(
        _grouped_kernel,
        grid_spec=grid_spec,
        out_shape=jax.ShapeDtypeStruct((PACK_HALVES, p, PACK_W), jnp.int32),
        compiler_params=_params("arbitrary", "arbitrary"),
        name="grouped_swiglu",
    )(tile_expert, tile_valid, xs, w13, w13, w2)


def _combine_kernel(*refs, final_norm):
    if final_norm:
        x_ref, mw_ref, y1_ref, y2_ref, gf_ref, o_ref = refs
    else:
        x_ref, mw_ref, y1_ref, y2_ref, o_ref = refs
    mw = mw_ref[...]
    out = x_ref[...] + mw[:, 0:1] * _load_packed(y1_ref) + mw[:, 1:2] * _load_packed(y2_ref)
    if final_norm:
        out = _rms(out, gf_ref[...])
    o_ref[...] = out


def _combine(x, mw, y12, g_final, *, tm):
    t, d = x.shape
    row = lambda i: (i, 0)
    in_specs = [pl.BlockSpec((tm, d), row), pl.BlockSpec((tm, LANES), row),
                pl.BlockSpec((PACK_HALVES, tm, PACK_W), lambda i: (0, i, 0)),
                pl.BlockSpec((PACK_HALVES, tm, PACK_W), lambda i: (0, t // tm + i, 0))]
    args = [x, mw, y12, y12]
    if g_final is not None:
        in_specs.append(pl.BlockSpec((1, d), lambda i: (0, 0)))
        args.append(g_final)
    return pl.pallas_call(
        functools.partial(_combine_kernel, final_norm=g_final is not None),
        grid=(t // tm,),
        in_specs=in_specs,
        out_specs=pl.BlockSpec((tm, d), row),
        out_shape=jax.ShapeDtypeStruct((t, d), F32),
        compiler_params=_params("parallel"),
        name="combine",
    )(*args)


def _gather_rows(table, idx):
    halves, n_rows, width = table.shape
    n = halves * idx.shape[0]
    assert n % SC_GATHER_WINDOW == 0
    flat_idx = jnp.concatenate([idx + h * n_rows for h in range(halves)]).reshape(1, n)
    mesh = plsc.VectorSubcoreMesh(core_axis_name="core", subcore_axis_name="subcore")

    @pl.kernel(out_type=jax.ShapeDtypeStruct((n, width), table.dtype), mesh=mesh)
    def gather(table_hbm, idx_hbm, out_hbm):
        def body(idx_vmem, out_vmem):
            pltpu.sync_copy(table_hbm.at[idx_vmem.at[0]], out_vmem)

        pltpu.emit_pipeline(
            body,
            grid=(n // SC_GATHER_WINDOW,),
            in_specs=[pl.BlockSpec((1, SC_GATHER_WINDOW), lambda i: (0, i))],
            out_specs=[pl.BlockSpec((SC_GATHER_WINDOW, width), lambda i: (i, 0))],
            core_axis_name=("core", "subcore"),
            dimension_semantics=(pltpu.PARALLEL,),
        )(idx_hbm, out_hbm)

    return gather(table.reshape(halves * n_rows, width), flat_idx).reshape(halves, idx.shape[0], width)


def _dispatch_rows(table, pos_list, n_out):
    halves, t, width = table.shape
    copies = len(pos_list)
    idx = jnp.concatenate([pos + h * n_out for h in range(halves) for pos in pos_list])
    n = idx.shape[0]
    win_t = t // SC_GATHER_WINDOW
    assert t % SC_GATHER_WINDOW == 0
    mesh = plsc.VectorSubcoreMesh(core_axis_name="core", subcore_axis_name="subcore")

    @pl.kernel(out_type=jax.ShapeDtypeStruct((halves * n_out, width), table.dtype), mesh=mesh)
    def scatter(table_hbm, idx_hbm, out_hbm):
        def body(rows_vmem, idx_vmem):
            pltpu.sync_copy(rows_vmem, out_hbm.at[idx_vmem.at[0]])

        pltpu.emit_pipeline(
            body,
            grid=(n // SC_GATHER_WINDOW,),
            in_specs=[pl.BlockSpec((SC_GATHER_WINDOW, width),
                                   lambda i: ((i // (copies * win_t)) * win_t + i % win_t, 0)),
                      pl.BlockSpec((1, SC_GATHER_WINDOW), lambda i: (0, i))],
            out_specs=[],
            core_axis_name=("core", "subcore"),
            dimension_semantics=(pltpu.PARALLEL,),
        )(table_hbm, idx_hbm)

    return scatter(table.reshape(halves * t, width), idx.reshape(1, n)).reshape(halves, n_out, width)


def _moe_sparse(x, g, router, w13, w2, g_final, *, tm, tmg, tf):
    t, d = x.shape
    hp, mi, mw, cnt = _router(x, g, router, tm=tm)
    counts = cnt[0, :N_EXPERTS].astype(jnp.int32)
    tiles_per_expert = (counts + tmg - 1) // tmg
    tile_end = jnp.cumsum(tiles_per_expert)
    group_start = (tile_end - tiles_per_expert) * tmg
    e1, e2, r1, r2 = mi[:, 0], mi[:, 1], mi[:, 2], mi[:, 3]
    pos1 = group_start[e1] + r1
    pos2 = group_start[e2] + r2
    n_tiles = 2 * t // tmg + N_EXPERTS
    tile = jnp.arange(n_tiles, dtype=jnp.int32)
    tile_expert = jnp.minimum(jnp.sum(tile[:, None] >= tile_end[None, :], axis=1), N_EXPERTS - 1).astype(jnp.int32)
    rows_left = counts[tile_expert] - (tile - (tile_end - tiles_per_expert)[tile_expert]) * tmg
    tile_valid = jnp.where(tile < tile_end[-1], jnp.clip(rows_left, 0, tmg), 0).astype(jnp.int32)

    xs = _dispatch_rows(hp, [pos1, pos2], n_tiles * tmg)
    ys = _grouped_swiglu(tile_expert, tile_valid, xs, w13, w2, tmg=tmg, tf=tf)
    y12 = _gather_rows(ys, jnp.concatenate([pos1, pos2]))
    return _combine(x, mw, y12, g_final, tm=tm)


def _cast_kernel(w_ref, o_ref):
    o_ref[...] = w_ref[...].astype(o_ref.dtype)


def _cast_bf16(w, *, rows):
    shape = w.shape
    w2 = w.reshape(-1, shape[-1])
    r, c = w2.shape
    out = pl.pallas_call(
        _cast_kernel,
        grid=(r // rows,),
        in_specs=[pl.BlockSpec((rows, c), lambda i: (i, 0))],
        out_specs=pl.BlockSpec((rows, c), lambda i: (i, 0)),
        out_shape=jax.ShapeDtypeStruct((r, c), BF16),
        compiler_params=_params("parallel"),
        name="cast_bf16",
    )(w2)
    return out.reshape(shape)


def _split_w_in(w_in):
    sizes = (512, 512, 1024, GLA_RANK, 1024, 512, 512, 1024, M_HEADS, M_HEADS, 1024, 1024, 1024)
    parts, off = [], 0
    for n in sizes:
        parts.append(w_in[:, off:off + n])
        off += n
    g_q, g_k, g_v, g_lr, g_g, m_q, m_k, m_v, m_i, m_f, m_o, a_g, a_m = parts
    wz = jnp.concatenate([g_q, g_k, g_v, g_g, m_q, m_k, m_v, m_o, a_g, a_m], axis=1).astype(BF16)
    d = w_in.shape[0]
    gate_cols = jnp.stack([m_i, m_f], axis=2).reshape(d, 2 * M_HEADS)
    ws = jnp.concatenate([jnp.pad(g_lr, ((0, 0), (0, LANES - GLA_RANK))),
                          jnp.pad(gate_cols, ((0, 0), (0, LANES - 2 * M_HEADS)))], axis=1).astype(BF16)
    assert wz.shape == (d, Z_COLS) and ws.shape == (d, ZS_COLS)
    return wz, ws


def _pick(total, want):
    t = min(total, want)
    while total % t:
        t -= 1
    return t


def kernel(x, mem, norm_mix, w_in, gla_gk_up, gla_gk_bias, gla_norm, m_conv, m_gate_bias, m_norm, gla_proj, m_proj, w_out, norm_xattn, norm_mem, x_wq, x_wkv, x_wo, norm_ffn, ffn_w13, ffn_w2, moe_router, moe_w13, moe_w2, norm_final):
    b, s, d = x.shape
    m_len = mem.shape[1]
    t = b * s
    depth = norm_mix.shape[0]
    tm = _pick(t, 512)
    row2 = lambda a: a.reshape(1, -1)

    xt = x.reshape(t, d)
    mem_t = mem.reshape(b * m_len, d)
    for l in range(depth):
        last_layer = l == depth - 1
        wz, ws = _split_w_in(w_in[l])
        z, zs = _norm_matmul(xt, row2(norm_mix[l]), wz, ws, tm=_pick(t, 1024), tn=1024)
        z3 = z.reshape(b, s, Z_COLS)
        zs3 = zs.reshape(b, s, ZS_COLS)
        up = jnp.pad(gla_gk_up[l], ((0, LANES - GLA_RANK), (0, 0))).astype(BF16)
        o_gla = _gla(z3, zs3, up, row2(gla_gk_bias[l]), row2(gla_norm[l]))
        gbias = jnp.pad(jnp.stack([m_gate_bias[l, :M_HEADS], m_gate_bias[l, M_HEADS:]], axis=1).reshape(1, -1),
                        ((0, 0), (0, LANES - 2 * M_HEADS)))
        arow, rep = _mlstm_gates(zs3, gbias)
        qk3 = _mlstm_conv(z3, m_conv[l], ts=_pick(s, 512))
        o_m = _mlstm(qk3, z3, arow, rep, row2(m_norm[l]))
        xt = _merge(xt, o_gla.reshape(t, d), o_m.reshape(t, d), z,
                    gla_proj[l].astype(BF16), m_proj[l].astype(BF16), w_out[l].astype(BF16), tm=tm)
        kv = _norm_matmul(mem_t, row2(norm_mem[l]), x_wkv[l].astype(BF16), tm=_pick(b * m_len, 512), tn=1024)
        xt = _xattn(xt.reshape(b, s, d), row2(norm_xattn[l]), x_wq[l].astype(BF16),
                    kv.reshape(b, m_len, 2 * d), x_wo[l].astype(BF16), tq=_pick(s, 1024)).reshape(t, d)
        g_final = row2(norm_final) if last_layer else None
        if l % 2 == 0:
            w13, w2 = ffn_w13[l // 2], ffn_w2[l // 2]
            f = w2.shape[0]
            fp = -(-f // (2 * LANES)) * (2 * LANES)
            w1 = jnp.pad(w13[:, :f], ((0, 0), (0, fp - f))).astype(BF16)
            w3 = jnp.pad(w13[:, f:], ((0, 0), (0, fp - f))).astype(BF16)
            w2p = jnp.pad(w2, ((0, fp - f), (0, 0))).astype(BF16)
            xt = _ffn(xt, row2(norm_ffn[l]), w1, w3, w2p, g_final, tm=tm, tf=fp // 2)
        else:
            router = jnp.pad(moe_router[l // 2], ((0, 0), (0, LANES - N_EXPERTS)))
            w13 = _cast_bf16(moe_w13[l // 2], rows=256)
            w2 = _cast_bf16(moe_w2[l // 2], rows=512)
            xt = _moe_sparse(xt, row2(norm_ffn[l]), router, w13, w2, g_final, tm=tm, tmg=tm, tf=w2.shape[1] // 2)
    return xt.reshape(b, s, d)
```

```python
import functools

import jax
import jax.numpy as jnp
from jax import lax
from jax.experimental import pallas as pl
from jax.experimental.pallas import tpu as pltpu
from jax.experimental.pallas import tpu_sc as plsc

F32 = jnp.float32
BF16 = jnp.bfloat16

EPS = 1e-6
D_MODEL = 1024
GLA_HEADS = 4
GLA_DK = 128
GLA_DV = 256
GLA_RANK = 16
GLA_GATE_NORM = 16.0
GLA_LOG_DECAY_MIN = -1.0
GLA_CHUNK = 64
M_HEADS = 4
M_DK = 128
M_DV = 256
M_CHUNK = 128
CONV_W = 4
X_HEADS = 4
N_EXPERTS = 8
LANES = 128
MXU_TILE = 256
SUBLANES = 8
VMEM_LIMIT = 48 * 1024 * 1024
SC_GATHER_WINDOW = 128
PACK_HALVES = 2
PACK_W = D_MODEL // 2 // PACK_HALVES

Z_GQ, Z_GK, Z_GV, Z_GG = 0, 512, 1024, 2048
Z_MQ, Z_MK, Z_MV, Z_MO = 3072, 3584, 4096, 5120
Z_AG, Z_AM = 6144, 7168
Z_COLS = 8192
ZS_COLS = 2 * LANES


def _params(*sem):
    return pltpu.CompilerParams(dimension_semantics=sem, vmem_limit_bytes=VMEM_LIMIT)


def _rms(x, g):
    return x * lax.rsqrt(jnp.mean(x * x, axis=-1, keepdims=True) + EPS) * g


def _log_sigmoid(u):
    return jnp.minimum(u, 0.0) - jnp.log(1.0 + jnp.exp(-jnp.abs(u)))


def _sigmoid(u):
    return 1.0 / (1.0 + jnp.exp(-u))


def _dot(a, b):
    return jnp.dot(a, b, preferred_element_type=F32)


def _dot_nt(a, b):
    return lax.dot_general(a, b, (((1,), (1,)), ((), ())), preferred_element_type=F32)


def _dot_tn(a, b):
    return lax.dot_general(a, b, (((0,), (0,)), ((), ())), preferred_element_type=F32)


def _cumsum_rows(x):
    n = x.shape[0]
    row = lax.broadcasted_iota(jnp.int32, x.shape, 0)
    s = 1
    while s < n:
        x = x + jnp.where(row >= s, pltpu.roll(x, s, 0), 0.0)
        s *= 2
    return x


def _norm_matmul_kernel(*refs, with_small):
    if with_small:
        x_ref, g_ref, w_ref, ws_ref, z_ref, zs_ref, h_scr = refs
    else:
        x_ref, g_ref, w_ref, z_ref, h_scr = refs

    @pl.when(pl.program_id(1) == 0)
    def _():
        h = _rms(x_ref[...], g_ref[...]).astype(BF16)
        h_scr[...] = h
        if with_small:
            zs_ref[...] = _dot(h, ws_ref[...])

    z_ref[...] = _dot(h_scr[...], w_ref[...]).astype(z_ref.dtype)


def _norm_matmul(x, g, w, ws=None, *, tm, tn):
    t, d = x.shape
    n = w.shape[1]
    in_specs = [pl.BlockSpec((tm, d), lambda i, j: (i, 0)),
                pl.BlockSpec((1, d), lambda i, j: (0, 0)),
                pl.BlockSpec((d, tn), lambda i, j: (0, j))]
    out_specs = [pl.BlockSpec((tm, tn), lambda i, j: (i, j))]
    out_shape = [jax.ShapeDtypeStruct((t, n), BF16)]
    args = [x, g, w]
    if ws is not None:
        ns = ws.shape[1]
        in_specs.append(pl.BlockSpec((d, ns), lambda i, j: (0, 0)))
        out_specs.append(pl.BlockSpec((tm, ns), lambda i, j: (i, 0)))
        out_shape.append(jax.ShapeDtypeStruct((t, ns), F32))
        args.append(ws)
    out = pl.pallas_call(
        functools.partial(_norm_matmul_kernel, with_small=ws is not None),
        grid=(t // tm, n // tn),
        in_specs=in_specs, out_specs=out_specs, out_shape=out_shape,
        scratch_shapes=[pltpu.VMEM((tm, d), BF16)],
        compiler_params=_params("parallel", "arbitrary"),
        name="norm_matmul",
    )(*args)
    return out if ws is not None else out[0]


def _gla_kernel(q_ref, k_ref, v_ref, gg_ref, lr_ref, up_ref, gb_ref, nw_ref, o_ref, la_scr, st_scr):
    seq = q_ref.shape[0]
    c = GLA_CHUNK
    scale = GLA_DK ** -0.5

    u = _dot(lr_ref[...].astype(BF16), up_ref[...]) + gb_ref[...]
    la_scr[...] = jnp.maximum(_log_sigmoid(u) * (1.0 / GLA_GATE_NORM), GLA_LOG_DECAY_MIN)
    st_scr[...] = jnp.zeros_like(st_scr)
    causal = (lax.broadcasted_iota(jnp.int32, (c, c), 0) >= lax.broadcasted_iota(jnp.int32, (c, c), 1))

    def body(n, carry):
        r0 = pl.multiple_of(n * c, c)
        rows = pl.ds(r0, c)
        cum = _cumsum_rows(la_scr[rows, :])
        cum_last = cum[c - 1:c, :]
        q = q_ref[rows, :].astype(F32) * scale
        k = k_ref[rows, :].astype(F32)
        v = v_ref[rows, :]
        q_dec = (q * jnp.exp(cum)).astype(BF16)
        k_inv = (k * jnp.exp(-cum)).astype(BF16)
        k_end = (k * jnp.exp(cum_last - cum)).astype(BF16)
        scores = jnp.where(causal, _dot_nt(q_dec, k_inv), 0.0).astype(BF16)
        st = st_scr[...]
        o = _dot(scores, v) + _dot_nt(q_dec, st.astype(BF16))
        st_scr[...] = st * jnp.exp(cum_last) + _dot_tn(v, k_end)
        gate = gg_ref[rows, :].astype(F32)
        o_ref[rows, :] = (_rms(o, nw_ref[...]) * (gate * _sigmoid(gate))).astype(o_ref.dtype)
        return carry

    lax.fori_loop(0, seq // c, body, 0, unroll=8)


def _gla(z3, zs3, up, gb, nw):
    b, s, _ = z3.shape
    qb, kb, vb, gb_ = Z_GQ // GLA_DK, Z_GK // GLA_DK, Z_GV // GLA_DV, Z_GG // GLA_DV
    return pl.pallas_call(
        _gla_kernel,
        grid=(b, GLA_HEADS),
        in_specs=[pl.BlockSpec((None, s, GLA_DK), lambda i, h: (i, 0, qb + h)),
                  pl.BlockSpec((None, s, GLA_DK), lambda i, h: (i, 0, kb + h)),
                  pl.BlockSpec((None, s, GLA_DV), lambda i, h: (i, 0, vb + h)),
                  pl.BlockSpec((None, s, GLA_DV), lambda i, h: (i, 0, gb_ + h)),
                  pl.BlockSpec((None, s, LANES), lambda i, h: (i, 0, 0)),
                  pl.BlockSpec((LANES, GLA_DK), lambda i, h: (0, h)),
                  pl.BlockSpec((1, GLA_DK), lambda i, h: (0, h)),
                  pl.BlockSpec((1, GLA_DV), lambda i, h: (0, h))],
        out_specs=pl.BlockSpec((None, s, GLA_DV), lambda i, h: (i, 0, h)),
        out_shape=jax.ShapeDtypeStruct((b, s, GLA_HEADS * GLA_DV), BF16),
        scratch_shapes=[pltpu.VMEM((s, GLA_DK), F32), pltpu.VMEM((GLA_DV, GLA_DK), F32)],
        compiler_params=_params("parallel", "arbitrary"),
        name="gla",
    )(z3, z3, z3, z3, zs3, up, gb, nw)


def _chunk_scan_lanes(x, combine, identity, chunk):
    lane_in_chunk = lax.broadcasted_iota(jnp.int32, x.shape, 1) % chunk
    step = 1
    while step < chunk:
        x = combine(x, jnp.where(lane_in_chunk >= step, pltpu.roll(x, step, 1), identity))
        step *= 2
    return x


def _replicate_rows(x, row):
    hi = x.astype(BF16)
    r1 = x - hi.astype(F32)
    mid = r1.astype(BF16)
    lo = (r1 - mid.astype(F32)).astype(BF16)
    parts = jnp.concatenate([hi, mid, lo, jnp.zeros_like(hi)], axis=0)
    pick = lax.broadcasted_iota(jnp.int32, (4 * SUBLANES, LANES), 0) % SUBLANES == row
    return _dot_tn(parts, jnp.where(pick, 1.0, 0.0).astype(BF16))


def _mlstm_gates_kernel(gt_ref, gb_ref, arow_ref, rep_ref):
    c = M_CHUNK
    gates_t = (gt_ref[...] + gb_ref[...]).T
    top = gates_t[0:SUBLANES, :]
    odd = lax.broadcasted_iota(jnp.int32, top.shape, 0) % 2 == 1
    lf = jnp.where(odd, _log_sigmoid(top), 0.0)
    b_odd = _chunk_scan_lanes(lf, jnp.add, 0.0, c)
    a_even = top - pltpu.roll(b_odd, SUBLANES - 1, 0)
    amax_even = _chunk_scan_lanes(jnp.where(odd, -jnp.inf, a_even), jnp.maximum, -jnp.inf, c)
    for h in range(M_HEADS):
        arow_ref[h] = pltpu.roll(a_even, (SUBLANES - 2 * h) % SUBLANES, 0)
        rep_ref[h, 0] = _replicate_rows(b_odd, 2 * h + 1)
        rep_ref[h, 1] = _replicate_rows(a_even, 2 * h)
        rep_ref[h, 2] = _replicate_rows(jnp.where(odd, 0.0, amax_even), 2 * h)


def _mlstm_gates(zs3, gbias):
    b, s, _ = zs3.shape
    return pl.pallas_call(
        _mlstm_gates_kernel,
        grid=(b,),
        in_specs=[pl.BlockSpec((None, s, LANES), lambda i: (i, 0, 1)),
                  pl.BlockSpec((1, LANES), lambda i: (0, 0))],
        out_specs=[pl.BlockSpec((None, M_HEADS, SUBLANES, s), lambda i: (i, 0, 0, 0)),
                   pl.BlockSpec((None, M_HEADS, 3, s, LANES), lambda i: (i, 0, 0, 0, 0))],
        out_shape=[jax.ShapeDtypeStruct((b, M_HEADS, SUBLANES, s), F32),
                   jax.ShapeDtypeStruct((b, M_HEADS, 3, s, LANES), F32)],
        compiler_params=_params("parallel"),
        name="mlstm_gates",
    )(zs3, gbias)


def _mlstm_conv_kernel(cur_ref, prev_ref, w_ref, shift_ref, o_ref):
    c = M_CHUNK
    d = cur_ref.shape[1]
    prev = jnp.where(pl.program_id(1) > 0, prev_ref[...], jnp.zeros_like(prev_ref))
    lane = lax.broadcasted_iota(jnp.int32, (1, d), 1)
    col_scale = jnp.where(lane < d // 2, M_DK ** -0.5, 1.0)
    for i in range(cur_ref.shape[0] // c):
        cur = cur_ref[i * c:(i + 1) * c, :]
        before = prev if i == 0 else cur_ref[(i - 1) * c:i * c, :]
        shifted = _dot(shift_ref[...], jnp.concatenate([before, cur], axis=0))
        acc = w_ref[CONV_W - 1:CONV_W, :] * cur.astype(F32)
        for s in range(1, CONV_W):
            acc = acc + w_ref[CONV_W - 1 - s:CONV_W - s, :] * shifted[(s - 1) * c:s * c, :]
        o_ref[i * c:(i + 1) * c, :] = (acc * _sigmoid(acc) * col_scale).astype(o_ref.dtype)


def _mlstm_conv(z3, conv, *, ts):
    b, s, _ = z3.shape
    c = M_CHUNK
    d = conv.shape[1]
    blk = Z_MQ // d
    assert Z_MQ % d == 0 and Z_MK == Z_MQ + d // 2
    t_idx = jnp.arange(c)[:, None]
    col = jnp.arange(2 * c)[None, :]
    shift = jnp.concatenate([(col == c + t_idx - s) for s in range(1, CONV_W)], axis=0).astype(BF16)
    return pl.pallas_call(
        _mlstm_conv_kernel,
        grid=(b, s // ts),
        in_specs=[pl.BlockSpec((None, ts, d), lambda i, j: (i, j, blk)),
                  pl.BlockSpec((None, c, d), lambda i, j: (i, jnp.maximum(j * (ts // c) - 1, 0), blk)),
                  pl.BlockSpec((CONV_W, d), lambda i, j: (0, 0)),
                  pl.BlockSpec(((CONV_W - 1) * c, 2 * c), lambda i, j: (0, 0))],
        out_specs=pl.BlockSpec((None, ts, d), lambda i, j: (i, j, 0)),
        out_shape=jax.ShapeDtypeStruct((b, s, d), BF16),
        compiler_params=_params("parallel", "parallel"),
        name="mlstm_conv",
    )(z3, z3, conv, shift)


def _mlstm_kernel(q_ref, k_ref, v_ref, og_ref, arow_ref, rep_ref, nw_ref, o_ref, st_scr, m_scr):
    seq = q_ref.shape[0]
    c = M_CHUNK
    st_scr[...] = jnp.zeros_like(st_scr)
    m_scr[...] = jnp.zeros_like(m_scr)
    causal = (lax.broadcasted_iota(jnp.int32, (c, c), 0) >= lax.broadcasted_iota(jnp.int32, (c, c), 1))
    ones_cols = jnp.ones((c, LANES), BF16)
    ones_dv = jnp.ones((M_DV, LANES), BF16)

    def wide(x, n):
        return jnp.concatenate([x] * n, axis=1)

    def body(n, carry):
        r0 = pl.multiple_of(n * c, c)
        rows = pl.ds(r0, c)
        qb = q_ref[rows, :]
        kb = k_ref[rows, :]
        v_ext = jnp.concatenate([v_ref[rows, :], ones_cols], axis=1)
        b = rep_ref[0, rows, :]
        a = rep_ref[1, rows, :]
        amax = rep_ref[2, rows, :]
        a_row = arow_ref[0:1, rows]
        m_prev = m_scr[...]
        st = st_scr[...]

        dmat = jnp.where(causal, b + a_row, -jnp.inf)
        m_inter = b + m_prev
        m_t = jnp.maximum(m_inter, b + amax)
        w_ts = jnp.exp(dmat - m_t) * _dot_nt(qb, kb)
        s_inter = jnp.exp(m_inter - m_t)
        nd = _dot(w_ts.astype(BF16), v_ext) + wide(s_inter, 3) * _dot(qb, st.astype(BF16))
        inv = 1.0 / jnp.maximum(jnp.abs(nd[:, M_DV:]), jnp.exp(-m_t))
        h = nd[:, :M_DV] * wide(inv, 2)

        b_last = b[c - 1:c, :]
        m_new = jnp.maximum(b_last + m_prev, b_last + amax[c - 1:c, :])
        kw = kb.astype(F32) * jnp.exp(b_last + a - m_new)
        st_scr[...] = wide(jnp.exp(b_last + m_prev - m_new), 3) * st + _dot_tn(kw.astype(BF16), v_ext)
        m_scr[...] = m_new

        mean_sq = _dot((h * h).astype(BF16), ones_dv) * (1.0 / M_DV)
        gate = og_ref[rows, :].astype(F32)
        y = h * wide(lax.rsqrt(mean_sq + EPS), 2) * nw_ref[...]
        o_ref[rows, :] = (y * _sigmoid(gate)).astype(o_ref.dtype)
        return carry

    lax.fori_loop(0, seq // c, body, 0, unroll=8)


def _mlstm(qk3, z3, arow, rep, nw):
    b, s, _ = z3.shape
    vb, ob = Z_MV // M_DV, Z_MO // M_DV
    return pl.pallas_call(
        _mlstm_kernel,
        grid=(b, M_HEADS),
        in_specs=[pl.BlockSpec((None, s, M_DK), lambda i, h: (i, 0, h)),
                  pl.BlockSpec((None, s, M_DK), lambda i, h: (i, 0, M_HEADS + h)),
                  pl.BlockSpec((None, s, M_DV), lambda i, h: (i, 0, vb + h)),
                  pl.BlockSpec((None, s, M_DV), lambda i, h: (i, 0, ob + h)),
                  pl.BlockSpec((None, None, SUBLANES, s), lambda i, h: (i, h, 0, 0)),
                  pl.BlockSpec((None, None, 3, s, LANES), lambda i, h: (i, h, 0, 0, 0)),
                  pl.BlockSpec((1, M_DV), lambda i, h: (0, h))],
        out_specs=pl.BlockSpec((None, s, M_DV), lambda i, h: (i, 0, h)),
        out_shape=jax.ShapeDtypeStruct((b, s, M_HEADS * M_DV), BF16),
        scratch_shapes=[pltpu.VMEM((M_DK, M_DV + LANES), F32), pltpu.VMEM((1, LANES), F32)],
        compiler_params=_params("parallel", "arbitrary"),
        name="mlstm",
    )(qk3, qk3, z3, z3, arow, rep, nw)


def _merge_kernel(x_ref, og_ref, om_ref, ag_ref, am_ref, wg_ref, wm_ref, wo_ref, o_ref):
    merged = (_sigmoid(ag_ref[...].astype(F32)) * _dot(og_ref[...], wg_ref[...])
              + _sigmoid(am_ref[...].astype(F32)) * _dot(om_ref[...], wm_ref[...]))
    o_ref[...] = x_ref[...] + _dot(merged.astype(BF16), wo_ref[...])


def _merge(x, o_gla, o_m, z, wg, wm, wo, *, tm):
    t, d = x.shape
    row = lambda i: (i, 0)
    full = lambda i: (0, 0)
    return pl.pallas_call(
        _merge_kernel,
        grid=(t // tm,),
        in_specs=[pl.BlockSpec((tm, d), row), pl.BlockSpec((tm, d), row), pl.BlockSpec((tm, d), row),
                  pl.BlockSpec((tm, d), lambda i: (i, Z_AG // D_MODEL)),
                  pl.BlockSpec((tm, d), lambda i: (i, Z_AM // D_MODEL)),
                  pl.BlockSpec((d, d), full), pl.BlockSpec((d, d), full), pl.BlockSpec((d, d), full)],
        out_specs=pl.BlockSpec((tm, d), row),
        out_shape=jax.ShapeDtypeStruct((t, d), F32),
        compiler_params=_params("parallel"),
        name="merge",
    )(x, o_gla, o_m, z, z, wg, wm, wo)


def _xattn_kernel(x_ref, g_ref, wq_ref, kv_ref, wo_ref, o_ref):
    d = x_ref.shape[1]
    dh = d // X_HEADS
    x = x_ref[...]
    q = _dot(_rms(x, g_ref[...]).astype(BF16), wq_ref[...]).astype(BF16)
    heads = []
    for h in range(X_HEADS):
        k = kv_ref[:, h * dh:(h + 1) * dh]
        v = kv_ref[:, d + h * dh:d + (h + 1) * dh]
        s = _dot_nt(q[:, h * dh:(h + 1) * dh], k) * dh ** -0.5
        p = jnp.exp(s - jnp.max(s, axis=-1, keepdims=True))
        p = p / jnp.sum(p, axis=-1, keepdims=True)
        heads.append(_dot(p.astype(BF16), v).astype(BF16))
    o_ref[...] = x + _dot(jnp.concatenate(heads, axis=-1), wo_ref[...])


def _xattn(x3, g, wq, kv, wo, *, tq):
    b, s, d = x3.shape
    m = kv.shape[1]
    return pl.pallas_call(
        _xattn_kernel,
        grid=(b, s // tq),
        in_specs=[pl.BlockSpec((None, tq, d), lambda i, j: (i, j, 0)),
                  pl.BlockSpec((1, d), lambda i, j: (0, 0)),
                  pl.BlockSpec((d, d), lambda i, j: (0, 0)),
                  pl.BlockSpec((None, m, 2 * d), lambda i, j: (i, 0, 0)),
                  pl.BlockSpec((d, d), lambda i, j: (0, 0))],
        out_specs=pl.BlockSpec((None, tq, d), lambda i, j: (i, j, 0)),
        out_shape=jax.ShapeDtypeStruct((b, s, d), F32),
        compiler_params=_params("parallel", "parallel"),
        name="xattn",
    )(x3, g, wq, kv, wo)


def _ffn_kernel(*refs, chunks, final_norm):
    if final_norm:
        x_ref, g_ref, w1_ref, w3_ref, w2_ref, gf_ref, o_ref = refs
    else:
        x_ref, g_ref, w1_ref, w3_ref, w2_ref, o_ref = refs
    x = x_ref[...]
    h = _rms(x, g_ref[...]).astype(BF16)
    out = x
    for c0, c1 in chunks:
        a = _dot(h, w1_ref[:, c0:c1])
        act = (a * _sigmoid(a) * _dot(h, w3_ref[:, c0:c1])).astype(BF16)
        out = out + _dot(act, w2_ref[c0:c1, :])
    if final_norm:
        out = _rms(out, gf_ref[...])
    o_ref[...] = out


def _ffn(x, g, w1, w3, w2, g_final, *, tm):
    t, d = x.shape
    f = w2.shape[0]
    assert f % MXU_TILE == 0
    half = (f // MXU_TILE + 1) // 2 * MXU_TILE
    chunks = ((0, half), (half, f))
    resident = pl.Buffered(1)
    in_specs = [pl.BlockSpec((tm, d), lambda i: (i, 0)),
                pl.BlockSpec((1, d), lambda i: (0, 0)),
                pl.BlockSpec((d, f), lambda i: (0, 0), pipeline_mode=resident),
                pl.BlockSpec((d, f), lambda i: (0, 0), pipeline_mode=resident),
                pl.BlockSpec((f, d), lambda i: (0, 0), pipeline_mode=resident)]
    args = [x, g, w1, w3, w2]
    if g_final is not None:
        in_specs.append(pl.BlockSpec((1, d), lambda i: (0, 0)))
        args.append(g_final)
    return pl.pallas_call(
        functools.partial(_ffn_kernel, chunks=chunks, final_norm=g_final is not None),
        grid=(t // tm,),
        in_specs=in_specs,
        out_specs=pl.BlockSpec((tm, d), lambda i: (i, 0)),
        out_shape=jax.ShapeDtypeStruct((t, d), F32),
        compiler_params=_params("parallel"),
        name="dense_ffn",
    )(*args)


def _store_packed(ref, y):
    half = y.shape[1] // 2
    bits = lax.bitcast_convert_type(y.astype(BF16).astype(F32), jnp.int32)
    lo = lax.shift_right_logical(bits[:, :half], 16)
    hi = jnp.bitwise_and(bits[:, half:], jnp.int32(-65536))
    packed = jnp.bitwise_or(hi, lo)
    w = half // PACK_HALVES
    for p in range(PACK_HALVES):
        ref[p] = packed[:, p * w:(p + 1) * w]


def _load_packed(ref):
    packed = jnp.concatenate([ref[p] for p in range(PACK_HALVES)], axis=1)
    lo = lax.bitcast_convert_type(lax.shift_left(packed, 16), F32)
    hi = lax.bitcast_convert_type(jnp.bitwise_and(packed, jnp.int32(-65536)), F32)
    return jnp.concatenate([lo, hi], axis=1)


def _router_kernel(x_ref, g_ref, r_ref, hp_ref, mi_ref, mw_ref, cnt_ref, carry_scr):
    @pl.when(pl.program_id(0) == 0)
    def _():
        carry_scr[...] = jnp.zeros_like(carry_scr)

    rows = x_ref.shape[0]
    h = _rms(x_ref[...], g_ref[...])
    _store_packed(hp_ref, h)
    r = r_ref[...]
    h_hi = h.astype(BF16)
    h_lo = (h - h_hi.astype(F32)).astype(BF16)
    r_hi = r.astype(BF16)
    r_lo = (r - r_hi.astype(F32)).astype(BF16)
    logits = _dot(h_hi, r_hi) + (_dot(h_hi, r_lo) + _dot(h_lo, r_hi))
    lane = lax.broadcasted_iota(jnp.int32, logits.shape, 1)
    logits = jnp.where(lane < N_EXPERTS, logits, -jnp.inf)
    m1 = jnp.max(logits, axis=-1, keepdims=True)
    i1 = jnp.min(jnp.where(logits == m1, lane, LANES), axis=-1, keepdims=True)
    rest = jnp.where(lane == i1, -jnp.inf, logits)
    m2 = jnp.max(rest, axis=-1, keepdims=True)
    i2 = jnp.min(jnp.where(rest == m2, lane, LANES), axis=-1, keepdims=True)
    e2 = jnp.exp(m2 - m1)
    w1 = 1.0 / (1.0 + e2)
    w2 = e2 / (1.0 + e2)
    oh1 = (lane == i1).astype(F32)
    oh2 = (lane == i2).astype(F32)
    oh = oh1 + oh2
    earlier = (lax.broadcasted_iota(jnp.int32, (rows, rows), 0) > lax.broadcasted_iota(jnp.int32, (rows, rows), 1))
    before = _dot(earlier.astype(BF16), oh.astype(BF16)) + carry_scr[...]
    rank1 = jnp.sum(oh1 * before, axis=-1, keepdims=True).astype(jnp.int32)
    rank2 = jnp.sum(oh2 * before, axis=-1, keepdims=True).astype(jnp.int32)
    carry_scr[...] += jnp.sum(oh, axis=0, keepdims=True)
    cnt_ref[...] = carry_scr[...]
    mi_ref[...] = jnp.where(lane == 0, i1, jnp.where(lane == 1, i2, jnp.where(lane == 2, rank1,
                            jnp.where(lane == 3, rank2, 0))))
    mw_ref[...] = jnp.where(lane == 0, w1, jnp.where(lane == 1, w2, 0.0))


def _router(x, g, router, *, tm):
    t, d = x.shape
    row = lambda i: (i, 0)
    fix = lambda i: (0, 0)
    return pl.pallas_call(
        _router_kernel,
        grid=(t // tm,),
        in_specs=[pl.BlockSpec((tm, d), row), pl.BlockSpec((1, d), fix), pl.BlockSpec((d, LANES), fix)],
        out_specs=[pl.BlockSpec((PACK_HALVES, tm, PACK_W), lambda i: (0, i, 0)), pl.BlockSpec((tm, LANES), row),
                   pl.BlockSpec((tm, LANES), row), pl.BlockSpec((1, LANES), fix)],
        out_shape=[jax.ShapeDtypeStruct((PACK_HALVES, t, PACK_W), jnp.int32),
                   jax.ShapeDtypeStruct((t, LANES), jnp.int32),
                   jax.ShapeDtypeStruct((t, LANES), F32), jax.ShapeDtypeStruct((1, LANES), F32)],
        scratch_shapes=[pltpu.VMEM((1, LANES), F32)],
        compiler_params=_params("arbitrary"),
        name="router",
    )(x, g, router)


def _grouped_kernel(te_ref, tv_ref, xs_ref, w1_ref, w3_ref, w2_ref, ys_ref, h_scr, acc_scr):
    i = pl.program_id(0)
    j = pl.program_id(1)
    last = pl.num_programs(1) - 1
    valid = tv_ref[i]
    active = valid > 0

    @pl.when(jnp.logical_and(active, j == 0))
    def _():
        h = _load_packed(xs_ref)
        row = lax.broadcasted_iota(jnp.int32, h.shape, 0)
        h_scr[...] = jnp.where(row < valid, h, 0.0).astype(BF16)
        acc_scr[...] = jnp.zeros_like(acc_scr)

    @pl.when(active)
    def _():
        h = h_scr[...]
        a = _dot(h, w1_ref[...])
        act = (a * _sigmoid(a) * _dot(h, w3_ref[...])).astype(BF16)
        acc_scr[...] += _dot(act, w2_ref[...])

    @pl.when(jnp.logical_and(active, j == last))
    def _():
        _store_packed(ys_ref, acc_scr[...])

    @pl.when(jnp.logical_and(jnp.logical_not(active), j == last))
    def _():
        ys_ref[...] = jnp.zeros_like(ys_ref)


def _grouped_swiglu(tile_expert, tile_valid, xs, w13, w2, *, tmg, tf):
    _, p, _ = xs.shape
    d = w2.shape[2]
    f = w2.shape[1]
    nj = f // tf

    def jj(i, j, tv):
        return jnp.where(tv[i] > 0, j, nj - 1)

    grid_spec = pltpu.PrefetchScalarGridSpec(
        num_scalar_prefetch=2,
        grid=(p // tmg, nj),
        in_specs=[pl.BlockSpec((PACK_HALVES, tmg, PACK_W), lambda i, j, te, tv: (0, i, 0)),
                  pl.BlockSpec((None, d, tf), lambda i, j, te, tv: (te[i], 0, jj(i, j, tv))),
                  pl.BlockSpec((None, d, tf), lambda i, j, te, tv: (te[i], 0, nj + jj(i, j, tv))),
                  pl.BlockSpec((None, tf, d), lambda i, j, te, tv: (te[i], jj(i, j, tv), 0))],
        out_specs=pl.BlockSpec((PACK_HALVES, tmg, PACK_W), lambda i, j, te, tv: (0, i, 0)),
        scratch_shapes=[pltpu.VMEM((tmg, d), BF16), pltpu.VMEM((tmg, d), F32)],
    )
    return pl.pallas_call(
        _grouped_kernel,
        grid_spec=grid_spec,
        out_shape=jax.ShapeDtypeStruct((PACK_HALVES, p, PACK_W), jnp.int32),
        compiler_params=_params("arbitrary", "arbitrary"),
        name="grouped_swiglu",
    )(tile_expert, tile_valid, xs, w13, w13, w2)


def _combine_kernel(*refs, final_norm):
    if final_norm:
        x_ref, mw_ref, y1_ref, y2_ref, gf_ref, o_ref = refs
    else:
        x_ref, mw_ref, y1_ref, y2_ref, o_ref = refs
    mw = mw_ref[...]
    out = x_ref[...] + mw[:, 0:1] * _load_packed(y1_ref) + mw[:, 1:2] * _load_packed(y2_ref)
    if final_norm:
        out = _rms(out, gf_ref[...])
    o_ref[...] = out


def _combine(x, mw, y12, g_final, *, tm):
    t, d = x.shape
    row = lambda i: (i, 0)
    in_specs = [pl.BlockSpec((tm, d), row), pl.BlockSpec((tm, LANES), row),
                pl.BlockSpec((PACK_HALVES, tm, PACK_W), lambda i: (0, i, 0)),
                pl.BlockSpec((PACK_HALVES, tm, PACK_W), lambda i: (0, t // tm + i, 0))]
    args = [x, mw, y12, y12]
    if g_final is not None:
        in_specs.append(pl.BlockSpec((1, d), lambda i: (0, 0)))
        args.append(g_final)
    return pl.pallas_call(
        functools.partial(_combine_kernel, final_norm=g_final is not None),
        grid=(t // tm,),
        in_specs=in_specs,
        out_specs=pl.BlockSpec((tm, d), row),
        out_shape=jax.ShapeDtypeStruct((t, d), F32),
        compiler_params=_params("parallel"),
        name="combine",
    )(*args)


def _gather_rows(table, idx):
    halves, n_rows, width = table.shape
    n = halves * idx.shape[0]
    assert n % SC_GATHER_WINDOW == 0
    flat_idx = jnp.concatenate([idx + h * n_rows for h in range(halves)]).reshape(1, n)
    mesh = plsc.VectorSubcoreMesh(core_axis_name="core", subcore_axis_name="subcore")

    @pl.kernel(out_type=jax.ShapeDtypeStruct((n, width), table.dtype), mesh=mesh)
    def gather(table_hbm, idx_hbm, out_hbm):
        def body(idx_vmem, out_vmem):
            pltpu.sync_copy(table_hbm.at[idx_vmem.at[0]], out_vmem)

        pltpu.emit_pipeline(
            body,
            grid=(n // SC_GATHER_WINDOW,),
            in_specs=[pl.BlockSpec((1, SC_GATHER_WINDOW), lambda i: (0, i))],
            out_specs=[pl.BlockSpec((SC_GATHER_WINDOW, width), lambda i: (i, 0))],
            core_axis_name=("core", "subcore"),
            dimension_semantics=(pltpu.PARALLEL,),
        )(idx_hbm, out_hbm)

    return gather(table.reshape(halves * n_rows, width), flat_idx).reshape(halves, idx.shape[0], width)


def _dispatch_rows(table, pos_list, n_out):
    halves, t, width = table.shape
    copies = len(pos_list)
    idx = jnp.concatenate([pos + h * n_out for h in range(halves) for pos in pos_list])
    n = idx.shape[0]
    win_t = t // SC_GATHER_WINDOW
    assert t % SC_GATHER_WINDOW == 0
    mesh = plsc.VectorSubcoreMesh(core_axis_name="core", subcore_axis_name="subcore")

    @pl.kernel(out_type=jax.ShapeDtypeStruct((halves * n_out, width), table.dtype), mesh=mesh)
    def scatter(table_hbm, idx_hbm, out_hbm):
        def body(rows_vmem, idx_vmem):
            pltpu.sync_copy(rows_vmem, out_hbm.at[idx_vmem.at[0]])

        pltpu.emit_pipeline(
            body,
            grid=(n // SC_GATHER_WINDOW,),
            in_specs=[pl.BlockSpec((SC_GATHER_WINDOW, width),
                                   lambda i: ((i // (copies * win_t)) * win_t + i % win_t, 0)),
                      pl.BlockSpec((1, SC_GATHER_WINDOW), lambda i: (0, i))],
            out_specs=[],
            core_axis_name=("core", "subcore"),
            dimension_semantics=(pltpu.PARALLEL,),
        )(table_hbm, idx_hbm)

    return scatter(table.reshape(halves * t, width), idx.reshape(1, n)).reshape(halves, n_out, width)


def _moe_sparse(x, g, router, w13, w2, g_final, *, tm, tmg, tf):
    t, d = x.shape
    hp, mi, mw, cnt = _router(x, g, router, tm=tm)
    counts = cnt[0, :N_EXPERTS].astype(jnp.int32)
    tiles_per_expert = (counts + tmg - 1) // tmg
    tile_end = jnp.cumsum(tiles_per_expert)
    group_start = (tile_end - tiles_per_expert) * tmg
    e1, e2, r1, r2 = mi[:, 0], mi[:, 1], mi[:, 2], mi[:, 3]
    pos1 = group_start[e1] + r1
    pos2 = group_start[e2] + r2
    n_tiles = 2 * t // tmg + N_EXPERTS
    tile = jnp.arange(n_tiles, dtype=jnp.int32)
    tile_expert = jnp.minimum(jnp.sum(tile[:, None] >= tile_end[None, :], axis=1), N_EXPERTS - 1).astype(jnp.int32)
    rows_left = counts[tile_expert] - (tile - (tile_end - tiles_per_expert)[tile_expert]) * tmg
    tile_valid = jnp.where(tile < tile_end[-1], jnp.clip(rows_left, 0, tmg), 0).astype(jnp.int32)

    xs = _dispatch_rows(hp, [pos1, pos2], n_tiles * tmg)
    ys = _grouped_swiglu(tile_expert, tile_valid, xs, w13, w2, tmg=tmg, tf=tf)
    y12 = _gather_rows(ys, jnp.concatenate([pos1, pos2]))
    return _combine(x, mw, y12, g_final, tm=tm)


def _cast_kernel(w_ref, o_ref):
    o_ref[...] = w_ref[...].astype(o_ref.dtype)


def _cast_bf16(w, *, rows):
    shape = w.shape
    w2 = w.reshape(-1, shape[-1])
    r, c = w2.shape
    out = pl.pallas_call(
        _cast_kernel,
        grid=(r // rows,),
        in_specs=[pl.BlockSpec((rows, c), lambda i: (i, 0))],
        out_specs=pl.BlockSpec((rows, c), lambda i: (i, 0)),
        out_shape=jax.ShapeDtypeStruct((r, c), BF16),
        compiler_params=_params("parallel"),
        name="cast_bf16",
    )(w2)
    return out.reshape(shape)


def _split_w_in(w_in):
    sizes = (512, 512, 1024, GLA_RANK, 1024, 512, 512, 1024, M_HEADS, M_HEADS, 1024, 1024, 1024)
    parts, off = [], 0
    for n in sizes:
        parts.append(w_in[:, off:off + n])
        off += n
    g_q, g_k, g_v, g_lr, g_g, m_q, m_k, m_v, m_i, m_f, m_o, a_g, a_m = parts
    wz = jnp.concatenate([g_q, g_k, g_v, g_g, m_q, m_k, m_v, m_o, a_g, a_m], axis=1).astype(BF16)
    d = w_in.shape[0]
    gate_cols = jnp.stack([m_i, m_f], axis=2).reshape(d, 2 * M_HEADS)
    ws = jnp.concatenate([jnp.pad(g_lr, ((0, 0), (0, LANES - GLA_RANK))),
                          jnp.pad(gate_cols, ((0, 0), (0, LANES - 2 * M_HEADS)))], axis=1).astype(BF16)
    assert wz.shape == (d, Z_COLS) and ws.shape == (d, ZS_COLS)
    return wz, ws


def _pick(total, want):
    t = min(total, want)
    while total % t:
        t -= 1
    return t


def kernel(x, mem, norm_mix, w_in, gla_gk_up, gla_gk_bias, gla_norm, m_conv, m_gate_bias, m_norm, gla_proj, m_proj, w_out, norm_xattn, norm_mem, x_wq, x_wkv, x_wo, norm_ffn, ffn_w13, ffn_w2, moe_router, moe_w13, moe_w2, norm_final):
    b, s, d = x.shape
    m_len = mem.shape[1]
    t = b * s
    depth = norm_mix.shape[0]
    tm = _pick(t, 512)
    row2 = lambda a: a.reshape(1, -1)

    xt = x.reshape(t, d)
    mem_t = mem.reshape(b * m_len, d)
    for l in range(depth):
        last_layer = l == depth - 1
        wz, ws = _split_w_in(w_in[l])
        z, zs = _norm_matmul(xt, row2(norm_mix[l]), wz, ws, tm=_pick(t, 1024), tn=2048)
        z3 = z.reshape(b, s, Z_COLS)
        zs3 = zs.reshape(b, s, ZS_COLS)
        up = jnp.pad(gla_gk_up[l], ((0, LANES - GLA_RANK), (0, 0))).astype(BF16)
        o_gla = _gla(z3, zs3, up, row2(gla_gk_bias[l]), row2(gla_norm[l]))
        gbias = jnp.pad(jnp.stack([m_gate_bias[l, :M_HEADS], m_gate_bias[l, M_HEADS:]], axis=1).reshape(1, -1),
                        ((0, 0), (0, LANES - 2 * M_HEADS)))
        arow, rep = _mlstm_gates(zs3, gbias)
        qk3 = _mlstm_conv(z3, m_conv[l], ts=_pick(s, 512))
        o_m = _mlstm(qk3, z3, arow, rep, row2(m_norm[l]))
        xt = _merge(xt, o_gla.reshape(t, d), o_m.reshape(t, d), z,
                    gla_proj[l].astype(BF16), m_proj[l].astype(BF16), w_out[l].astype(BF16), tm=tm)
        kv = _norm_matmul(mem_t, row2(norm_mem[l]), x_wkv[l].astype(BF16), tm=_pick(b * m_len, 512), tn=1024)
        xt = _xattn(xt.reshape(b, s, d), row2(norm_xattn[l]), x_wq[l].astype(BF16),
                    kv.reshape(b, m_len, 2 * d), x_wo[l].astype(BF16), tq=_pick(s, 1024)).reshape(t, d)
        g_final = row2(norm_final) if last_layer else None
        if l % 2 == 0:
            w13, w2 = ffn_w13[l // 2], ffn_w2[l // 2]
            f = w2.shape[0]
            fp = -(-f // MXU_TILE) * MXU_TILE
            w1 = jnp.pad(w13[:, :f], ((0, 0), (0, fp - f))).astype(BF16)
            w3 = jnp.pad(w13[:, f:], ((0, 0), (0, fp - f))).astype(BF16)
            w2p = jnp.pad(w2, ((0, fp - f), (0, 0))).astype(BF16)
            xt = _ffn(xt, row2(norm_ffn[l]), w1, w3, w2p, g_final, tm=tm)
        else:
            router = jnp.pad(moe_router[l // 2], ((0, 0), (0, LANES - N_EXPERTS)))
            w13 = _cast_bf16(moe_w13[l // 2], rows=256)
            w2 = _cast_bf16(moe_w2[l // 2], rows=512)
            xt = _moe_sparse(xt, row2(norm_ffn[l]), router, w13, w2, g_final, tm=tm, tmg=tm, tf=w2.shape[1] // 2)
    return xt.reshape(b, s, d)
```

```python
import functools

import jax
import jax.numpy as jnp
from jax import lax
from jax.experimental import pallas as pl
from jax.experimental.pallas import tpu as pltpu
from jax.experimental.pallas import tpu_sc as plsc

F32 = jnp.float32
BF16 = jnp.bfloat16

EPS = 1e-6
D_MODEL = 1024
GLA_HEADS = 4
GLA_DK = 128
GLA_DV = 256
GLA_RANK = 16
GLA_GATE_NORM = 16.0
GLA_LOG_DECAY_MIN = -1.0
GLA_CHUNK = 64
M_HEADS = 4
M_DK = 128
M_DV = 256
M_CHUNK = 128
CONV_W = 4
X_HEADS = 4
N_EXPERTS = 8
LANES = 128
MXU_TILE = 256
SUBLANES = 8
VMEM_LIMIT = 48 * 1024 * 1024
SC_GATHER_WINDOW = 128
PACK_HALVES = 2
PACK_W = D_MODEL // 2 // PACK_HALVES

Z_GQ, Z_GK, Z_GV, Z_GG = 0, 512, 1024, 2048
Z_MQ, Z_MK, Z_MV, Z_MO = 3072, 3584, 4096, 5120
Z_AG, Z_AM = 6144, 7168
Z_COLS = 8192
ZS_COLS = 2 * LANES


def _params(*sem):
    return pltpu.CompilerParams(dimension_semantics=sem, vmem_limit_bytes=VMEM_LIMIT)


def _rms(x, g):
    return x * lax.rsqrt(jnp.mean(x * x, axis=-1, keepdims=True) + EPS) * g


def _log_sigmoid(u):
    return jnp.minimum(u, 0.0) - jnp.log(1.0 + jnp.exp(-jnp.abs(u)))


def _sigmoid(u):
    return 1.0 / (1.0 + jnp.exp(-u))


def _dot(a, b):
    return jnp.dot(a, b, preferred_element_type=F32)


def _dot_nt(a, b):
    return lax.dot_general(a, b, (((1,), (1,)), ((), ())), preferred_element_type=F32)


def _dot_tn(a, b):
    return lax.dot_general(a, b, (((0,), (0,)), ((), ())), preferred_element_type=F32)


def _cumsum_rows(x):
    n = x.shape[0]
    row = lax.broadcasted_iota(jnp.int32, x.shape, 0)
    s = 1
    while s < n:
        x = x + jnp.where(row >= s, pltpu.roll(x, s, 0), 0.0)
        s *= 2
    return x


def _norm_matmul_kernel(*refs, tn, with_small):
    if with_small:
        x_ref, g_ref, w_ref, ws_ref, z_ref, zs_ref = refs
    else:
        x_ref, g_ref, w_ref, z_ref = refs
    h = _rms(x_ref[...], g_ref[...]).astype(BF16)
    if with_small:
        zs_ref[...] = _dot(h, ws_ref[...])
    for c0 in range(0, z_ref.shape[1], tn):
        z_ref[:, c0:c0 + tn] = _dot(h, w_ref[:, c0:c0 + tn]).astype(z_ref.dtype)


def _norm_matmul(x, g, w, ws=None, *, tm, tn):
    t, d = x.shape
    n = w.shape[1]
    resident = pl.Buffered(1)
    in_specs = [pl.BlockSpec((tm, d), lambda i: (i, 0)),
                pl.BlockSpec((1, d), lambda i: (0, 0)),
                pl.BlockSpec((d, n), lambda i: (0, 0), pipeline_mode=resident)]
    out_specs = [pl.BlockSpec((tm, n), lambda i: (i, 0))]
    out_shape = [jax.ShapeDtypeStruct((t, n), BF16)]
    args = [x, g, w]
    if ws is not None:
        ns = ws.shape[1]
        in_specs.append(pl.BlockSpec((d, ns), lambda i: (0, 0), pipeline_mode=resident))
        out_specs.append(pl.BlockSpec((tm, ns), lambda i: (i, 0)))
        out_shape.append(jax.ShapeDtypeStruct((t, ns), F32))
        args.append(ws)
    out = pl.pallas_call(
        functools.partial(_norm_matmul_kernel, tn=tn, with_small=ws is not None),
        grid=(t // tm,),
        in_specs=in_specs, out_specs=out_specs, out_shape=out_shape,
        compiler_params=_params("parallel"),
        name="norm_matmul",
    )(*args)
    return out if ws is not None else out[0]


def _gla_kernel(q_ref, k_ref, v_ref, gg_ref, lr_ref, up_ref, gb_ref, nw_ref, o_ref, la_scr, st_scr):
    seq = q_ref.shape[0]
    c = GLA_CHUNK
    scale = GLA_DK ** -0.5

    u = _dot(lr_ref[...].astype(BF16), up_ref[...]) + gb_ref[...]
    la_scr[...] = jnp.maximum(_log_sigmoid(u) * (1.0 / GLA_GATE_NORM), GLA_LOG_DECAY_MIN)
    st_scr[...] = jnp.zeros_like(st_scr)
    causal = (lax.broadcasted_iota(jnp.int32, (c, c), 0) >= lax.broadcasted_iota(jnp.int32, (c, c), 1))

    def body(n, carry):
        r0 = pl.multiple_of(n * c, c)
        rows = pl.ds(r0, c)
        cum = _cumsum_rows(la_scr[rows, :])
        cum_last = cum[c - 1:c, :]
        q = q_ref[rows, :].astype(F32) * scale
        k = k_ref[rows, :].astype(F32)
        v = v_ref[rows, :]
        q_dec = (q * jnp.exp(cum)).astype(BF16)
        k_inv = (k * jnp.exp(-cum)).astype(BF16)
        k_end = (k * jnp.exp(cum_last - cum)).astype(BF16)
        scores = jnp.where(causal, _dot_nt(q_dec, k_inv), 0.0).astype(BF16)
        st = st_scr[...]
        o = _dot(scores, v) + _dot_nt(q_dec, st.astype(BF16))
        st_scr[...] = st * jnp.exp(cum_last) + _dot_tn(v, k_end)
        gate = gg_ref[rows, :].astype(F32)
        o_ref[rows, :] = (_rms(o, nw_ref[...]) * (gate * _sigmoid(gate))).astype(o_ref.dtype)
        return carry

    lax.fori_loop(0, seq // c, body, 0, unroll=8)


def _gla(z3, zs3, up, gb, nw):
    b, s, _ = z3.shape
    qb, kb, vb, gb_ = Z_GQ // GLA_DK, Z_GK // GLA_DK, Z_GV // GLA_DV, Z_GG // GLA_DV
    return pl.pallas_call(
        _gla_kernel,
        grid=(b, GLA_HEADS),
        in_specs=[pl.BlockSpec((None, s, GLA_DK), lambda i, h: (i, 0, qb + h)),
                  pl.BlockSpec((None, s, GLA_DK), lambda i, h: (i, 0, kb + h)),
                  pl.BlockSpec((None, s, GLA_DV), lambda i, h: (i, 0, vb + h)),
                  pl.BlockSpec((None, s, GLA_DV), lambda i, h: (i, 0, gb_ + h)),
                  pl.BlockSpec((None, s, LANES), lambda i, h: (i, 0, 0)),
                  pl.BlockSpec((LANES, GLA_DK), lambda i, h: (0, h)),
                  pl.BlockSpec((1, GLA_DK), lambda i, h: (0, h)),
                  pl.BlockSpec((1, GLA_DV), lambda i, h: (0, h))],
        out_specs=pl.BlockSpec((None, s, GLA_DV), lambda i, h: (i, 0, h)),
        out_shape=jax.ShapeDtypeStruct((b, s, GLA_HEADS * GLA_DV), BF16),
        scratch_shapes=[pltpu.VMEM((s, GLA_DK), F32), pltpu.VMEM((GLA_DV, GLA_DK), F32)],
        compiler_params=_params("parallel", "arbitrary"),
        name="gla",
    )(z3, z3, z3, z3, zs3, up, gb, nw)


def _chunk_scan_lanes(x, combine, identity, chunk):
    lane_in_chunk = lax.broadcasted_iota(jnp.int32, x.shape, 1) % chunk
    step = 1
    while step < chunk:
        x = combine(x, jnp.where(lane_in_chunk >= step, pltpu.roll(x, step, 1), identity))
        step *= 2
    return x


def _replicate_rows(x, row):
    hi = x.astype(BF16)
    r1 = x - hi.astype(F32)
    mid = r1.astype(BF16)
    lo = (r1 - mid.astype(F32)).astype(BF16)
    parts = jnp.concatenate([hi, mid, lo, jnp.zeros_like(hi)], axis=0)
    pick = lax.broadcasted_iota(jnp.int32, (4 * SUBLANES, LANES), 0) % SUBLANES == row
    return _dot_tn(parts, jnp.where(pick, 1.0, 0.0).astype(BF16))


def _mlstm_gates_kernel(gt_ref, gb_ref, arow_ref, rep_ref):
    c = M_CHUNK
    gates_t = (gt_ref[...] + gb_ref[...]).T
    top = gates_t[0:SUBLANES, :]
    odd = lax.broadcasted_iota(jnp.int32, top.shape, 0) % 2 == 1
    lf = jnp.where(odd, _log_sigmoid(top), 0.0)
    b_odd = _chunk_scan_lanes(lf, jnp.add, 0.0, c)
    a_even = top - pltpu.roll(b_odd, SUBLANES - 1, 0)
    amax_even = _chunk_scan_lanes(jnp.where(odd, -jnp.inf, a_even), jnp.maximum, -jnp.inf, c)
    for h in range(M_HEADS):
        arow_ref[h] = pltpu.roll(a_even, (SUBLANES - 2 * h) % SUBLANES, 0)
        rep_ref[h, 0] = _replicate_rows(b_odd, 2 * h + 1)
        rep_ref[h, 1] = _replicate_rows(a_even, 2 * h)
        rep_ref[h, 2] = _replicate_rows(jnp.where(odd, 0.0, amax_even), 2 * h)


def _mlstm_gates(zs3, gbias):
    b, s, _ = zs3.shape
    return pl.pallas_call(
        _mlstm_gates_kernel,
        grid=(b,),
        in_specs=[pl.BlockSpec((None, s, LANES), lambda i: (i, 0, 1)),
                  pl.BlockSpec((1, LANES), lambda i: (0, 0))],
        out_specs=[pl.BlockSpec((None, M_HEADS, SUBLANES, s), lambda i: (i, 0, 0, 0)),
                   pl.BlockSpec((None, M_HEADS, 3, s, LANES), lambda i: (i, 0, 0, 0, 0))],
        out_shape=[jax.ShapeDtypeStruct((b, M_HEADS, SUBLANES, s), F32),
                   jax.ShapeDtypeStruct((b, M_HEADS, 3, s, LANES), F32)],
        compiler_params=_params("parallel"),
        name="mlstm_gates",
    )(zs3, gbias)


def _mlstm_conv_kernel(cur_ref, prev_ref, w_ref, shift_ref, o_ref):
    c = M_CHUNK
    d = cur_ref.shape[1]
    prev = jnp.where(pl.program_id(1) > 0, prev_ref[...], jnp.zeros_like(prev_ref))
    lane = lax.broadcasted_iota(jnp.int32, (1, d), 1)
    col_scale = jnp.where(lane < d // 2, M_DK ** -0.5, 1.0)
    for i in range(cur_ref.shape[0] // c):
        cur = cur_ref[i * c:(i + 1) * c, :]
        before = prev if i == 0 else cur_ref[(i - 1) * c:i * c, :]
        shifted = _dot(shift_ref[...], jnp.concatenate([before, cur], axis=0))
        acc = w_ref[CONV_W - 1:CONV_W, :] * cur.astype(F32)
        for s in range(1, CONV_W):
            acc = acc + w_ref[CONV_W - 1 - s:CONV_W - s, :] * shifted[(s - 1) * c:s * c, :]
        o_ref[i * c:(i + 1) * c, :] = (acc * _sigmoid(acc) * col_scale).astype(o_ref.dtype)


def _mlstm_conv(z3, conv, *, ts):
    b, s, _ = z3.shape
    c = M_CHUNK
    d = conv.shape[1]
    blk = Z_MQ // d
    assert Z_MQ % d == 0 and Z_MK == Z_MQ + d // 2
    t_idx = jnp.arange(c)[:, None]
    col = jnp.arange(2 * c)[None, :]
    shift = jnp.concatenate([(col == c + t_idx - s) for s in range(1, CONV_W)], axis=0).astype(BF16)
    return pl.pallas_call(
        _mlstm_conv_kernel,
        grid=(b, s // ts),
        in_specs=[pl.BlockSpec((None, ts, d), lambda i, j: (i, j, blk)),
                  pl.BlockSpec((None, c, d), lambda i, j: (i, jnp.maximum(j * (ts // c) - 1, 0), blk)),
                  pl.BlockSpec((CONV_W, d), lambda i, j: (0, 0)),
                  pl.BlockSpec(((CONV_W - 1) * c, 2 * c), lambda i, j: (0, 0))],
        out_specs=pl.BlockSpec((None, ts, d), lambda i, j: (i, j, 0)),
        out_shape=jax.ShapeDtypeStruct((b, s, d), BF16),
        compiler_params=_params("parallel", "parallel"),
        name="mlstm_conv",
    )(z3, z3, conv, shift)


def _mlstm_kernel(q_ref, k_ref, v_ref, og_ref, arow_ref, rep_ref, nw_ref, o_ref, st_scr, m_scr):
    seq = q_ref.shape[0]
    c = M_CHUNK
    st_scr[...] = jnp.zeros_like(st_scr)
    m_scr[...] = jnp.zeros_like(m_scr)
    causal = (lax.broadcasted_iota(jnp.int32, (c, c), 0) >= lax.broadcasted_iota(jnp.int32, (c, c), 1))
    ones_cols = jnp.ones((c, LANES), BF16)
    ones_dv = jnp.ones((M_DV, LANES), BF16)

    def wide(x, n):
        return jnp.concatenate([x] * n, axis=1)

    def body(n, carry):
        r0 = pl.multiple_of(n * c, c)
        rows = pl.ds(r0, c)
        qb = q_ref[rows, :]
        kb = k_ref[rows, :]
        v_ext = jnp.concatenate([v_ref[rows, :], ones_cols], axis=1)
        b = rep_ref[0, rows, :]
        a = rep_ref[1, rows, :]
        amax = rep_ref[2, rows, :]
        a_row = arow_ref[0:1, rows]
        m_prev = m_scr[...]
        st = st_scr[...]

        dmat = jnp.where(causal, b + a_row, -jnp.inf)
        m_inter = b + m_prev
        m_t = jnp.maximum(m_inter, b + amax)
        w_ts = jnp.exp(dmat - m_t) * _dot_nt(qb, kb)
        s_inter = jnp.exp(m_inter - m_t)
        nd = _dot(w_ts.astype(BF16), v_ext) + wide(s_inter, 3) * _dot(qb, st.astype(BF16))
        inv = 1.0 / jnp.maximum(jnp.abs(nd[:, M_DV:]), jnp.exp(-m_t))
        h = nd[:, :M_DV] * wide(inv, 2)

        b_last = b[c - 1:c, :]
        m_new = jnp.maximum(b_last + m_prev, b_last + amax[c - 1:c, :])
        kw = kb.astype(F32) * jnp.exp(b_last + a - m_new)
        st_scr[...] = wide(jnp.exp(b_last + m_prev - m_new), 3) * st + _dot_tn(kw.astype(BF16), v_ext)
        m_scr[...] = m_new

        mean_sq = _dot((h * h).astype(BF16), ones_dv) * (1.0 / M_DV)
        gate = og_ref[rows, :].astype(F32)
        y = h * wide(lax.rsqrt(mean_sq + EPS), 2) * nw_ref[...]
        o_ref[rows, :] = (y * _sigmoid(gate)).astype(o_ref.dtype)
        return carry

    lax.fori_loop(0, seq // c, body, 0, unroll=16)


def _mlstm(qk3, z3, arow, rep, nw):
    b, s, _ = z3.shape
    vb, ob = Z_MV // M_DV, Z_MO // M_DV
    return pl.pallas_call(
        _mlstm_kernel,
        grid=(b, M_HEADS),
        in_specs=[pl.BlockSpec((None, s, M_DK), lambda i, h: (i, 0, h)),
                  pl.BlockSpec((None, s, M_DK), lambda i, h: (i, 0, M_HEADS + h)),
                  pl.BlockSpec((None, s, M_DV), lambda i, h: (i, 0, vb + h)),
                  pl.BlockSpec((None, s, M_DV), lambda i, h: (i, 0, ob + h)),
                  pl.BlockSpec((None, None, SUBLANES, s), lambda i, h: (i, h, 0, 0)),
                  pl.BlockSpec((None, None, 3, s, LANES), lambda i, h: (i, h, 0, 0, 0)),
                  pl.BlockSpec((1, M_DV), lambda i, h: (0, h))],
        out_specs=pl.BlockSpec((None, s, M_DV), lambda i, h: (i, 0, h)),
        out_shape=jax.ShapeDtypeStruct((b, s, M_HEADS * M_DV), BF16),
        scratch_shapes=[pltpu.VMEM((M_DK, M_DV + LANES), F32), pltpu.VMEM((1, LANES), F32)],
        compiler_params=_params("parallel", "arbitrary"),
        name="mlstm",
    )(qk3, qk3, z3, z3, arow, rep, nw)


def _merge_kernel(x_ref, og_ref, om_ref, ag_ref, am_ref, wg_ref, wm_ref, wo_ref, o_ref):
    merged = (_sigmoid(ag_ref[...].astype(F32)) * _dot(og_ref[...], wg_ref[...])
              + _sigmoid(am_ref[...].astype(F32)) * _dot(om_ref[...], wm_ref[...]))
    o_ref[...] = x_ref[...] + _dot(merged.astype(BF16), wo_ref[...])


def _merge(x, o_gla, o_m, z, wg, wm, wo, *, tm):
    t, d = x.shape
    row = lambda i: (i, 0)
    full = lambda i: (0, 0)
    return pl.pallas_call(
        _merge_kernel,
        grid=(t // tm,),
        in_specs=[pl.BlockSpec((tm, d), row), pl.BlockSpec((tm, d), row), pl.BlockSpec((tm, d), row),
                  pl.BlockSpec((tm, d), lambda i: (i, Z_AG // D_MODEL)),
                  pl.BlockSpec((tm, d), lambda i: (i, Z_AM // D_MODEL)),
                  pl.BlockSpec((d, d), full), pl.BlockSpec((d, d), full), pl.BlockSpec((d, d), full)],
        out_specs=pl.BlockSpec((tm, d), row),
        out_shape=jax.ShapeDtypeStruct((t, d), F32),
        compiler_params=_params("parallel"),
        name="merge",
    )(x, o_gla, o_m, z, z, wg, wm, wo)


def _xattn_kernel(x_ref, g_ref, wq_ref, kv_ref, wo_ref, o_ref):
    d = x_ref.shape[1]
    dh = d // X_HEADS
    x = x_ref[...]
    q = _dot(_rms(x, g_ref[...]).astype(BF16), wq_ref[...]).astype(BF16)
    heads = []
    for h in range(X_HEADS):
        k = kv_ref[:, h * dh:(h + 1) * dh]
        v = kv_ref[:, d + h * dh:d + (h + 1) * dh]
        s = _dot_nt(q[:, h * dh:(h + 1) * dh], k) * dh ** -0.5
        p = jnp.exp(s - jnp.max(s, axis=-1, keepdims=True))
        p = p / jnp.sum(p, axis=-1, keepdims=True)
        heads.append(_dot(p.astype(BF16), v).astype(BF16))
    o_ref[...] = x + _dot(jnp.concatenate(heads, axis=-1), wo_ref[...])


def _xattn(x3, g, wq, kv, wo, *, tq):
    b, s, d = x3.shape
    m = kv.shape[1]
    return pl.pallas_call(
        _xattn_kernel,
        grid=(b, s // tq),
        in_specs=[pl.BlockSpec((None, tq, d), lambda i, j: (i, j, 0)),
                  pl.BlockSpec((1, d), lambda i, j: (0, 0)),
                  pl.BlockSpec((d, d), lambda i, j: (0, 0)),
                  pl.BlockSpec((None, m, 2 * d), lambda i, j: (i, 0, 0)),
                  pl.BlockSpec((d, d), lambda i, j: (0, 0))],
        out_specs=pl.BlockSpec((None, tq, d), lambda i, j: (i, j, 0)),
        out_shape=jax.ShapeDtypeStruct((b, s, d), F32),
        compiler_params=_params("parallel", "parallel"),
        name="xattn",
    )(x3, g, wq, kv, wo)


def _ffn_kernel(*refs, chunks, final_norm):
    if final_norm:
        x_ref, g_ref, w1_ref, w3_ref, w2_ref, gf_ref, o_ref = refs
    else:
        x_ref, g_ref, w1_ref, w3_ref, w2_ref, o_ref = refs
    x = x_ref[...]
    h = _rms(x, g_ref[...]).astype(BF16)
    out = x
    for c0, c1 in chunks:
        a = _dot(h, w1_ref[:, c0:c1])
        act = (a * _sigmoid(a) * _dot(h, w3_ref[:, c0:c1])).astype(BF16)
        out = out + _dot(act, w2_ref[c0:c1, :])
    if final_norm:
        out = _rms(out, gf_ref[...])
    o_ref[...] = out


def _ffn(x, g, w1, w3, w2, g_final, *, tm):
    t, d = x.shape
    f = w2.shape[0]
    assert f % MXU_TILE == 0
    half = (f // MXU_TILE + 1) // 2 * MXU_TILE
    chunks = ((0, half), (half, f))
    resident = pl.Buffered(1)
    in_specs = [pl.BlockSpec((tm, d), lambda i: (i, 0)),
                pl.BlockSpec((1, d), lambda i: (0, 0)),
                pl.BlockSpec((d, f), lambda i: (0, 0), pipeline_mode=resident),
                pl.BlockSpec((d, f), lambda i: (0, 0), pipeline_mode=resident),
                pl.BlockSpec((f, d), lambda i: (0, 0), pipeline_mode=resident)]
    args = [x, g, w1, w3, w2]
    if g_final is not None:
        in_specs.append(pl.BlockSpec((1, d), lambda i: (0, 0)))
        args.append(g_final)
    return pl.pallas_call(
        functools.partial(_ffn_kernel, chunks=chunks, final_norm=g_final is not None),
        grid=(t // tm,),
        in_specs=in_specs,
        out_specs=pl.BlockSpec((tm, d), lambda i: (i, 0)),
        out_shape=jax.ShapeDtypeStruct((t, d), F32),
        compiler_params=_params("parallel"),
        name="dense_ffn",
    )(*args)


def _store_packed(ref, y):
    half = y.shape[1] // 2
    bits = lax.bitcast_convert_type(y.astype(BF16).astype(F32), jnp.int32)
    lo = lax.shift_right_logical(bits[:, :half], 16)
    hi = jnp.bitwise_and(bits[:, half:], jnp.int32(-65536))
    packed = jnp.bitwise_or(hi, lo)
    w = half // PACK_HALVES
    for p in range(PACK_HALVES):
        ref[p] = packed[:, p * w:(p + 1) * w]


def _load_packed(ref):
    packed = jnp.concatenate([ref[p] for p in range(PACK_HALVES)], axis=1)
    lo = lax.bitcast_convert_type(lax.shift_left(packed, 16), F32)
    hi = lax.bitcast_convert_type(jnp.bitwise_and(packed, jnp.int32(-65536)), F32)
    return jnp.concatenate([lo, hi], axis=1)


def _router_kernel(x_ref, g_ref, r_ref, hp_ref, mi_ref, mw_ref, cnt_ref, carry_scr):
    @pl.when(pl.program_id(0) == 0)
    def _():
        carry_scr[...] = jnp.zeros_like(carry_scr)

    rows = x_ref.shape[0]
    h = _rms(x_ref[...], g_ref[...])
    _store_packed(hp_ref, h)
    r = r_ref[...]
    h_hi = h.astype(BF16)
    h_lo = (h - h_hi.astype(F32)).astype(BF16)
    r_hi = r.astype(BF16)
    r_lo = (r - r_hi.astype(F32)).astype(BF16)
    logits = _dot(h_hi, r_hi) + (_dot(h_hi, r_lo) + _dot(h_lo, r_hi))
    lane = lax.broadcasted_iota(jnp.int32, logits.shape, 1)
    logits = jnp.where(lane < N_EXPERTS, logits, -jnp.inf)
    m1 = jnp.max(logits, axis=-1, keepdims=True)
    i1 = jnp.min(jnp.where(logits == m1, lane, LANES), axis=-1, keepdims=True)
    rest = jnp.where(lane == i1, -jnp.inf, logits)
    m2 = jnp.max(rest, axis=-1, keepdims=True)
    i2 = jnp.min(jnp.where(rest == m2, lane, LANES), axis=-1, keepdims=True)
    e2 = jnp.exp(m2 - m1)
    w1 = 1.0 / (1.0 + e2)
    w2 = e2 / (1.0 + e2)
    oh1 = (lane == i1).astype(F32)
    oh2 = (lane == i2).astype(F32)
    oh = oh1 + oh2
    earlier = (lax.broadcasted_iota(jnp.int32, (rows, rows), 0) > lax.broadcasted_iota(jnp.int32, (rows, rows), 1))
    before = _dot(earlier.astype(BF16), oh.astype(BF16)) + carry_scr[...]
    rank1 = jnp.sum(oh1 * before, axis=-1, keepdims=True).astype(jnp.int32)
    rank2 = jnp.sum(oh2 * before, axis=-1, keepdims=True).astype(jnp.int32)
    carry_scr[...] += jnp.sum(oh, axis=0, keepdims=True)
    cnt_ref[...] = carry_scr[...]
    mi_ref[...] = jnp.where(lane == 0, i1, jnp.where(lane == 1, i2, jnp.where(lane == 2, rank1,
                            jnp.where(lane == 3, rank2, 0))))
    mw_ref[...] = jnp.where(lane == 0, w1, jnp.where(lane == 1, w2, 0.0))


def _router(x, g, router, *, tm):
    t, d = x.shape
    row = lambda i: (i, 0)
    fix = lambda i: (0, 0)
    return pl.pallas_call(
        _router_kernel,
        grid=(t // tm,),
        in_specs=[pl.BlockSpec((tm, d), row), pl.BlockSpec((1, d), fix), pl.BlockSpec((d, LANES), fix)],
        out_specs=[pl.BlockSpec((PACK_HALVES, tm, PACK_W), lambda i: (0, i, 0)), pl.BlockSpec((tm, LANES), row),
                   pl.BlockSpec((tm, LANES), row), pl.BlockSpec((1, LANES), fix)],
        out_shape=[jax.ShapeDtypeStruct((PACK_HALVES, t, PACK_W), jnp.int32),
                   jax.ShapeDtypeStruct((t, LANES), jnp.int32),
                   jax.ShapeDtypeStruct((t, LANES), F32), jax.ShapeDtypeStruct((1, LANES), F32)],
        scratch_shapes=[pltpu.VMEM((1, LANES), F32)],
        compiler_params=_params("arbitrary"),
        name="router",
    )(x, g, router)


def _grouped_kernel(te_ref, tv_ref, xs_ref, w1_ref, w3_ref, w2_ref, ys_ref, h_scr, acc_scr):
    i = pl.program_id(0)
    j = pl.program_id(1)
    last = pl.num_programs(1) - 1
    valid = tv_ref[i]
    active = valid > 0

    @pl.when(jnp.logical_and(active, j == 0))
    def _():
        h = _load_packed(xs_ref)
        row = lax.broadcasted_iota(jnp.int32, h.shape, 0)
        h_scr[...] = jnp.where(row < valid, h, 0.0).astype(BF16)
        acc_scr[...] = jnp.zeros_like(acc_scr)

    @pl.when(active)
    def _():
        h = h_scr[...]
        a = _dot(h, w1_ref[...])
        act = (a * _sigmoid(a) * _dot(h, w3_ref[...])).astype(BF16)
        acc_scr[...] += _dot(act, w2_ref[...])

    @pl.when(jnp.logical_and(active, j == last))
    def _():
        _store_packed(ys_ref, acc_scr[...])

    @pl.when(jnp.logical_and(jnp.logical_not(active), j == last))
    def _():
        ys_ref[...] = jnp.zeros_like(ys_ref)


def _grouped_swiglu(tile_expert, tile_valid, xs, w13, w2, *, tmg, tf):
    _, p, _ = xs.shape
    d = w2.shape[2]
    f = w2.shape[1]
    nj = f // tf

    def jj(i, j, tv):
        return jnp.where(tv[i] > 0, j, nj - 1)

    grid_spec = pltpu.PrefetchScalarGridSpec(
        num_scalar_prefetch=2,
        grid=(p // tmg, nj),
        in_specs=[pl.BlockSpec((PACK_HALVES, tmg, PACK_W), lambda i, j, te, tv: (0, i, 0)),
                  pl.BlockSpec((None, d, tf), lambda i, j, te, tv: (te[i], 0, jj(i, j, tv))),
                  pl.BlockSpec((None, d, tf), lambda i, j, te, tv: (te[i], 0, nj + jj(i, j, tv))),
                  pl.BlockSpec((None, tf, d), lambda i, j, te, tv: (te[i], jj(i, j, tv), 0))],
        out_specs=pl.BlockSpec((PACK_HALVES, tmg, PACK_W), lambda i, j, te, tv: (0, i, 0)),
        scratch_shapes=[pltpu.VMEM((tmg, d), BF16), pltpu.VMEM((tmg, d), F32)],
    )
    return pl.pallas_call(
        _grouped_kernel,
        grid_spec=grid_spec,
        out_shape=jax.ShapeDtypeStruct((PACK_HALVES, p, PACK_W), jnp.int32),
        compiler_params=_params("arbitrary", "arbitrary"),
        name="grouped_swiglu",
    )(tile_expert, tile_valid, xs, w13, w13, w2)


def _combine_kernel(*refs, final_norm):
    if final_norm:
        x_ref, mw_ref, y1_ref, y2_ref, gf_ref, o_ref = refs
    else:
        x_ref, mw_ref, y1_ref, y2_ref, o_ref = refs
    mw = mw_ref[...]
    out = x_ref[...] + mw[:, 0:1] * _load_packed(y1_ref) + mw[:, 1:2] * _load_packed(y2_ref)
    if final_norm:
        out = _rms(out, gf_ref[...])
    o_ref[...] = out


def _combine(x, mw, y12, g_final, *, tm):
    t, d = x.shape
    row = lambda i: (i, 0)
    in_specs = [pl.BlockSpec((tm, d), row), pl.BlockSpec((tm, LANES), row),
                pl.BlockSpec((PACK_HALVES, tm, PACK_W), lambda i: (0, i, 0)),
                pl.BlockSpec((PACK_HALVES, tm, PACK_W), lambda i: (0, t // tm + i, 0))]
    args = [x, mw, y12, y12]
    if g_final is not None:
        in_specs.append(pl.BlockSpec((1, d), lambda i: (0, 0)))
        args.append(g_final)
    return pl.pallas_call(
        functools.partial(_combine_kernel, final_norm=g_final is not None),
        grid=(t // tm,),
        in_specs=in_specs,
        out_specs=pl.BlockSpec((tm, d), row),
        out_shape=jax.ShapeDtypeStruct((t, d), F32),
        compiler_params=_params("parallel"),
        name="combine",
    )(*args)


def _gather_rows(table, idx):
    halves, n_rows, width = table.shape
    n = halves * idx.shape[0]
    assert n % SC_GATHER_WINDOW == 0
    flat_idx = jnp.concatenate([idx + h * n_rows for h in range(halves)]).reshape(1, n)
    mesh = plsc.VectorSubcoreMesh(core_axis_name="core", subcore_axis_name="subcore")

    @pl.kernel(out_type=jax.ShapeDtypeStruct((n, width), table.dtype), mesh=mesh)
    def gather(table_hbm, idx_hbm, out_hbm):
        def body(idx_vmem, out_vmem):
            pltpu.sync_copy(table_hbm.at[idx_vmem.at[0]], out_vmem)

        pltpu.emit_pipeline(
            body,
            grid=(n // SC_GATHER_WINDOW,),
            in_specs=[pl.BlockSpec((1, SC_GATHER_WINDOW), lambda i: (0, i))],
            out_specs=[pl.BlockSpec((SC_GATHER_WINDOW, width), lambda i: (i, 0))],
            core_axis_name=("core", "subcore"),
            dimension_semantics=(pltpu.PARALLEL,),
        )(idx_hbm, out_hbm)

    return gather(table.reshape(halves * n_rows, width), flat_idx).reshape(halves, idx.shape[0], width)


def _dispatch_rows(table, pos_list, n_out):
    halves, t, width = table.shape
    copies = len(pos_list)
    idx = jnp.concatenate([pos + h * n_out for h in range(halves) for pos in pos_list])
    n = idx.shape[0]
    win_t = t // SC_GATHER_WINDOW
    assert t % SC_GATHER_WINDOW == 0
    mesh = plsc.VectorSubcoreMesh(core_axis_name="core", subcore_axis_name="subcore")

    @pl.kernel(out_type=jax.ShapeDtypeStruct((halves * n_out, width), table.dtype), mesh=mesh)
    def scatter(table_hbm, idx_hbm, out_hbm):
        def body(rows_vmem, idx_vmem):
            pltpu.sync_copy(rows_vmem, out_hbm.at[idx_vmem.at[0]])

        pltpu.emit_pipeline(
            body,
            grid=(n // SC_GATHER_WINDOW,),
            in_specs=[pl.BlockSpec((SC_GATHER_WINDOW, width),
                                   lambda i: ((i // (copies * win_t)) * win_t + i % win_t, 0)),
                      pl.BlockSpec((1, SC_GATHER_WINDOW), lambda i: (0, i))],
            out_specs=[],
            core_axis_name=("core", "subcore"),
            dimension_semantics=(pltpu.PARALLEL,),
        )(table_hbm, idx_hbm)

    return scatter(table.reshape(halves * t, width), idx.reshape(1, n)).reshape(halves, n_out, width)


def _moe_sparse(x, g, router, w13, w2, g_final, *, tm, tmg, tf):
    t, d = x.shape
    hp, mi, mw, cnt = _router(x, g, router, tm=tm)
    counts = cnt[0, :N_EXPERTS].astype(jnp.int32)
    tiles_per_expert = (counts + tmg - 1) // tmg
    tile_end = jnp.cumsum(tiles_per_expert)
    group_start = (tile_end - tiles_per_expert) * tmg
    e1, e2, r1, r2 = mi[:, 0], mi[:, 1], mi[:, 2], mi[:, 3]
    pos1 = group_start[e1] + r1
    pos2 = group_start[e2] + r2
    n_tiles = 2 * t // tmg + N_EXPERTS
    tile = jnp.arange(n_tiles, dtype=jnp.int32)
    tile_expert = jnp.minimum(jnp.sum(tile[:, None] >= tile_end[None, :], axis=1), N_EXPERTS - 1).astype(jnp.int32)
    rows_left = counts[tile_expert] - (tile - (tile_end - tiles_per_expert)[tile_expert]) * tmg
    tile_valid = jnp.where(tile < tile_end[-1], jnp.clip(rows_left, 0, tmg), 0).astype(jnp.int32)

    xs = _dispatch_rows(hp, [pos1, pos2], n_tiles * tmg)
    ys = _grouped_swiglu(tile_expert, tile_valid, xs, w13, w2, tmg=tmg, tf=tf)
    y12 = _gather_rows(ys, jnp.concatenate([pos1, pos2]))
    return _combine(x, mw, y12, g_final, tm=tm)


def _cast_kernel(w_ref, o_ref):
    o_ref[...] = w_ref[...].astype(o_ref.dtype)


def _cast_bf16(w, *, rows):
    shape = w.shape
    w2 = w.reshape(-1, shape[-1])
    r, c = w2.shape
    out = pl.pallas_call(
        _cast_kernel,
        grid=(r // rows,),
        in_specs=[pl.BlockSpec((rows, c), lambda i: (i, 0))],
        out_specs=pl.BlockSpec((rows, c), lambda i: (i, 0)),
        out_shape=jax.ShapeDtypeStruct((r, c), BF16),
        compiler_params=_params("parallel"),
        name="cast_bf16",
    )(w2)
    return out.reshape(shape)


def _split_w_in(w_in):
    sizes = (512, 512, 1024, GLA_RANK, 1024, 512, 512, 1024, M_HEADS, M_HEADS, 1024, 1024, 1024)
    parts, off = [], 0
    for n in sizes:
        parts.append(w_in[:, off:off + n])
        off += n
    g_q, g_k, g_v, g_lr, g_g, m_q, m_k, m_v, m_i, m_f, m_o, a_g, a_m = parts
    wz = jnp.concatenate([g_q, g_k, g_v, g_g, m_q, m_k, m_v, m_o, a_g, a_m], axis=1).astype(BF16)
    d = w_in.shape[0]
    gate_cols = jnp.stack([m_i, m_f], axis=2).reshape(d, 2 * M_HEADS)
    ws = jnp.concatenate([jnp.pad(g_lr, ((0, 0), (0, LANES - GLA_RANK))),
                          jnp.pad(gate_cols, ((0, 0), (0, LANES - 2 * M_HEADS)))], axis=1).astype(BF16)
    assert wz.shape == (d, Z_COLS) and ws.shape == (d, ZS_COLS)
    return wz, ws


def _pick(total, want):
    t = min(total, want)
    while total % t:
        t -= 1
    return t


def kernel(x, mem, norm_mix, w_in, gla_gk_up, gla_gk_bias, gla_norm, m_conv, m_gate_bias, m_norm, gla_proj, m_proj, w_out, norm_xattn, norm_mem, x_wq, x_wkv, x_wo, norm_ffn, ffn_w13, ffn_w2, moe_router, moe_w13, moe_w2, norm_final):
    b, s, d = x.shape
    m_len = mem.shape[1]
    t = b * s
    depth = norm_mix.shape[0]
    tm = _pick(t, 512)
    row2 = lambda a: a.reshape(1, -1)

    xt = x.reshape(t, d)
    mem_t = mem.reshape(b * m_len, d)
    for l in range(depth):
        last_layer = l == depth - 1
        wz, ws = _split_w_in(w_in[l])
        z, zs = _norm_matmul(xt, row2(norm_mix[l]), wz, ws, tm=tm, tn=2048)
        z3 = z.reshape(b, s, Z_COLS)
        zs3 = zs.reshape(b, s, ZS_COLS)
        up = jnp.pad(gla_gk_up[l], ((0, LANES - GLA_RANK), (0, 0))).astype(BF16)
        o_gla = _gla(z3, zs3, up, row2(gla_gk_bias[l]), row2(gla_norm[l]))
        gbias = jnp.pad(jnp.stack([m_gate_bias[l, :M_HEADS], m_gate_bias[l, M_HEADS:]], axis=1).reshape(1, -1),
                        ((0, 0), (0, LANES - 2 * M_HEADS)))
        arow, rep = _mlstm_gates(zs3, gbias)
        qk3 = _mlstm_conv(z3, m_conv[l], ts=_pick(s, 512))
        o_m = _mlstm(qk3, z3, arow, rep, row2(m_norm[l]))
        xt = _merge(xt, o_gla.reshape(t, d), o_m.reshape(t, d), z,
                    gla_proj[l].astype(BF16), m_proj[l].astype(BF16), w_out[l].astype(BF16), tm=tm)
        kv = _norm_matmul(mem_t, row2(norm_mem[l]), x_wkv[l].astype(BF16), tm=_pick(b * m_len, 512), tn=1024)
        xt = _xattn(xt.reshape(b, s, d), row2(norm_xattn[l]), x_wq[l].astype(BF16),
                    kv.reshape(b, m_len, 2 * d), x_wo[l].astype(BF16), tq=_pick(s, 1024)).reshape(t, d)
        g_final = row2(norm_final) if last_layer else None
        if l % 2 == 0:
            w13, w2 = ffn_w13[l // 2], ffn_w2[l // 2]
            f = w2.shape[0]
            fp = -(-f // MXU_TILE) * MXU_TILE
            w1 = jnp.pad(w13[:, :f], ((0, 0), (0, fp - f))).astype(BF16)
            w3 = jnp.pad(w13[:, f:], ((0, 0), (0, fp - f))).astype(BF16)
            w2p = jnp.pad(w2, ((0, fp - f), (0, 0))).astype(BF16)
            xt = _ffn(xt, row2(norm_ffn[l]), w1, w3, w2p, g_final, tm=tm)
        else:
            router = jnp.pad(moe_router[l // 2], ((0, 0), (0, LANES - N_EXPERTS)))
            w13 = _cast_bf16(moe_w13[l // 2], rows=256)
            w2 = _cast_bf16(moe_w2[l // 2], rows=512)
            xt = _moe_sparse(xt, row2(norm_ffn[l]), router, w13, w2, g_final, tm=tm, tmg=tm, tf=w2.shape[1] // 2)
    return xt.reshape(b, s, d)
```

```python
import functools

import jax
import jax.numpy as jnp
from jax import lax
from jax.experimental import pallas as pl
from jax.experimental.pallas import tpu as pltpu
from jax.experimental.pallas import tpu_sc as plsc

F32 = jnp.float32
BF16 = jnp.bfloat16

EPS = 1e-6
LOG2E = 1.4426950408889634
D_MODEL = 1024
GLA_HEADS = 4
GLA_DK = 128
GLA_DV = 256
GLA_RANK = 16
GLA_GATE_NORM = 16.0
GLA_LOG_DECAY_MIN = -1.0
GLA_CHUNK = 64
M_HEADS = 4
M_DK = 128
M_DV = 256
M_CHUNK = 128
CONV_W = 4
X_HEADS = 4
N_EXPERTS = 8
LANES = 128
MXU_TILE = 256
SUBLANES = 8
VMEM_LIMIT = 48 * 1024 * 1024
SC_GATHER_WINDOW = 128
PACK_HALVES = 2
PACK_W = D_MODEL // 2 // PACK_HALVES
CAST_PARTS = 4

Z_GQ, Z_GK, Z_GV, Z_GG = 0, 512, 1024, 2048
Z_MQ, Z_MK, Z_MV, Z_MO = 3072, 3584, 4096, 5120
Z_AG, Z_AM = 6144, 7168
Z_COLS = 8192
ZS_COLS = 2 * LANES


def _params(*sem):
    return pltpu.CompilerParams(dimension_semantics=sem, vmem_limit_bytes=VMEM_LIMIT)


def _rms(x, g):
    return x * lax.rsqrt(jnp.mean(x * x, axis=-1, keepdims=True) + EPS) * g


def _log_sigmoid(u):
    return jnp.minimum(u, 0.0) - jnp.log(1.0 + jnp.exp(-jnp.abs(u)))


def _sigmoid(u):
    return 1.0 / (1.0 + jnp.exp(-u))


def _dot(a, b):
    return jnp.dot(a, b, preferred_element_type=F32)


def _dot_nt(a, b):
    return lax.dot_general(a, b, (((1,), (1,)), ((), ())), preferred_element_type=F32)


def _dot_tn(a, b):
    return lax.dot_general(a, b, (((0,), (0,)), ((), ())), preferred_element_type=F32)


def _cumsum_rows(x):
    n = x.shape[0]
    row = lax.broadcasted_iota(jnp.int32, x.shape, 0)
    s = 1
    while s < n:
        x = x + jnp.where(row >= s, pltpu.roll(x, s, 0), 0.0)
        s *= 2
    return x


def _norm_matmul_kernel(*refs, tn, with_small):
    if with_small:
        x_ref, g_ref, w_ref, ws_ref, z_ref, zs_ref = refs
    else:
        x_ref, g_ref, w_ref, z_ref = refs
    h = _rms(x_ref[...], g_ref[...]).astype(BF16)
    if with_small:
        zs_ref[...] = _dot(h, ws_ref[...])
    for c0 in range(0, z_ref.shape[1], tn):
        z_ref[:, c0:c0 + tn] = _dot(h, w_ref[:, c0:c0 + tn]).astype(z_ref.dtype)


def _norm_matmul(x, g, w, ws=None, *, tm, tn):
    t, d = x.shape
    n = w.shape[1]
    resident = pl.Buffered(1)
    in_specs = [pl.BlockSpec((tm, d), lambda i: (i, 0)),
                pl.BlockSpec((1, d), lambda i: (0, 0)),
                pl.BlockSpec((d, n), lambda i: (0, 0), pipeline_mode=resident)]
    out_specs = [pl.BlockSpec((tm, n), lambda i: (i, 0))]
    out_shape = [jax.ShapeDtypeStruct((t, n), BF16)]
    args = [x, g, w]
    if ws is not None:
        ns = ws.shape[1]
        in_specs.append(pl.BlockSpec((d, ns), lambda i: (0, 0), pipeline_mode=resident))
        out_specs.append(pl.BlockSpec((tm, ns), lambda i: (i, 0)))
        out_shape.append(jax.ShapeDtypeStruct((t, ns), F32))
        args.append(ws)
    out = pl.pallas_call(
        functools.partial(_norm_matmul_kernel, tn=tn, with_small=ws is not None),
        grid=(t // tm,),
        in_specs=in_specs, out_specs=out_specs, out_shape=out_shape,
        compiler_params=_params("parallel"),
        name="norm_matmul",
    )(*args)
    return out if ws is not None else out[0]


def _gla_kernel(q_ref, k_ref, v_ref, gg_ref, lr_ref, up_ref, gb_ref, nw_ref, o_ref, la_scr, st_scr):
    seq = q_ref.shape[0]
    c = GLA_CHUNK
    scale = GLA_DK ** -0.5

    u = _dot(lr_ref[...].astype(BF16), up_ref[...]) + gb_ref[...]
    la_scr[...] = jnp.maximum(_log_sigmoid(u) * (1.0 / GLA_GATE_NORM), GLA_LOG_DECAY_MIN)
    st_scr[...] = jnp.zeros_like(st_scr)
    causal = (lax.broadcasted_iota(jnp.int32, (c, c), 0) >= lax.broadcasted_iota(jnp.int32, (c, c), 1))

    def body(n, carry):
        r0 = pl.multiple_of(n * c, c)
        rows = pl.ds(r0, c)
        cum = _cumsum_rows(la_scr[rows, :])
        cum_last = cum[c - 1:c, :]
        q = q_ref[rows, :].astype(F32) * scale
        k = k_ref[rows, :].astype(F32)
        v = v_ref[rows, :]
        q_dec = (q * jnp.exp(cum)).astype(BF16)
        k_inv = (k * jnp.exp(-cum)).astype(BF16)
        k_end = (k * jnp.exp(cum_last - cum)).astype(BF16)
        scores = jnp.where(causal, _dot_nt(q_dec, k_inv), 0.0).astype(BF16)
        st = st_scr[...]
        o = _dot(scores, v) + _dot_nt(q_dec, st.astype(BF16))
        st_scr[...] = st * jnp.exp(cum_last) + _dot_tn(v, k_end)
        gate = gg_ref[rows, :].astype(F32)
        o_ref[rows, :] = (_rms(o, nw_ref[...]) * (gate * _sigmoid(gate))).astype(o_ref.dtype)
        return carry

    lax.fori_loop(0, seq // c, body, 0, unroll=8)


def _gla(z3, zs3, up, gb, nw):
    b, s, _ = z3.shape
    qb, kb, vb, gb_ = Z_GQ // GLA_DK, Z_GK // GLA_DK, Z_GV // GLA_DV, Z_GG // GLA_DV
    return pl.pallas_call(
        _gla_kernel,
        grid=(b, GLA_HEADS),
        in_specs=[pl.BlockSpec((None, s, GLA_DK), lambda i, h: (i, 0, qb + h)),
                  pl.BlockSpec((None, s, GLA_DK), lambda i, h: (i, 0, kb + h)),
                  pl.BlockSpec((None, s, GLA_DV), lambda i, h: (i, 0, vb + h)),
                  pl.BlockSpec((None, s, GLA_DV), lambda i, h: (i, 0, gb_ + h)),
                  pl.BlockSpec((None, s, LANES), lambda i, h: (i, 0, 0)),
                  pl.BlockSpec((LANES, GLA_DK), lambda i, h: (0, h)),
                  pl.BlockSpec((1, GLA_DK), lambda i, h: (0, h)),
                  pl.BlockSpec((1, GLA_DV), lambda i, h: (0, h))],
        out_specs=pl.BlockSpec((None, s, GLA_DV), lambda i, h: (i, 0, h)),
        out_shape=jax.ShapeDtypeStruct((b, s, GLA_HEADS * GLA_DV), BF16),
        scratch_shapes=[pltpu.VMEM((s, GLA_DK), F32), pltpu.VMEM((GLA_DV, GLA_DK), F32)],
        compiler_params=_params("parallel", "arbitrary"),
        name="gla",
    )(z3, z3, z3, z3, zs3, up, gb, nw)


def _chunk_scan_lanes(x, combine, identity, chunk):
    lane_in_chunk = lax.broadcasted_iota(jnp.int32, x.shape, 1) % chunk
    step = 1
    while step < chunk:
        x = combine(x, jnp.where(lane_in_chunk >= step, pltpu.roll(x, step, 1), identity))
        step *= 2
    return x


def _replicate_rows(x, row):
    hi = x.astype(BF16)
    r1 = x - hi.astype(F32)
    mid = r1.astype(BF16)
    lo = (r1 - mid.astype(F32)).astype(BF16)
    parts = jnp.concatenate([hi, mid, lo, jnp.zeros_like(hi)], axis=0)
    pick = lax.broadcasted_iota(jnp.int32, (4 * SUBLANES, LANES), 0) % SUBLANES == row
    return _dot_tn(parts, jnp.where(pick, 1.0, 0.0).astype(BF16))


def _mlstm_gates_kernel(gt_ref, gb_ref, arow_ref, rep_ref):
    c = M_CHUNK
    gates_t = (gt_ref[...] + gb_ref[...]).T
    top = gates_t[0:SUBLANES, :]
    odd = lax.broadcasted_iota(jnp.int32, top.shape, 0) % 2 == 1
    lf = jnp.where(odd, _log_sigmoid(top), 0.0)
    b_odd = _chunk_scan_lanes(lf, jnp.add, 0.0, c)
    a_even = top - pltpu.roll(b_odd, SUBLANES - 1, 0)
    amax_even = _chunk_scan_lanes(jnp.where(odd, -jnp.inf, a_even), jnp.maximum, -jnp.inf, c)
    for h in range(M_HEADS):
        arow_ref[h] = pltpu.roll(a_even, (SUBLANES - 2 * h) % SUBLANES, 0)
        rep_ref[h, 0] = _replicate_rows(b_odd, 2 * h + 1)
        rep_ref[h, 1] = _replicate_rows(a_even, 2 * h)
        rep_ref[h, 2] = _replicate_rows(jnp.where(odd, 0.0, amax_even), 2 * h)


def _mlstm_gates(zs3, gbias):
    b, s, _ = zs3.shape
    return pl.pallas_call(
        _mlstm_gates_kernel,
        grid=(b,),
        in_specs=[pl.BlockSpec((None, s, LANES), lambda i: (i, 0, 1)),
                  pl.BlockSpec((1, LANES), lambda i: (0, 0))],
        out_specs=[pl.BlockSpec((None, M_HEADS, SUBLANES, s), lambda i: (i, 0, 0, 0)),
                   pl.BlockSpec((None, M_HEADS, 3, s, LANES), lambda i: (i, 0, 0, 0, 0))],
        out_shape=[jax.ShapeDtypeStruct((b, M_HEADS, SUBLANES, s), F32),
                   jax.ShapeDtypeStruct((b, M_HEADS, 3, s, LANES), F32)],
        compiler_params=_params("parallel"),
        name="mlstm_gates",
    )(zs3, gbias)


def _mlstm_conv_kernel(cur_ref, prev_ref, w_ref, shift_ref, o_ref):
    c = M_CHUNK
    d = cur_ref.shape[1]
    prev = jnp.where(pl.program_id(1) > 0, prev_ref[...], jnp.zeros_like(prev_ref))
    lane = lax.broadcasted_iota(jnp.int32, (1, d), 1)
    col_scale = jnp.where(lane < d // 2, M_DK ** -0.5, 1.0)
    for i in range(cur_ref.shape[0] // c):
        cur = cur_ref[i * c:(i + 1) * c, :]
        before = prev if i == 0 else cur_ref[(i - 1) * c:i * c, :]
        shifted = _dot(shift_ref[...], jnp.concatenate([before, cur], axis=0))
        acc = w_ref[CONV_W - 1:CONV_W, :] * cur.astype(F32)
        for s in range(1, CONV_W):
            acc = acc + w_ref[CONV_W - 1 - s:CONV_W - s, :] * shifted[(s - 1) * c:s * c, :]
        o_ref[i * c:(i + 1) * c, :] = (acc * _sigmoid(acc) * col_scale).astype(o_ref.dtype)


def _mlstm_conv(z3, conv, *, ts):
    b, s, _ = z3.shape
    c = M_CHUNK
    d = conv.shape[1]
    blk = Z_MQ // d
    assert Z_MQ % d == 0 and Z_MK == Z_MQ + d // 2
    t_idx = jnp.arange(c)[:, None]
    col = jnp.arange(2 * c)[None, :]
    shift = jnp.concatenate([(col == c + t_idx - s) for s in range(1, CONV_W)], axis=0).astype(BF16)
    return pl.pallas_call(
        _mlstm_conv_kernel,
        grid=(b, s // ts),
        in_specs=[pl.BlockSpec((None, ts, d), lambda i, j: (i, j, blk)),
                  pl.BlockSpec((None, c, d), lambda i, j: (i, jnp.maximum(j * (ts // c) - 1, 0), blk)),
                  pl.BlockSpec((CONV_W, d), lambda i, j: (0, 0)),
                  pl.BlockSpec(((CONV_W - 1) * c, 2 * c), lambda i, j: (0, 0))],
        out_specs=pl.BlockSpec((None, ts, d), lambda i, j: (i, j, 0)),
        out_shape=jax.ShapeDtypeStruct((b, s, d), BF16),
        compiler_params=_params("parallel", "parallel"),
        name="mlstm_conv",
    )(z3, z3, conv, shift)


def _mlstm_kernel(q_ref, k_ref, v_ref, og_ref, arow_ref, rep_ref, nw_ref, o_ref, st_scr, m_scr):
    seq = q_ref.shape[0]
    c = M_CHUNK
    st_scr[...] = jnp.zeros_like(st_scr)
    m_scr[...] = jnp.zeros_like(m_scr)
    causal = (lax.broadcasted_iota(jnp.int32, (c, c), 0) >= lax.broadcasted_iota(jnp.int32, (c, c), 1))
    ones_cols = jnp.ones((c, LANES), BF16)
    ones_dv = jnp.ones((M_DV, LANES), BF16)

    def wide(x, n):
        return jnp.concatenate([x] * n, axis=1)

    def body(n, carry):
        r0 = pl.multiple_of(n * c, c)
        rows = pl.ds(r0, c)
        qb = q_ref[rows, :]
        kb = k_ref[rows, :]
        v_ext = jnp.concatenate([v_ref[rows, :], ones_cols], axis=1)
        b = rep_ref[0, rows, :]
        a = rep_ref[1, rows, :]
        amax = rep_ref[2, rows, :]
        a_row = arow_ref[0:1, rows]
        m_prev = m_scr[...]
        st = st_scr[...]

        dmat = jnp.where(causal, b + a_row, -jnp.inf)
        m_inter = b + m_prev
        m_t = jnp.maximum(m_inter, b + amax)
        w_ts = jnp.exp(dmat - m_t) * _dot_nt(qb, kb)
        s_inter = jnp.exp(m_inter - m_t)
        nd = _dot(w_ts.astype(BF16), v_ext) + wide(s_inter, 3) * _dot(qb, st.astype(BF16))
        inv = 1.0 / jnp.maximum(jnp.abs(nd[:, M_DV:]), jnp.exp(-m_t))
        h = nd[:, :M_DV] * wide(inv, 2)

        b_last = b[c - 1:c, :]
        m_new = jnp.maximum(b_last + m_prev, b_last + amax[c - 1:c, :])
        kw = kb.astype(F32) * jnp.exp(b_last + a - m_new)
        st_scr[...] = wide(jnp.exp(b_last + m_prev - m_new), 3) * st + _dot_tn(kw.astype(BF16), v_ext)
        m_scr[...] = m_new

        mean_sq = _dot((h * h).astype(BF16), ones_dv) * (1.0 / M_DV)
        gate = og_ref[rows, :].astype(F32)
        y = h * wide(lax.rsqrt(mean_sq + EPS), 2) * nw_ref[...]
        o_ref[rows, :] = (y * _sigmoid(gate)).astype(o_ref.dtype)
        return carry

    lax.fori_loop(0, seq // c, body, 0, unroll=16)


def _mlstm(qk3, z3, arow, rep, nw):
    b, s, _ = z3.shape
    vb, ob = Z_MV // M_DV, Z_MO // M_DV
    return pl.pallas_call(
        _mlstm_kernel,
        grid=(b, M_HEADS),
        in_specs=[pl.BlockSpec((None, s, M_DK), lambda i, h: (i, 0, h)),
                  pl.BlockSpec((None, s, M_DK), lambda i, h: (i, 0, M_HEADS + h)),
                  pl.BlockSpec((None, s, M_DV), lambda i, h: (i, 0, vb + h)),
                  pl.BlockSpec((None, s, M_DV), lambda i, h: (i, 0, ob + h)),
                  pl.BlockSpec((None, None, SUBLANES, s), lambda i, h: (i, h, 0, 0)),
                  pl.BlockSpec((None, None, 3, s, LANES), lambda i, h: (i, h, 0, 0, 0)),
                  pl.BlockSpec((1, M_DV), lambda i, h: (0, h))],
        out_specs=pl.BlockSpec((None, s, M_DV), lambda i, h: (i, 0, h)),
        out_shape=jax.ShapeDtypeStruct((b, s, M_HEADS * M_DV), BF16),
        scratch_shapes=[pltpu.VMEM((M_DK, M_DV + LANES), F32), pltpu.VMEM((1, LANES), F32)],
        compiler_params=_params("parallel", "arbitrary"),
        name="mlstm",
    )(qk3, qk3, z3, z3, arow, rep, nw)


def _merge_kernel(x_ref, og_ref, om_ref, ag_ref, am_ref, wg_ref, wm_ref, wo_ref, o_ref):
    merged = (_sigmoid(ag_ref[...].astype(F32)) * _dot(og_ref[...], wg_ref[...])
              + _sigmoid(am_ref[...].astype(F32)) * _dot(om_ref[...], wm_ref[...]))
    o_ref[...] = x_ref[...] + _dot(merged.astype(BF16), wo_ref[...])


def _merge(x, o_gla, o_m, z, wg, wm, wo, *, tm):
    t, d = x.shape
    row = lambda i: (i, 0)
    full = lambda i: (0, 0)
    return pl.pallas_call(
        _merge_kernel,
        grid=(t // tm,),
        in_specs=[pl.BlockSpec((tm, d), row), pl.BlockSpec((tm, d), row), pl.BlockSpec((tm, d), row),
                  pl.BlockSpec((tm, d), lambda i: (i, Z_AG // D_MODEL)),
                  pl.BlockSpec((tm, d), lambda i: (i, Z_AM // D_MODEL)),
                  pl.BlockSpec((d, d), full), pl.BlockSpec((d, d), full), pl.BlockSpec((d, d), full)],
        out_specs=pl.BlockSpec((tm, d), row),
        out_shape=jax.ShapeDtypeStruct((t, d), F32),
        compiler_params=_params("parallel"),
        name="merge",
    )(x, o_gla, o_m, z, z, wg, wm, wo)


def _xattn_kernel(x_ref, g_ref, wq_ref, kv_ref, wo_ref, o_ref):
    d = x_ref.shape[1]
    dh = d // X_HEADS
    x = x_ref[...]
    q = _dot(_rms(x, g_ref[...]).astype(BF16), wq_ref[...]).astype(BF16)
    heads = []
    ones_m = jnp.ones((kv_ref.shape[0], LANES), BF16)
    for h in range(X_HEADS):
        k = kv_ref[:, h * dh:(h + 1) * dh]
        v = kv_ref[:, d + h * dh:d + (h + 1) * dh]
        s = _dot_nt(q[:, h * dh:(h + 1) * dh], k)
        p = jnp.exp2((s - jnp.max(s, axis=-1, keepdims=True)) * (dh ** -0.5 * LOG2E)).astype(BF16)
        inv_l = 1.0 / _dot(p, ones_m)
        heads.append((_dot(p, v) * jnp.concatenate([inv_l] * (dh // LANES), axis=1)).astype(BF16))
    o_ref[...] = x + _dot(jnp.concatenate(heads, axis=-1), wo_ref[...])


def _xattn(x3, g, wq, kv, wo, *, tq):
    b, s, d = x3.shape
    m = kv.shape[1]
    return pl.pallas_call(
        _xattn_kernel,
        grid=(b, s // tq),
        in_specs=[pl.BlockSpec((None, tq, d), lambda i, j: (i, j, 0)),
                  pl.BlockSpec((1, d), lambda i, j: (0, 0)),
                  pl.BlockSpec((d, d), lambda i, j: (0, 0)),
                  pl.BlockSpec((None, m, 2 * d), lambda i, j: (i, 0, 0)),
                  pl.BlockSpec((d, d), lambda i, j: (0, 0))],
        out_specs=pl.BlockSpec((None, tq, d), lambda i, j: (i, j, 0)),
        out_shape=jax.ShapeDtypeStruct((b, s, d), F32),
        compiler_params=_params("parallel", "parallel"),
        name="xattn",
    )(x3, g, wq, kv, wo)


def _ffn_kernel(*refs, chunks, final_norm):
    if final_norm:
        x_ref, g_ref, w1_ref, w3_ref, w2_ref, gf_ref, o_ref = refs
    else:
        x_ref, g_ref, w1_ref, w3_ref, w2_ref, o_ref = refs
    x = x_ref[...]
    h = _rms(x, g_ref[...]).astype(BF16)
    out = x
    for c0, c1 in chunks:
        a = _dot(h, w1_ref[:, c0:c1])
        act = (a * _sigmoid(a) * _dot(h, w3_ref[:, c0:c1])).astype(BF16)
        out = out + _dot(act, w2_ref[c0:c1, :])
    if final_norm:
        out = _rms(out, gf_ref[...])
    o_ref[...] = out


def _ffn(x, g, w1, w3, w2, g_final, *, tm):
    t, d = x.shape
    f = w2.shape[0]
    assert f % MXU_TILE == 0
    half = (f // MXU_TILE + 1) // 2 * MXU_TILE
    chunks = ((0, half), (half, f))
    resident = pl.Buffered(1)
    in_specs = [pl.BlockSpec((tm, d), lambda i: (i, 0)),
                pl.BlockSpec((1, d), lambda i: (0, 0)),
                pl.BlockSpec((d, f), lambda i: (0, 0), pipeline_mode=resident),
                pl.BlockSpec((d, f), lambda i: (0, 0), pipeline_mode=resident),
                pl.BlockSpec((f, d), lambda i: (0, 0), pipeline_mode=resident)]
    args = [x, g, w1, w3, w2]
    if g_final is not None:
        in_specs.append(pl.BlockSpec((1, d), lambda i: (0, 0)))
        args.append(g_final)
    return pl.pallas_call(
        functools.partial(_ffn_kernel, chunks=chunks, final_norm=g_final is not None),
        grid=(t // tm,),
        in_specs=in_specs,
        out_specs=pl.BlockSpec((tm, d), lambda i: (i, 0)),
        out_shape=jax.ShapeDtypeStruct((t, d), F32),
        compiler_params=_params("parallel"),
        name="dense_ffn",
    )(*args)


def _store_packed(ref, y):
    half = y.shape[1] // 2
    bits = lax.bitcast_convert_type(y.astype(BF16).astype(F32), jnp.int32)
    lo = lax.shift_right_logical(bits[:, :half], 16)
    hi = jnp.bitwise_and(bits[:, half:], jnp.int32(-65536))
    packed = jnp.bitwise_or(hi, lo)
    w = half // PACK_HALVES
    for p in range(PACK_HALVES):
        ref[p] = packed[:, p * w:(p + 1) * w]


def _load_packed(ref):
    packed = jnp.concatenate([ref[p] for p in range(PACK_HALVES)], axis=1)
    lo = lax.bitcast_convert_type(lax.shift_left(packed, 16), F32)
    hi = lax.bitcast_convert_type(jnp.bitwise_and(packed, jnp.int32(-65536)), F32)
    return jnp.concatenate([lo, hi], axis=1)


def _router_kernel(x_ref, g_ref, r_ref, hp_ref, mi_ref, mw_ref, cnt_ref, carry_scr):
    @pl.when(pl.program_id(0) == 0)
    def _():
        carry_scr[...] = jnp.zeros_like(carry_scr)

    rows = x_ref.shape[0]
    h = _rms(x_ref[...], g_ref[...])
    _store_packed(hp_ref, h)
    r = r_ref[...]
    h_hi = h.astype(BF16)
    h_lo = (h - h_hi.astype(F32)).astype(BF16)
    r_hi = r.astype(BF16)
    r_lo = (r - r_hi.astype(F32)).astype(BF16)
    logits = _dot(h_hi, r_hi) + (_dot(h_hi, r_lo) + _dot(h_lo, r_hi))
    lane = lax.broadcasted_iota(jnp.int32, logits.shape, 1)
    logits = jnp.where(lane < N_EXPERTS, logits, -jnp.inf)
    m1 = jnp.max(logits, axis=-1, keepdims=True)
    i1 = jnp.min(jnp.where(logits == m1, lane, LANES), axis=-1, keepdims=True)
    rest = jnp.where(lane == i1, -jnp.inf, logits)
    m2 = jnp.max(rest, axis=-1, keepdims=True)
    i2 = jnp.min(jnp.where(rest == m2, lane, LANES), axis=-1, keepdims=True)
    e2 = jnp.exp(m2 - m1)
    w1 = 1.0 / (1.0 + e2)
    w2 = e2 / (1.0 + e2)
    oh1 = (lane == i1).astype(F32)
    oh2 = (lane == i2).astype(F32)
    oh = oh1 + oh2
    earlier = (lax.broadcasted_iota(jnp.int32, (rows, rows), 0) > lax.broadcasted_iota(jnp.int32, (rows, rows), 1))
    before = _dot(earlier.astype(BF16), oh.astype(BF16)) + carry_scr[...]
    rank1 = jnp.sum(oh1 * before, axis=-1, keepdims=True)
    rank2 = jnp.sum(oh2 * before, axis=-1, keepdims=True)
    carry_scr[...] += jnp.sum(oh, axis=0, keepdims=True)
    cnt_ref[...] = carry_scr[...]
    meta = jnp.where(lane == 0, i1.astype(F32), jnp.where(lane == 1, i2.astype(F32), jnp.where(lane == 2, rank1,
                     jnp.where(lane == 3, rank2, 0.0))))
    mi_ref[...] = meta.T[0:SUBLANES, :]
    mw_ref[...] = jnp.where(lane == 0, w1, jnp.where(lane == 1, w2, 0.0))


def _router(x, g, router, *, tm):
    t, d = x.shape
    row = lambda i: (i, 0)
    fix = lambda i: (0, 0)
    return pl.pallas_call(
        _router_kernel,
        grid=(t // tm,),
        in_specs=[pl.BlockSpec((tm, d), row), pl.BlockSpec((1, d), fix), pl.BlockSpec((d, LANES), fix)],
        out_specs=[pl.BlockSpec((PACK_HALVES, tm, PACK_W), lambda i: (0, i, 0)),
                   pl.BlockSpec((SUBLANES, tm), lambda i: (0, i)),
                   pl.BlockSpec((tm, LANES), row), pl.BlockSpec((1, LANES), fix)],
        out_shape=[jax.ShapeDtypeStruct((PACK_HALVES, t, PACK_W), jnp.int32),
                   jax.ShapeDtypeStruct((SUBLANES, t), F32),
                   jax.ShapeDtypeStruct((t, LANES), F32), jax.ShapeDtypeStruct((1, LANES), F32)],
        scratch_shapes=[pltpu.VMEM((1, LANES), F32)],
        compiler_params=_params("arbitrary"),
        name="router",
    )(x, g, router)


def _grouped_kernel(te_ref, tv_ref, xs_ref, w1_ref, w3_ref, w2_ref, ys_ref, h_scr, acc_scr):
    i = pl.program_id(0)
    j = pl.program_id(1)
    last = pl.num_programs(1) - 1
    valid = tv_ref[i]
    active = valid > 0

    @pl.when(jnp.logical_and(active, j == 0))
    def _():
        h = _load_packed(xs_ref)
        row = lax.broadcasted_iota(jnp.int32, h.shape, 0)
        h_scr[...] = jnp.where(row < valid, h, 0.0).astype(BF16)
        acc_scr[...] = jnp.zeros_like(acc_scr)

    @pl.when(active)
    def _():
        h = h_scr[...]
        a = _dot(h, w1_ref[...])
        act = (a * _sigmoid(a) * _dot(h, w3_ref[...])).astype(BF16)
        acc_scr[...] += _dot(act, w2_ref[...])

    @pl.when(jnp.logical_and(active, j == last))
    def _():
        _store_packed(ys_ref, acc_scr[...])

    @pl.when(jnp.logical_and(jnp.logical_not(active), j == last))
    def _():
        ys_ref[...] = jnp.zeros_like(ys_ref)


def _grouped_swiglu(tile_expert, tile_valid, xs, w13, w2, *, tmg, tf):
    _, p, _ = xs.shape
    d = w2.shape[2]
    f = w2.shape[1]
    nj = f // tf

    def jj(i, j, tv):
        return jnp.where(tv[i] > 0, j, nj - 1)

    grid_spec = pltpu.PrefetchScalarGridSpec(
        num_scalar_prefetch=2,
        grid=(p // tmg, nj),
        in_specs=[pl.BlockSpec((PACK_HALVES, tmg, PACK_W), lambda i, j, te, tv: (0, i, 0)),
                  pl.BlockSpec((None, d, tf), lambda i, j, te, tv: (te[i], 0, jj(i, j, tv))),
                  pl.BlockSpec((None, d, tf), lambda i, j, te, tv: (te[i], 0, nj + jj(i, j, tv))),
                  pl.BlockSpec((None, tf, d), lambda i, j, te, tv: (te[i], jj(i, j, tv), 0))],
        out_specs=pl.BlockSpec((PACK_HALVES, tmg, PACK_W), lambda i, j, te, tv: (0, i, 0)),
        scratch_shapes=[pltpu.VMEM((tmg, d), BF16), pltpu.VMEM((tmg, d), F32)],
    )
    return pl.pallas_call(
        _grouped_kernel,
        grid_spec=grid_spec,
        out_shape=jax.ShapeDtypeStruct((PACK_HALVES, p, PACK_W), jnp.int32),
        compiler_params=_params("arbitrary", "arbitrary"),
        name="grouped_swiglu",
    )(tile_expert, tile_valid, xs, w13, w13, w2)


def _combine_kernel(*refs, final_norm):
    if final_norm:
        x_ref, mw_ref, y1_ref, y2_ref, gf_ref, o_ref = refs
    else:
        x_ref, mw_ref, y1_ref, y2_ref, o_ref = refs
    mw = mw_ref[...]
    out = x_ref[...] + mw[:, 0:1] * _load_packed(y1_ref) + mw[:, 1:2] * _load_packed(y2_ref)
    if final_norm:
        out = _rms(out, gf_ref[...])
    o_ref[...] = out


def _combine(x, mw, y12, g_final, *, tm):
    t, d = x.shape
    row = lambda i: (i, 0)
    in_specs = [pl.BlockSpec((tm, d), row), pl.BlockSpec((tm, LANES), row),
                pl.BlockSpec((PACK_HALVES, tm, PACK_W), lambda i: (0, i, 0)),
                pl.BlockSpec((PACK_HALVES, tm, PACK_W), lambda i: (0, t // tm + i, 0))]
    args = [x, mw, y12, y12]
    if g_final is not None:
        in_specs.append(pl.BlockSpec((1, d), lambda i: (0, 0)))
        args.append(g_final)
    return pl.pallas_call(
        functools.partial(_combine_kernel, final_norm=g_final is not None),
        grid=(t // tm,),
        in_specs=in_specs,
        out_specs=pl.BlockSpec((tm, d), row),
        out_shape=jax.ShapeDtypeStruct((t, d), F32),
        compiler_params=_params("parallel"),
        name="combine",
    )(*args)


def _gather_rows(table, idx):
    halves, n_rows, width = table.shape
    n = halves * idx.shape[0]
    assert n % SC_GATHER_WINDOW == 0
    flat_idx = jnp.concatenate([idx + h * n_rows for h in range(halves)]).reshape(1, n)
    mesh = plsc.VectorSubcoreMesh(core_axis_name="core", subcore_axis_name="subcore")

    @pl.kernel(out_type=jax.ShapeDtypeStruct((n, width), table.dtype), mesh=mesh)
    def gather(table_hbm, idx_hbm, out_hbm):
        def body(idx_vmem, out_vmem):
            pltpu.sync_copy(table_hbm.at[idx_vmem.at[0]], out_vmem)

        pltpu.emit_pipeline(
            body,
            grid=(n // SC_GATHER_WINDOW,),
            in_specs=[pl.BlockSpec((1, SC_GATHER_WINDOW), lambda i: (0, i))],
            out_specs=[pl.BlockSpec((SC_GATHER_WINDOW, width), lambda i: (i, 0))],
            core_axis_name=("core", "subcore"),
            dimension_semantics=(pltpu.PARALLEL,),
        )(idx_hbm, out_hbm)

    return gather(table.reshape(halves * n_rows, width), flat_idx).reshape(halves, idx.shape[0], width)


def _dispatch_rows(table, pos_list, n_out):
    halves, t, width = table.shape
    copies = len(pos_list)
    idx = jnp.concatenate([pos + h * n_out for h in range(halves) for pos in pos_list])
    n = idx.shape[0]
    win_t = t // SC_GATHER_WINDOW
    assert t % SC_GATHER_WINDOW == 0
    mesh = plsc.VectorSubcoreMesh(core_axis_name="core", subcore_axis_name="subcore")

    @pl.kernel(out_type=jax.ShapeDtypeStruct((halves * n_out, width), table.dtype), mesh=mesh)
    def scatter(table_hbm, idx_hbm, out_hbm):
        def body(rows_vmem, idx_vmem):
            pltpu.sync_copy(rows_vmem, out_hbm.at[idx_vmem.at[0]])

        pltpu.emit_pipeline(
            body,
            grid=(n // SC_GATHER_WINDOW,),
            in_specs=[pl.BlockSpec((SC_GATHER_WINDOW, width),
                                   lambda i: ((i // (copies * win_t)) * win_t + i % win_t, 0)),
                      pl.BlockSpec((1, SC_GATHER_WINDOW), lambda i: (0, i))],
            out_specs=[],
            core_axis_name=("core", "subcore"),
            dimension_semantics=(pltpu.PARALLEL,),
        )(table_hbm, idx_hbm)

    return scatter(table.reshape(halves * t, width), idx.reshape(1, n)).reshape(halves, n_out, width)


def _moe_sparse(x, g, router, w13, w2, g_final, *, tm, tmg, tf):
    t, d = x.shape
    hp, mi, mw, cnt = _router(x, g, router, tm=tm)
    counts = cnt[0, :N_EXPERTS].astype(jnp.int32)
    tiles_per_expert = (counts + tmg - 1) // tmg
    tile_end = jnp.cumsum(tiles_per_expert)
    group_start = (tile_end - tiles_per_expert) * tmg
    e1, e2, r1, r2 = (mi[r].astype(jnp.int32) for r in range(4))
    pos1 = group_start[e1] + r1
    pos2 = group_start[e2] + r2
    n_tiles = 2 * t // tmg + N_EXPERTS
    tile = jnp.arange(n_tiles, dtype=jnp.int32)
    tile_expert = jnp.minimum(jnp.sum(tile[:, None] >= tile_end[None, :], axis=1), N_EXPERTS - 1).astype(jnp.int32)
    rows_left = counts[tile_expert] - (tile - (tile_end - tiles_per_expert)[tile_expert]) * tmg
    tile_valid = jnp.where(tile < tile_end[-1], jnp.clip(rows_left, 0, tmg), 0).astype(jnp.int32)

    xs = _dispatch_rows(hp, [pos1, pos2], n_tiles * tmg)
    ys = _grouped_swiglu(tile_expert, tile_valid, xs, w13, w2, tmg=tmg, tf=tf)
    y12 = _gather_rows(ys, jnp.concatenate([pos1, pos2]))
    return _combine(x, mw, y12, g_final, tm=tm)


def _cast_kernel(*refs):
    *w_refs, o_ref = refs
    width = w_refs[0].shape[1]
    for p, w_ref in enumerate(w_refs):
        o_ref[:, p * width:(p + 1) * width] = w_ref[...].astype(o_ref.dtype)


def _cast_bf16(w, *, rows):
    shape = w.shape
    w2 = w.reshape(-1, shape[-1])
    r, c = w2.shape
    assert c % (CAST_PARTS * LANES) == 0
    out = pl.pallas_call(
        _cast_kernel,
        grid=(r // rows,),
        in_specs=[pl.BlockSpec((rows, c // CAST_PARTS), functools.partial(lambda i, p: (i, p), p=p))
                  for p in range(CAST_PARTS)],
        out_specs=pl.BlockSpec((rows, c), lambda i: (i, 0)),
        out_shape=jax.ShapeDtypeStruct((r, c), BF16),
        compiler_params=_params("parallel"),
        name="cast_bf16",
    )(*([w2] * CAST_PARTS))
    return out.reshape(shape)


def _split_w_in(w_in):
    sizes = (512, 512, 1024, GLA_RANK, 1024, 512, 512, 1024, M_HEADS, M_HEADS, 1024, 1024, 1024)
    parts, off = [], 0
    for n in sizes:
        parts.append(w_in[:, off:off + n])
        off += n
    g_q, g_k, g_v, g_lr, g_g, m_q, m_k, m_v, m_i, m_f, m_o, a_g, a_m = parts
    wz = jnp.concatenate([g_q, g_k, g_v, g_g, m_q, m_k, m_v, m_o, a_g, a_m], axis=1).astype(BF16)
    d = w_in.shape[0]
    gate_cols = jnp.stack([m_i, m_f], axis=2).reshape(d, 2 * M_HEADS)
    ws = jnp.concatenate([jnp.pad(g_lr, ((0, 0), (0, LANES - GLA_RANK))),
                          jnp.pad(gate_cols, ((0, 0), (0, LANES - 2 * M_HEADS)))], axis=1).astype(BF16)
    assert wz.shape == (d, Z_COLS) and ws.shape == (d, ZS_COLS)
    return wz, ws


def _pick(total, want):
    t = min(total, want)
    while total % t:
        t -= 1
    return t


def kernel(x, mem, norm_mix, w_in, gla_gk_up, gla_gk_bias, gla_norm, m_conv, m_gate_bias, m_norm, gla_proj, m_proj, w_out, norm_xattn, norm_mem, x_wq, x_wkv, x_wo, norm_ffn, ffn_w13, ffn_w2, moe_router, moe_w13, moe_w2, norm_final):
    b, s, d = x.shape
    m_len = mem.shape[1]
    t = b * s
    depth = norm_mix.shape[0]
    tm = _pick(t, 512)
    row2 = lambda a: a.reshape(1, -1)

    xt = x.reshape(t, d)
    mem_t = mem.reshape(b * m_len, d)
    for l in range(depth):
        last_layer = l == depth - 1
        wz, ws = _split_w_in(w_in[l])
        z, zs = _norm_matmul(xt, row2(norm_mix[l]), wz, ws, tm=tm, tn=2048)
        z3 = z.reshape(b, s, Z_COLS)
        zs3 = zs.reshape(b, s, ZS_COLS)
        up = jnp.pad(gla_gk_up[l], ((0, LANES - GLA_RANK), (0, 0))).astype(BF16)
        o_gla = _gla(z3, zs3, up, row2(gla_gk_bias[l]), row2(gla_norm[l]))
        gbias = jnp.pad(jnp.stack([m_gate_bias[l, :M_HEADS], m_gate_bias[l, M_HEADS:]], axis=1).reshape(1, -1),
                        ((0, 0), (0, LANES - 2 * M_HEADS)))
        arow, rep = _mlstm_gates(zs3, gbias)
        qk3 = _mlstm_conv(z3, m_conv[l], ts=_pick(s, 512))
        o_m = _mlstm(qk3, z3, arow, rep, row2(m_norm[l]))
        xt = _merge(xt, o_gla.reshape(t, d), o_m.reshape(t, d), z,
                    gla_proj[l].astype(BF16), m_proj[l].astype(BF16), w_out[l].astype(BF16), tm=tm)
        kv = _norm_matmul(mem_t, row2(norm_mem[l]), x_wkv[l].astype(BF16), tm=_pick(b * m_len, 512), tn=1024)
        xt = _xattn(xt.reshape(b, s, d), row2(norm_xattn[l]), x_wq[l].astype(BF16),
                    kv.reshape(b, m_len, 2 * d), x_wo[l].astype(BF16), tq=_pick(s, 1024)).reshape(t, d)
        g_final = row2(norm_final) if last_layer else None
        if l % 2 == 0:
            w13, w2 = ffn_w13[l // 2], ffn_w2[l // 2]
            f = w2.shape[0]
            fp = -(-f // MXU_TILE) * MXU_TILE
            w1 = jnp.pad(w13[:, :f], ((0, 0), (0, fp - f))).astype(BF16)
            w3 = jnp.pad(w13[:, f:], ((0, 0), (0, fp - f))).astype(BF16)
            w2p = jnp.pad(w2, ((0, fp - f), (0, 0))).astype(BF16)
            xt = _ffn(xt, row2(norm_ffn[l]), w1, w3, w2p, g_final, tm=tm)
        else:
            router = jnp.pad(moe_router[l // 2], ((0, 0), (0, LANES - N_EXPERTS)))
            w13 = _cast_bf16(moe_w13[l // 2], rows=256)
            w2 = _cast_bf16(moe_w2[l // 2], rows=512)
            xt = _moe_sparse(xt, row2(norm_ffn[l]), router, w13, w2, g_final, tm=tm, tmg=tm, tf=w2.shape[1] // 2)
    return xt.reshape(b, s, d)
```

```python
import functools

import jax
import jax.numpy as jnp
from jax import lax
from jax.experimental import pallas as pl
from jax.experimental.pallas import tpu as pltpu
from jax.experimental.pallas import tpu_sc as plsc

F32 = jnp.float32
BF16 = jnp.bfloat16

EPS = 1e-6
LOG2E = 1.4426950408889634
D_MODEL = 1024
GLA_HEADS = 4
GLA_DK = 128
GLA_DV = 256
GLA_RANK = 16
GLA_GATE_NORM = 16.0
GLA_LOG_DECAY_MIN = -1.0
GLA_CHUNK = 64
GLA_HEADS_PER_STEP = 4
GLA_UNROLL = 8
M_HEADS = 4
M_DK = 128
M_DV = 256
M_CHUNK = 128
M_HEADS_PER_STEP = 2
M_UNROLL = 16
CONV_W = 4
X_HEADS = 4
N_EXPERTS = 8
LANES = 128
MXU_TILE = 256
SUBLANES = 8
VMEM_LIMIT = 48 * 1024 * 1024
SC_GATHER_WINDOW = 128
PACK_HALVES = 2
PACK_W = D_MODEL // 2 // PACK_HALVES
CAST_ROWS = 512

Z_GQ, Z_GK, Z_GV, Z_GG = 0, 512, 1024, 2048
Z_MQ, Z_MK, Z_MV, Z_MO = 3072, 3584, 4096, 5120
Z_AG, Z_AM = 6144, 7168
Z_COLS = 8192
ZS_COLS = 2 * LANES


def _params(*sem):
    return pltpu.CompilerParams(dimension_semantics=sem, vmem_limit_bytes=VMEM_LIMIT)


def _rms(x, g):
    return x * lax.rsqrt(jnp.mean(x * x, axis=-1, keepdims=True) + EPS) * g


def _log_sigmoid(u):
    return jnp.minimum(u, 0.0) - jnp.log(1.0 + jnp.exp(-jnp.abs(u)))


def _sigmoid(u):
    return 1.0 / (1.0 + jnp.exp(-u))


def _dot(a, b):
    return jnp.dot(a, b, preferred_element_type=F32)


def _dot_nt(a, b):
    return lax.dot_general(a, b, (((1,), (1,)), ((), ())), preferred_element_type=F32)


def _dot_tn(a, b):
    return lax.dot_general(a, b, (((0,), (0,)), ((), ())), preferred_element_type=F32)


def _cumsum_rows(x):
    n = x.shape[0]
    row = lax.broadcasted_iota(jnp.int32, x.shape, 0)
    s = 1
    while s < n:
        x = x + jnp.where(row >= s, pltpu.roll(x, s, 0), 0.0)
        s *= 2
    return x


def _norm_matmul_kernel(*refs, tn, with_small):
    if with_small:
        x_ref, g_ref, w_ref, ws_ref, z_ref, zs_ref = refs
    else:
        x_ref, g_ref, w_ref, z_ref = refs
    h = _rms(x_ref[...], g_ref[...]).astype(BF16)
    if with_small:
        zs_ref[...] = _dot(h, ws_ref[...])
    for c0 in range(0, z_ref.shape[1], tn):
        z_ref[:, c0:c0 + tn] = _dot(h, w_ref[:, c0:c0 + tn]).astype(z_ref.dtype)


def _norm_matmul(x, g, w, ws=None, *, tm, tn):
    t, d = x.shape
    n = w.shape[1]
    resident = pl.Buffered(1)
    in_specs = [pl.BlockSpec((tm, d), lambda i: (i, 0)),
                pl.BlockSpec((1, d), lambda i: (0, 0)),
                pl.BlockSpec((d, n), lambda i: (0, 0), pipeline_mode=resident)]
    out_specs = [pl.BlockSpec((tm, n), lambda i: (i, 0))]
    out_shape = [jax.ShapeDtypeStruct((t, n), BF16)]
    args = [x, g, w]
    if ws is not None:
        ns = ws.shape[1]
        in_specs.append(pl.BlockSpec((d, ns), lambda i: (0, 0), pipeline_mode=resident))
        out_specs.append(pl.BlockSpec((tm, ns), lambda i: (i, 0)))
        out_shape.append(jax.ShapeDtypeStruct((t, ns), F32))
        args.append(ws)
    out = pl.pallas_call(
        functools.partial(_norm_matmul_kernel, tn=tn, with_small=ws is not None),
        grid=(t // tm,),
        in_specs=in_specs, out_specs=out_specs, out_shape=out_shape,
        compiler_params=_params("parallel"),
        name="norm_matmul",
    )(*args)
    return out if ws is not None else out[0]


def _gla_kernel(q_ref, k_ref, v_ref, gg_ref, lr_ref, up_ref, gb_ref, nw_ref, o_ref, la_scr, st_scr):
    seq = q_ref.shape[0]
    c = GLA_CHUNK
    scale = GLA_DK ** -0.5

    u = _dot(lr_ref[...].astype(BF16), up_ref[...]) + gb_ref[...]
    la_scr[...] = jnp.maximum(_log_sigmoid(u) * (1.0 / GLA_GATE_NORM), GLA_LOG_DECAY_MIN)
    st_scr[...] = jnp.zeros_like(st_scr)
    causal = (lax.broadcasted_iota(jnp.int32, (c, c), 0) >= lax.broadcasted_iota(jnp.int32, (c, c), 1))

    def body(n, carry):
        r0 = pl.multiple_of(n * c, c)
        rows = pl.ds(r0, c)
        for hh in range(GLA_HEADS_PER_STEP):
            kcols = slice(hh * GLA_DK, (hh + 1) * GLA_DK)
            vcols = slice(hh * GLA_DV, (hh + 1) * GLA_DV)
            cum = _cumsum_rows(la_scr[rows, kcols])
            cum_last = cum[c - 1:c, :]
            q = q_ref[rows, kcols].astype(F32) * scale
            k = k_ref[rows, kcols].astype(F32)
            v = v_ref[rows, vcols]
            q_dec = (q * jnp.exp(cum)).astype(BF16)
            k_inv = (k * jnp.exp(-cum)).astype(BF16)
            k_end = (k * jnp.exp(cum_last - cum)).astype(BF16)
            scores = jnp.where(causal, _dot_nt(q_dec, k_inv), 0.0).astype(BF16)
            st = st_scr[hh]
            o = _dot(scores, v) + _dot_nt(q_dec, st.astype(BF16))
            st_scr[hh] = st * jnp.exp(cum_last) + _dot_tn(v, k_end)
            gate = gg_ref[rows, vcols].astype(F32)
            o_ref[rows, vcols] = (_rms(o, nw_ref[:, vcols]) * (gate * _sigmoid(gate))).astype(o_ref.dtype)
        return carry

    lax.fori_loop(0, seq // c, body, 0, unroll=GLA_UNROLL)


def _gla(z3, zs3, up, gb, nw):
    b, s, _ = z3.shape
    hps = GLA_HEADS_PER_STEP
    dk, dv = hps * GLA_DK, hps * GLA_DV
    qb, kb, vb, gb_ = Z_GQ // dk, Z_GK // dk, Z_GV // dv, Z_GG // dv
    return pl.pallas_call(
        _gla_kernel,
        grid=(b, GLA_HEADS // hps),
        in_specs=[pl.BlockSpec((None, s, dk), lambda i, h: (i, 0, qb + h)),
                  pl.BlockSpec((None, s, dk), lambda i, h: (i, 0, kb + h)),
                  pl.BlockSpec((None, s, dv), lambda i, h: (i, 0, vb + h)),
                  pl.BlockSpec((None, s, dv), lambda i, h: (i, 0, gb_ + h)),
                  pl.BlockSpec((None, s, LANES), lambda i, h: (i, 0, 0)),
                  pl.BlockSpec((LANES, dk), lambda i, h: (0, h)),
                  pl.BlockSpec((1, dk), lambda i, h: (0, h)),
                  pl.BlockSpec((1, dv), lambda i, h: (0, h))],
        out_specs=pl.BlockSpec((None, s, dv), lambda i, h: (i, 0, h)),
        out_shape=jax.ShapeDtypeStruct((b, s, GLA_HEADS * GLA_DV), BF16),
        scratch_shapes=[pltpu.VMEM((s, dk), F32), pltpu.VMEM((hps, GLA_DV, GLA_DK), F32)],
        compiler_params=_params("parallel", "arbitrary"),
        name="gla",
    )(z3, z3, z3, z3, zs3, up, gb, nw)


def _chunk_scan_lanes(x, combine, identity, chunk):
    lane_in_chunk = lax.broadcasted_iota(jnp.int32, x.shape, 1) % chunk
    step = 1
    while step < chunk:
        x = combine(x, jnp.where(lane_in_chunk >= step, pltpu.roll(x, step, 1), identity))
        step *= 2
    return x


def _replicate_rows(x, row):
    hi = x.astype(BF16)
    r1 = x - hi.astype(F32)
    mid = r1.astype(BF16)
    lo = (r1 - mid.astype(F32)).astype(BF16)
    parts = jnp.concatenate([hi, mid, lo, jnp.zeros_like(hi)], axis=0)
    pick = lax.broadcasted_iota(jnp.int32, (4 * SUBLANES, LANES), 0) % SUBLANES == row
    return _dot_tn(parts, jnp.where(pick, 1.0, 0.0).astype(BF16))


def _mlstm_gates_kernel(gt_ref, gb_ref, arow_ref, rep_ref):
    c = M_CHUNK
    gates_t = (gt_ref[...] + gb_ref[...]).T
    top = gates_t[0:SUBLANES, :]
    odd = lax.broadcasted_iota(jnp.int32, top.shape, 0) % 2 == 1
    lf = jnp.where(odd, _log_sigmoid(top), 0.0)
    b_odd = _chunk_scan_lanes(lf, jnp.add, 0.0, c)
    a_even = top - pltpu.roll(b_odd, SUBLANES - 1, 0)
    amax_even = _chunk_scan_lanes(jnp.where(odd, -jnp.inf, a_even), jnp.maximum, -jnp.inf, c)
    for h in range(M_HEADS):
        arow_ref[h] = pltpu.roll(a_even, (SUBLANES - 2 * h) % SUBLANES, 0)
        rep_ref[h, 0] = _replicate_rows(b_odd, 2 * h + 1)
        rep_ref[h, 1] = _replicate_rows(a_even, 2 * h)
        rep_ref[h, 2] = _replicate_rows(jnp.where(odd, 0.0, amax_even), 2 * h)


def _mlstm_gates(zs3, gbias):
    b, s, _ = zs3.shape
    return pl.pallas_call(
        _mlstm_gates_kernel,
        grid=(b,),
        in_specs=[pl.BlockSpec((None, s, LANES), lambda i: (i, 0, 1)),
                  pl.BlockSpec((1, LANES), lambda i: (0, 0))],
        out_specs=[pl.BlockSpec((None, M_HEADS, SUBLANES, s), lambda i: (i, 0, 0, 0)),
                   pl.BlockSpec((None, M_HEADS, 3, s, LANES), lambda i: (i, 0, 0, 0, 0))],
        out_shape=[jax.ShapeDtypeStruct((b, M_HEADS, SUBLANES, s), F32),
                   jax.ShapeDtypeStruct((b, M_HEADS, 3, s, LANES), F32)],
        compiler_params=_params("parallel"),
        name="mlstm_gates",
    )(zs3, gbias)


def _mlstm_conv_kernel(cur_ref, prev_ref, w_ref, shift_ref, o_ref):
    c = M_CHUNK
    d = cur_ref.shape[1]
    prev = jnp.where(pl.program_id(1) > 0, prev_ref[...], jnp.zeros_like(prev_ref))
    lane = lax.broadcasted_iota(jnp.int32, (1, d), 1)
    col_scale = jnp.where(lane < d // 2, M_DK ** -0.5, 1.0)
    for i in range(cur_ref.shape[0] // c):
        cur = cur_ref[i * c:(i + 1) * c, :]
        before = prev if i == 0 else cur_ref[(i - 1) * c:i * c, :]
        shifted = _dot(shift_ref[...], jnp.concatenate([before, cur], axis=0))
        acc = w_ref[CONV_W - 1:CONV_W, :] * cur.astype(F32)
        for s in range(1, CONV_W):
            acc = acc + w_ref[CONV_W - 1 - s:CONV_W - s, :] * shifted[(s - 1) * c:s * c, :]
        o_ref[i * c:(i + 1) * c, :] = (acc * _sigmoid(acc) * col_scale).astype(o_ref.dtype)


def _mlstm_conv(z3, conv, *, ts):
    b, s, _ = z3.shape
    c = M_CHUNK
    d = conv.shape[1]
    blk = Z_MQ // d
    assert Z_MQ % d == 0 and Z_MK == Z_MQ + d // 2
    t_idx = jnp.arange(c)[:, None]
    col = jnp.arange(2 * c)[None, :]
    shift = jnp.concatenate([(col == c + t_idx - s) for s in range(1, CONV_W)], axis=0).astype(BF16)
    return pl.pallas_call(
        _mlstm_conv_kernel,
        grid=(b, s // ts),
        in_specs=[pl.BlockSpec((None, ts, d), lambda i, j: (i, j, blk)),
                  pl.BlockSpec((None, c, d), lambda i, j: (i, jnp.maximum(j * (ts // c) - 1, 0), blk)),
                  pl.BlockSpec((CONV_W, d), lambda i, j: (0, 0)),
                  pl.BlockSpec(((CONV_W - 1) * c, 2 * c), lambda i, j: (0, 0))],
        out_specs=pl.BlockSpec((None, ts, d), lambda i, j: (i, j, 0)),
        out_shape=jax.ShapeDtypeStruct((b, s, d), BF16),
        compiler_params=_params("parallel", "parallel"),
        name="mlstm_conv",
    )(z3, z3, conv, shift)


def _mlstm_kernel(q_ref, k_ref, v_ref, og_ref, arow_ref, rep_ref, nw_ref, o_ref, st_scr, m_scr):
    seq = q_ref.shape[0]
    c = M_CHUNK
    st_scr[...] = jnp.zeros_like(st_scr)
    m_scr[...] = jnp.zeros_like(m_scr)
    causal = (lax.broadcasted_iota(jnp.int32, (c, c), 0) >= lax.broadcasted_iota(jnp.int32, (c, c), 1))
    ones_cols = jnp.ones((c, LANES), BF16)
    ones_dv = jnp.ones((M_DV, LANES), BF16)

    def wide(x, n):
        return jnp.concatenate([x] * n, axis=1)

    def body(n, carry):
        r0 = pl.multiple_of(n * c, c)
        rows = pl.ds(r0, c)
        for hh in range(M_HEADS_PER_STEP):
            kcols = slice(hh * M_DK, (hh + 1) * M_DK)
            vcols = slice(hh * M_DV, (hh + 1) * M_DV)
            qb = q_ref[rows, kcols]
            kb = k_ref[rows, kcols]
            v_ext = jnp.concatenate([v_ref[rows, vcols], ones_cols], axis=1)
            b = rep_ref[hh, 0, rows, :]
            a = rep_ref[hh, 1, rows, :]
            amax = rep_ref[hh, 2, rows, :]
            a_row = arow_ref[hh, 0:1, rows]
            m_prev = m_scr[hh]
            st = st_scr[hh]

            dmat = jnp.where(causal, b + a_row, -jnp.inf)
            m_inter = b + m_prev
            m_t = jnp.maximum(m_inter, b + amax)
            w_ts = jnp.exp(dmat - m_t) * _dot_nt(qb, kb)
            s_inter = jnp.exp(m_inter - m_t)
            nd = _dot(w_ts.astype(BF16), v_ext) + wide(s_inter, 3) * _dot(qb, st.astype(BF16))
            inv = 1.0 / jnp.maximum(jnp.abs(nd[:, M_DV:]), jnp.exp(-m_t))
            h = nd[:, :M_DV] * wide(inv, 2)

            b_last = b[c - 1:c, :]
            m_new = jnp.maximum(b_last + m_prev, b_last + amax[c - 1:c, :])
            kw = kb.astype(F32) * jnp.exp(b_last + a - m_new)
            st_scr[hh] = wide(jnp.exp(b_last + m_prev - m_new), 3) * st + _dot_tn(kw.astype(BF16), v_ext)
            m_scr[hh] = m_new

            mean_sq = _dot((h * h).astype(BF16), ones_dv) * (1.0 / M_DV)
            gate = og_ref[rows, vcols].astype(F32)
            y = h * wide(lax.rsqrt(mean_sq + EPS), 2) * nw_ref[:, vcols]
            o_ref[rows, vcols] = (y * _sigmoid(gate)).astype(o_ref.dtype)
        return carry

    lax.fori_loop(0, seq // c, body, 0, unroll=M_UNROLL)


def _mlstm(qk3, z3, arow, rep, nw):
    b, s, _ = z3.shape
    hps = M_HEADS_PER_STEP
    dk, dv = hps * M_DK, hps * M_DV
    vb, ob = Z_MV // dv, Z_MO // dv
    return pl.pallas_call(
        _mlstm_kernel,
        grid=(b, M_HEADS // hps),
        in_specs=[pl.BlockSpec((None, s, dk), lambda i, h: (i, 0, h)),
                  pl.BlockSpec((None, s, dk), lambda i, h: (i, 0, M_HEADS // hps + h)),
                  pl.BlockSpec((None, s, dv), lambda i, h: (i, 0, vb + h)),
                  pl.BlockSpec((None, s, dv), lambda i, h: (i, 0, ob + h)),
                  pl.BlockSpec((None, hps, SUBLANES, s), lambda i, h: (i, h, 0, 0)),
                  pl.BlockSpec((None, hps, 3, s, LANES), lambda i, h: (i, h, 0, 0, 0)),
                  pl.BlockSpec((1, dv), lambda i, h: (0, h))],
        out_specs=pl.BlockSpec((None, s, dv), lambda i, h: (i, 0, h)),
        out_shape=jax.ShapeDtypeStruct((b, s, M_HEADS * M_DV), BF16),
        scratch_shapes=[pltpu.VMEM((hps, M_DK, M_DV + LANES), F32), pltpu.VMEM((hps, 1, LANES), F32)],
        compiler_params=_params("parallel", "arbitrary"),
        name="mlstm",
    )(qk3, qk3, z3, z3, arow, rep, nw)


def _merge_kernel(x_ref, og_ref, om_ref, ag_ref, am_ref, wg_ref, wm_ref, wo_ref, o_ref):
    merged = (_sigmoid(ag_ref[...].astype(F32)) * _dot(og_ref[...], wg_ref[...])
              + _sigmoid(am_ref[...].astype(F32)) * _dot(om_ref[...], wm_ref[...]))
    o_ref[...] = x_ref[...] + _dot(merged.astype(BF16), wo_ref[...])


def _merge(x, o_gla, o_m, z, wg, wm, wo, *, tm):
    t, d = x.shape
    row = lambda i: (i, 0)
    full = lambda i: (0, 0)
    return pl.pallas_call(
        _merge_kernel,
        grid=(t // tm,),
        in_specs=[pl.BlockSpec((tm, d), row), pl.BlockSpec((tm, d), row), pl.BlockSpec((tm, d), row),
                  pl.BlockSpec((tm, d), lambda i: (i, Z_AG // D_MODEL)),
                  pl.BlockSpec((tm, d), lambda i: (i, Z_AM // D_MODEL)),
                  pl.BlockSpec((d, d), full), pl.BlockSpec((d, d), full), pl.BlockSpec((d, d), full)],
        out_specs=pl.BlockSpec((tm, d), row),
        out_shape=jax.ShapeDtypeStruct((t, d), F32),
        compiler_params=_params("parallel"),
        name="merge",
    )(x, o_gla, o_m, z, z, wg, wm, wo)


def _xattn_kernel(x_ref, g_ref, wq_ref, kv_ref, wo_ref, o_ref):
    d = x_ref.shape[1]
    dh = d // X_HEADS
    x = x_ref[...]
    q = _dot(_rms(x, g_ref[...]).astype(BF16), wq_ref[...]).astype(BF16)
    heads = []
    ones_m = jnp.ones((kv_ref.shape[0], LANES), BF16)
    for h in range(X_HEADS):
        k = kv_ref[:, h * dh:(h + 1) * dh]
        v = kv_ref[:, d + h * dh:d + (h + 1) * dh]
        s = _dot_nt(q[:, h * dh:(h + 1) * dh], k)
        p = jnp.exp2((s - jnp.max(s, axis=-1, keepdims=True)) * (dh ** -0.5 * LOG2E)).astype(BF16)
        inv_l = 1.0 / _dot(p, ones_m)
        heads.append((_dot(p, v) * jnp.concatenate([inv_l] * (dh // LANES), axis=1)).astype(BF16))
    o_ref[...] = x + _dot(jnp.concatenate(heads, axis=-1), wo_ref[...])


def _xattn(x3, g, wq, kv, wo, *, tq):
    b, s, d = x3.shape
    m = kv.shape[1]
    return pl.pallas_call(
        _xattn_kernel,
        grid=(b, s // tq),
        in_specs=[pl.BlockSpec((None, tq, d), lambda i, j: (i, j, 0)),
                  pl.BlockSpec((1, d), lambda i, j: (0, 0)),
                  pl.BlockSpec((d, d), lambda i, j: (0, 0)),
                  pl.BlockSpec((None, m, 2 * d), lambda i, j: (i, 0, 0)),
                  pl.BlockSpec((d, d), lambda i, j: (0, 0))],
        out_specs=pl.BlockSpec((None, tq, d), lambda i, j: (i, j, 0)),
        out_shape=jax.ShapeDtypeStruct((b, s, d), F32),
        compiler_params=_params("parallel", "parallel"),
        name="xattn",
    )(x3, g, wq, kv, wo)


def _ffn_kernel(*refs, chunks, final_norm):
    if final_norm:
        x_ref, g_ref, w1_ref, w3_ref, w2_ref, gf_ref, o_ref = refs
    else:
        x_ref, g_ref, w1_ref, w3_ref, w2_ref, o_ref = refs
    x = x_ref[...]
    h = _rms(x, g_ref[...]).astype(BF16)
    out = x
    for c0, c1 in chunks:
        a = _dot(h, w1_ref[:, c0:c1])
        act = (a * _sigmoid(a) * _dot(h, w3_ref[:, c0:c1])).astype(BF16)
        out = out + _dot(act, w2_ref[c0:c1, :])
    if final_norm:
        out = _rms(out, gf_ref[...])
    o_ref[...] = out


def _ffn(x, g, w1, w3, w2, g_final, *, tm):
    t, d = x.shape
    f = w2.shape[0]
    assert f % MXU_TILE == 0
    half = (f // MXU_TILE + 1) // 2 * MXU_TILE
    chunks = ((0, half), (half, f))
    resident = pl.Buffered(1)
    in_specs = [pl.BlockSpec((tm, d), lambda i: (i, 0)),
                pl.BlockSpec((1, d), lambda i: (0, 0)),
                pl.BlockSpec((d, f), lambda i: (0, 0), pipeline_mode=resident),
                pl.BlockSpec((d, f), lambda i: (0, 0), pipeline_mode=resident),
                pl.BlockSpec((f, d), lambda i: (0, 0), pipeline_mode=resident)]
    args = [x, g, w1, w3, w2]
    if g_final is not None:
        in_specs.append(pl.BlockSpec((1, d), lambda i: (0, 0)))
        args.append(g_final)
    return pl.pallas_call(
        functools.partial(_ffn_kernel, chunks=chunks, final_norm=g_final is not None),
        grid=(t // tm,),
        in_specs=in_specs,
        out_specs=pl.BlockSpec((tm, d), lambda i: (i, 0)),
        out_shape=jax.ShapeDtypeStruct((t, d), F32),
        compiler_params=_params("parallel"),
        name="dense_ffn",
    )(*args)


def _store_packed(ref, y):
    half = y.shape[1] // 2
    bits = lax.bitcast_convert_type(y.astype(BF16).astype(F32), jnp.int32)
    lo = lax.shift_right_logical(bits[:, :half], 16)
    hi = jnp.bitwise_and(bits[:, half:], jnp.int32(-65536))
    packed = jnp.bitwise_or(hi, lo)
    w = half // PACK_HALVES
    for p in range(PACK_HALVES):
        ref[p] = packed[:, p * w:(p + 1) * w]


def _load_packed(ref):
    packed = jnp.concatenate([ref[p] for p in range(PACK_HALVES)], axis=1)
    lo = lax.bitcast_convert_type(lax.shift_left(packed, 16), F32)
    hi = lax.bitcast_convert_type(jnp.bitwise_and(packed, jnp.int32(-65536)), F32)
    return jnp.concatenate([lo, hi], axis=1)


def _router_kernel(x_ref, g_ref, r_ref, hp_ref, mi_ref, mw_ref, cnt_ref, carry_scr):
    @pl.when(pl.program_id(0) == 0)
    def _():
        carry_scr[...] = jnp.zeros_like(carry_scr)

    rows = x_ref.shape[0]
    h = _rms(x_ref[...], g_ref[...])
    _store_packed(hp_ref, h)
    r = r_ref[...]
    h_hi = h.astype(BF16)
    h_lo = (h - h_hi.astype(F32)).astype(BF16)
    r_hi = r.astype(BF16)
    r_lo = (r - r_hi.astype(F32)).astype(BF16)
    hi_terms = _dot(h_hi, jnp.concatenate([r_hi, r_lo], axis=1))
    logits = hi_terms[:, :LANES] + (hi_terms[:, LANES:] + _dot(h_lo, r_hi))
    lane = lax.broadcasted_iota(jnp.int32, logits.shape, 1)
    logits = jnp.where(lane < N_EXPERTS, logits, -jnp.inf)
    m1 = jnp.max(logits, axis=-1, keepdims=True)
    i1 = jnp.min(jnp.where(logits == m1, lane, LANES), axis=-1, keepdims=True)
    rest = jnp.where(lane == i1, -jnp.inf, logits)
    m2 = jnp.max(rest, axis=-1, keepdims=True)
    i2 = jnp.min(jnp.where(rest == m2, lane, LANES), axis=-1, keepdims=True)
    e2 = jnp.exp(m2 - m1)
    w1 = 1.0 / (1.0 + e2)
    w2 = e2 / (1.0 + e2)
    oh1 = (lane == i1).astype(F32)
    oh2 = (lane == i2).astype(F32)
    oh = oh1 + oh2
    earlier = (lax.broadcasted_iota(jnp.int32, (rows, rows), 0) > lax.broadcasted_iota(jnp.int32, (rows, rows), 1))
    before = _dot(earlier.astype(BF16), oh.astype(BF16)) + carry_scr[...]
    rank1 = jnp.sum(oh1 * before, axis=-1, keepdims=True)
    rank2 = jnp.sum(oh2 * before, axis=-1, keepdims=True)
    carry_scr[...] += jnp.sum(oh, axis=0, keepdims=True)
    cnt_ref[...] = carry_scr[...]
    meta = jnp.where(lane == 0, i1.astype(F32), jnp.where(lane == 1, i2.astype(F32), jnp.where(lane == 2, rank1,
                     jnp.where(lane == 3, rank2, 0.0))))
    mi_ref[...] = meta.T[0:SUBLANES, :]
    mw_ref[...] = jnp.where(lane == 0, w1, jnp.where(lane == 1, w2, 0.0))


def _router(x, g, router, *, tm):
    t, d = x.shape
    row = lambda i: (i, 0)
    fix = lambda i: (0, 0)
    return pl.pallas_call(
        _router_kernel,
        grid=(t // tm,),
        in_specs=[pl.BlockSpec((tm, d), row), pl.BlockSpec((1, d), fix), pl.BlockSpec((d, LANES), fix)],
        out_specs=[pl.BlockSpec((PACK_HALVES, tm, PACK_W), lambda i: (0, i, 0)),
                   pl.BlockSpec((SUBLANES, tm), lambda i: (0, i)),
                   pl.BlockSpec((tm, LANES), row), pl.BlockSpec((1, LANES), fix)],
        out_shape=[jax.ShapeDtypeStruct((PACK_HALVES, t, PACK_W), jnp.int32),
                   jax.ShapeDtypeStruct((SUBLANES, t), F32),
                   jax.ShapeDtypeStruct((t, LANES), F32), jax.ShapeDtypeStruct((1, LANES), F32)],
        scratch_shapes=[pltpu.VMEM((1, LANES), F32)],
        compiler_params=_params("arbitrary"),
        name="router",
    )(x, g, router)


def _grouped_kernel(te_ref, tv_ref, xs_ref, w1_ref, w3_ref, w2_ref, ys_ref, h_scr, acc_scr):
    i = pl.program_id(0)
    j = pl.program_id(1)
    last = pl.num_programs(1) - 1
    valid = tv_ref[i]
    active = valid > 0

    @pl.when(jnp.logical_and(active, j == 0))
    def _():
        h = _load_packed(xs_ref)
        row = lax.broadcasted_iota(jnp.int32, h.shape, 0)
        h_scr[...] = jnp.where(row < valid, h, 0.0).astype(BF16)
        acc_scr[...] = jnp.zeros_like(acc_scr)

    @pl.when(active)
    def _():
        h = h_scr[...]
        a = _dot(h, w1_ref[...])
        act = (a * _sigmoid(a) * _dot(h, w3_ref[...])).astype(BF16)
        acc_scr[...] += _dot(act, w2_ref[...])

    @pl.when(jnp.logical_and(active, j == last))
    def _():
        _store_packed(ys_ref, acc_scr[...])

    @pl.when(jnp.logical_and(jnp.logical_not(active), j == last))
    def _():
        ys_ref[...] = jnp.zeros_like(ys_ref)


def _grouped_swiglu(tile_expert, tile_valid, xs, w13, w2, *, tmg, tf):
    _, p, _ = xs.shape
    d = w2.shape[2]
    f = w2.shape[1]
    nj = f // tf

    def jj(i, j, tv):
        return jnp.where(tv[i] > 0, j, nj - 1)

    grid_spec = pltpu.PrefetchScalarGridSpec(
        num_scalar_prefetch=2,
        grid=(p // tmg, nj),
        in_specs=[pl.BlockSpec((PACK_HALVES, tmg, PACK_W), lambda i, j, te, tv: (0, i, 0)),
                  pl.BlockSpec((None, d, tf), lambda i, j, te, tv: (te[i], 0, jj(i, j, tv))),
                  pl.BlockSpec((None, d, tf), lambda i, j, te, tv: (te[i], 0, nj + jj(i, j, tv))),
                  pl.BlockSpec((None, tf, d), lambda i, j, te, tv: (te[i], jj(i, j, tv), 0))],
        out_specs=pl.BlockSpec((PACK_HALVES, tmg, PACK_W), lambda i, j, te, tv: (0, i, 0)),
        scratch_shapes=[pltpu.VMEM((tmg, d), BF16), pltpu.VMEM((tmg, d), F32)],
    )
    return pl.pallas_call(
        _grouped_kernel,
        grid_spec=grid_spec,
        out_shape=jax.ShapeDtypeStruct((PACK_HALVES, p, PACK_W), jnp.int32),
        compiler_params=_params("arbitrary", "arbitrary"),
        name="grouped_swiglu",
    )(tile_expert, tile_valid, xs, w13, w13, w2)


def _combine_kernel(*refs, final_norm):
    if final_norm:
        x_ref, mw_ref, y1_ref, y2_ref, gf_ref, o_ref = refs
    else:
        x_ref, mw_ref, y1_ref, y2_ref, o_ref = refs
    mw = mw_ref[...]
    out = x_ref[...] + mw[:, 0:1] * _load_packed(y1_ref) + mw[:, 1:2] * _load_packed(y2_ref)
    if final_norm:
        out = _rms(out, gf_ref[...])
    o_ref[...] = out


def _combine(x, mw, y12, g_final, *, tm):
    t, d = x.shape
    row = lambda i: (i, 0)
    in_specs = [pl.BlockSpec((tm, d), row), pl.BlockSpec((tm, LANES), row),
                pl.BlockSpec((PACK_HALVES, tm, PACK_W), lambda i: (0, i, 0)),
                pl.BlockSpec((PACK_HALVES, tm, PACK_W), lambda i: (0, t // tm + i, 0))]
    args = [x, mw, y12, y12]
    if g_final is not None:
        in_specs.append(pl.BlockSpec((1, d), lambda i: (0, 0)))
        args.append(g_final)
    return pl.pallas_call(
        functools.partial(_combine_kernel, final_norm=g_final is not None),
        grid=(t // tm,),
        in_specs=in_specs,
        out_specs=pl.BlockSpec((tm, d), row),
        out_shape=jax.ShapeDtypeStruct((t, d), F32),
        compiler_params=_params("parallel"),
        name="combine",
    )(*args)


def _gather_rows(table, idx):
    halves, n_rows, width = table.shape
    n = halves * idx.shape[0]
    assert n % SC_GATHER_WINDOW == 0
    flat_idx = jnp.concatenate([idx + h * n_rows for h in range(halves)]).reshape(1, n)
    mesh = plsc.VectorSubcoreMesh(core_axis_name="core", subcore_axis_name="subcore")

    @pl.kernel(out_type=jax.ShapeDtypeStruct((n, width), table.dtype), mesh=mesh)
    def gather(table_hbm, idx_hbm, out_hbm):
        def body(idx_vmem, out_vmem):
            pltpu.sync_copy(table_hbm.at[idx_vmem.at[0]], out_vmem)

        pltpu.emit_pipeline(
            body,
            grid=(n // SC_GATHER_WINDOW,),
            in_specs=[pl.BlockSpec((1, SC_GATHER_WINDOW), lambda i: (0, i))],
            out_specs=[pl.BlockSpec((SC_GATHER_WINDOW, width), lambda i: (i, 0))],
            core_axis_name=("core", "subcore"),
            dimension_semantics=(pltpu.PARALLEL,),
        )(idx_hbm, out_hbm)

    return gather(table.reshape(halves * n_rows, width), flat_idx).reshape(halves, idx.shape[0], width)


def _dispatch_rows(table, pos_list, n_out):
    halves, t, width = table.shape
    copies = len(pos_list)
    idx = jnp.concatenate([pos + h * n_out for h in range(halves) for pos in pos_list])
    n = idx.shape[0]
    win_t = t // SC_GATHER_WINDOW
    assert t % SC_GATHER_WINDOW == 0
    mesh = plsc.VectorSubcoreMesh(core_axis_name="core", subcore_axis_name="subcore")

    @pl.kernel(out_type=jax.ShapeDtypeStruct((halves * n_out, width), table.dtype), mesh=mesh)
    def scatter(table_hbm, idx_hbm, out_hbm):
        def body(rows_vmem, idx_vmem):
            pltpu.sync_copy(rows_vmem, out_hbm.at[idx_vmem.at[0]])

        pltpu.emit_pipeline(
            body,
            grid=(n // SC_GATHER_WINDOW,),
            in_specs=[pl.BlockSpec((SC_GATHER_WINDOW, width),
                                   lambda i: ((i // (copies * win_t)) * win_t + i % win_t, 0)),
                      pl.BlockSpec((1, SC_GATHER_WINDOW), lambda i: (0, i))],
            out_specs=[],
            core_axis_name=("core", "subcore"),
            dimension_semantics=(pltpu.PARALLEL,),
        )(table_hbm, idx_hbm)

    return scatter(table.reshape(halves * t, width), idx.reshape(1, n)).reshape(halves, n_out, width)


def _moe_sparse(x, g, router, w13, w2, g_final, *, tm, tmg, tf):
    t, d = x.shape
    hp, mi, mw, cnt = _router(x, g, router, tm=tm)
    counts = cnt[0, :N_EXPERTS].astype(jnp.int32)
    tiles_per_expert = (counts + tmg - 1) // tmg
    tile_end = jnp.cumsum(tiles_per_expert)
    group_start = (tile_end - tiles_per_expert) * tmg
    e1, e2, r1, r2 = (mi[r].astype(jnp.int32) for r in range(4))
    pos1 = group_start[e1] + r1
    pos2 = group_start[e2] + r2
    n_tiles = 2 * t // tmg + N_EXPERTS
    tile = jnp.arange(n_tiles, dtype=jnp.int32)
    tile_expert = jnp.minimum(jnp.sum(tile[:, None] >= tile_end[None, :], axis=1), N_EXPERTS - 1).astype(jnp.int32)
    rows_left = counts[tile_expert] - (tile - (tile_end - tiles_per_expert)[tile_expert]) * tmg
    tile_valid = jnp.where(tile < tile_end[-1], jnp.clip(rows_left, 0, tmg), 0).astype(jnp.int32)

    xs = _dispatch_rows(hp, [pos1, pos2], n_tiles * tmg)
    ys = _grouped_swiglu(tile_expert, tile_valid, xs, w13, w2, tmg=tmg, tf=tf)
    y12 = _gather_rows(ys, jnp.concatenate([pos1, pos2]))
    return _combine(x, mw, y12, g_final, tm=tm)


def _cast_kernel(w_ref, o_ref):
    o_ref[...] = w_ref[...].astype(o_ref.dtype)


def _cast_bf16(w, *, rows, cols):
    shape = w.shape
    w2 = w.reshape(-1, shape[-1])
    r, c = w2.shape
    out = pl.pallas_call(
        _cast_kernel,
        grid=(r // rows, c // cols),
        in_specs=[pl.BlockSpec((rows, cols), lambda i, j: (i, j))],
        out_specs=pl.BlockSpec((rows, cols), lambda i, j: (i, j)),
        out_shape=jax.ShapeDtypeStruct((r, c), BF16),
        compiler_params=_params("parallel", "parallel"),
        name="cast_bf16",
    )(w2)
    return out.reshape(shape)


def _split_w_in(w_in):
    sizes = (512, 512, 1024, GLA_RANK, 1024, 512, 512, 1024, M_HEADS, M_HEADS, 1024, 1024, 1024)
    parts, off = [], 0
    for n in sizes:
        parts.append(w_in[:, off:off + n])
        off += n
    g_q, g_k, g_v, g_lr, g_g, m_q, m_k, m_v, m_i, m_f, m_o, a_g, a_m = parts
    wz = jnp.concatenate([g_q, g_k, g_v, g_g, m_q, m_k, m_v, m_o, a_g, a_m], axis=1).astype(BF16)
    d = w_in.shape[0]
    gate_cols = jnp.stack([m_i, m_f], axis=2).reshape(d, 2 * M_HEADS)
    ws = jnp.concatenate([jnp.pad(g_lr, ((0, 0), (0, LANES - GLA_RANK))),
                          jnp.pad(gate_cols, ((0, 0), (0, LANES - 2 * M_HEADS)))], axis=1).astype(BF16)
    assert wz.shape == (d, Z_COLS) and ws.shape == (d, ZS_COLS)
    return wz, ws


def _pick(total, want):
    t = min(total, want)
    while total % t:
        t -= 1
    return t


def kernel(x, mem, norm_mix, w_in, gla_gk_up, gla_gk_bias, gla_norm, m_conv, m_gate_bias, m_norm, gla_proj, m_proj, w_out, norm_xattn, norm_mem, x_wq, x_wkv, x_wo, norm_ffn, ffn_w13, ffn_w2, moe_router, moe_w13, moe_w2, norm_final):
    b, s, d = x.shape
    m_len = mem.shape[1]
    t = b * s
    depth = norm_mix.shape[0]
    tm = _pick(t, 512)
    row2 = lambda a: a.reshape(1, -1)

    xt = x.reshape(t, d)
    mem_t = mem.reshape(b * m_len, d)
    for l in range(depth):
        last_layer = l == depth - 1
        wz, ws = _split_w_in(w_in[l])
        z, zs = _norm_matmul(xt, row2(norm_mix[l]), wz, ws, tm=tm, tn=2048)
        z3 = z.reshape(b, s, Z_COLS)
        zs3 = zs.reshape(b, s, ZS_COLS)
        up = jnp.pad(gla_gk_up[l], ((0, LANES - GLA_RANK), (0, 0))).astype(BF16)
        o_gla = _gla(z3, zs3, up, row2(gla_gk_bias[l]), row2(gla_norm[l]))
        gbias = jnp.pad(jnp.stack([m_gate_bias[l, :M_HEADS], m_gate_bias[l, M_HEADS:]], axis=1).reshape(1, -1),
                        ((0, 0), (0, LANES - 2 * M_HEADS)))
        arow, rep = _mlstm_gates(zs3, gbias)
        qk3 = _mlstm_conv(z3, m_conv[l], ts=_pick(s, 512))
        o_m = _mlstm(qk3, z3, arow, rep, row2(m_norm[l]))
        xt = _merge(xt, o_gla.reshape(t, d), o_m.reshape(t, d), z,
                    gla_proj[l].astype(BF16), m_proj[l].astype(BF16), w_out[l].astype(BF16), tm=tm)
        kv = _norm_matmul(mem_t, row2(norm_mem[l]), x_wkv[l].astype(BF16), tm=_pick(b * m_len, 512), tn=1024)
        xt = _xattn(xt.reshape(b, s, d), row2(norm_xattn[l]), x_wq[l].astype(BF16),
                    kv.reshape(b, m_len, 2 * d), x_wo[l].astype(BF16), tq=_pick(s, 1024)).reshape(t, d)
        g_final = row2(norm_final) if last_layer else None
        if l % 2 == 0:
            w13, w2 = ffn_w13[l // 2], ffn_w2[l // 2]
            f = w2.shape[0]
            fp = -(-f // MXU_TILE) * MXU_TILE
            w1 = jnp.pad(w13[:, :f], ((0, 0), (0, fp - f))).astype(BF16)
            w3 = jnp.pad(w13[:, f:], ((0, 0), (0, fp - f))).astype(BF16)
            w2p = jnp.pad(w2, ((0, fp - f), (0, 0))).astype(BF16)
            xt = _ffn(xt, row2(norm_ffn[l]), w1, w3, w2p, g_final, tm=tm)
        else:
            router = jnp.pad(moe_router[l // 2], ((0, 0), (0, LANES - N_EXPERTS)))
            w13 = _cast_bf16(moe_w13[l // 2], rows=CAST_ROWS, cols=d)
            w2 = _cast_bf16(moe_w2[l // 2], rows=CAST_ROWS, cols=d)
            xt = _moe_sparse(xt, row2(norm_ffn[l]), router, w13, w2, g_final, tm=tm, tmg=tm, tf=w2.shape[1] // 2)
    return xt.reshape(b, s, d)
```

```python
import functools

import jax
import jax.numpy as jnp
from jax import lax
from jax.experimental import pallas as pl
from jax.experimental.pallas import tpu as pltpu
from jax.experimental.pallas import tpu_sc as plsc

F32 = jnp.float32
BF16 = jnp.bfloat16

EPS = 1e-6
LOG2E = 1.4426950408889634
D_MODEL = 1024
GLA_HEADS = 4
GLA_DK = 128
GLA_DV = 256
GLA_RANK = 16
GLA_GATE_NORM = 16.0
GLA_LOG_DECAY_MIN = -1.0
GLA_CHUNK = 64
GLA_HEADS_PER_STEP = 4
GLA_UNROLL = 8
M_HEADS = 4
M_DK = 128
M_DV = 256
M_CHUNK = 128
M_HEADS_PER_STEP = 2
M_UNROLL = 8
CONV_W = 4
X_HEADS = 4
N_EXPERTS = 8
LANES = 128
MXU_TILE = 256
SUBLANES = 8
VMEM_LIMIT = 48 * 1024 * 1024
SC_GATHER_WINDOW = 128
PACK_HALVES = 2
PACK_W = D_MODEL // 2 // PACK_HALVES
CAST_ROWS = 512
COMBINE_PARTS = 2

Z_GQ, Z_GK, Z_GV, Z_GG = 0, 512, 1024, 2048
Z_MQ, Z_MK, Z_MV, Z_MO = 3072, 3584, 4096, 5120
Z_AG, Z_AM = 6144, 7168
Z_COLS = 8192
ZS_COLS = 2 * LANES


def _params(*sem):
    return pltpu.CompilerParams(dimension_semantics=sem, vmem_limit_bytes=VMEM_LIMIT)


def _rms(x, g):
    return x * lax.rsqrt(jnp.mean(x * x, axis=-1, keepdims=True) + EPS) * g


def _log_sigmoid(u):
    return jnp.minimum(u, 0.0) - jnp.log(1.0 + jnp.exp(-jnp.abs(u)))


def _sigmoid(u):
    return 1.0 / (1.0 + jnp.exp(-u))


def _dot(a, b):
    return jnp.dot(a, b, preferred_element_type=F32)


def _dot_nt(a, b):
    return lax.dot_general(a, b, (((1,), (1,)), ((), ())), preferred_element_type=F32)


def _dot_tn(a, b):
    return lax.dot_general(a, b, (((0,), (0,)), ((), ())), preferred_element_type=F32)


def _cumsum_rows(x):
    n = x.shape[0]
    row = lax.broadcasted_iota(jnp.int32, x.shape, 0)
    s = 1
    while s < n:
        x = x + jnp.where(row >= s, pltpu.roll(x, s, 0), 0.0)
        s *= 2
    return x


def _norm_matmul_kernel(*refs, tn, with_small):
    if with_small:
        x_ref, g_ref, w_ref, ws_ref, z_ref, zs_ref = refs
    else:
        x_ref, g_ref, w_ref, z_ref = refs
    h = _rms(x_ref[...], g_ref[...]).astype(BF16)
    if with_small:
        zs_ref[...] = _dot(h, ws_ref[...])
    for c0 in range(0, z_ref.shape[1], tn):
        z_ref[:, c0:c0 + tn] = _dot(h, w_ref[:, c0:c0 + tn]).astype(z_ref.dtype)


def _norm_matmul(x, g, w, ws=None, *, tm, tn):
    t, d = x.shape
    n = w.shape[1]
    resident = pl.Buffered(1)
    in_specs = [pl.BlockSpec((tm, d), lambda i: (i, 0)),
                pl.BlockSpec((1, d), lambda i: (0, 0)),
                pl.BlockSpec((d, n), lambda i: (0, 0), pipeline_mode=resident)]
    out_specs = [pl.BlockSpec((tm, n), lambda i: (i, 0))]
    out_shape = [jax.ShapeDtypeStruct((t, n), BF16)]
    args = [x, g, w]
    if ws is not None:
        ns = ws.shape[1]
        in_specs.append(pl.BlockSpec((d, ns), lambda i: (0, 0), pipeline_mode=resident))
        out_specs.append(pl.BlockSpec((tm, ns), lambda i: (i, 0)))
        out_shape.append(jax.ShapeDtypeStruct((t, ns), F32))
        args.append(ws)
    out = pl.pallas_call(
        functools.partial(_norm_matmul_kernel, tn=tn, with_small=ws is not None),
        grid=(t // tm,),
        in_specs=in_specs, out_specs=out_specs, out_shape=out_shape,
        compiler_params=_params("parallel"),
        name="norm_matmul",
    )(*args)
    return out if ws is not None else out[0]


def _gla_kernel(q_ref, k_ref, v_ref, gg_ref, lr_ref, up_ref, gb_ref, nw_ref, o_ref, la_scr, st_scr):
    seq = q_ref.shape[0]
    c = GLA_CHUNK
    scale = GLA_DK ** -0.5

    u = _dot(lr_ref[...].astype(BF16), up_ref[...]) + gb_ref[...]
    la_scr[...] = jnp.maximum(_log_sigmoid(u) * (1.0 / GLA_GATE_NORM), GLA_LOG_DECAY_MIN)
    st_scr[...] = jnp.zeros_like(st_scr)
    causal = (lax.broadcasted_iota(jnp.int32, (c, c), 0) >= lax.broadcasted_iota(jnp.int32, (c, c), 1))

    def body(n, carry):
        r0 = pl.multiple_of(n * c, c)
        rows = pl.ds(r0, c)
        for hh in range(GLA_HEADS_PER_STEP):
            kcols = slice(hh * GLA_DK, (hh + 1) * GLA_DK)
            vcols = slice(hh * GLA_DV, (hh + 1) * GLA_DV)
            cum = _cumsum_rows(la_scr[rows, kcols])
            cum_last = cum[c - 1:c, :]
            q = q_ref[rows, kcols].astype(F32) * scale
            k = k_ref[rows, kcols].astype(F32)
            v = v_ref[rows, vcols]
            q_dec = (q * jnp.exp(cum)).astype(BF16)
            k_inv = (k * jnp.exp(-cum)).astype(BF16)
            k_end = (k * jnp.exp(cum_last - cum)).astype(BF16)
            scores = jnp.where(causal, _dot_nt(q_dec, k_inv), 0.0).astype(BF16)
            st = st_scr[hh]
            o = _dot(scores, v) + _dot_nt(q_dec, st.astype(BF16))
            st_scr[hh] = st * jnp.exp(cum_last) + _dot_tn(v, k_end)
            gate = gg_ref[rows, vcols].astype(F32)
            o_ref[rows, vcols] = (_rms(o, nw_ref[:, vcols]) * (gate * _sigmoid(gate))).astype(o_ref.dtype)
        return carry

    lax.fori_loop(0, seq // c, body, 0, unroll=GLA_UNROLL)


def _gla(z3, zs3, up, gb, nw):
    b, s, _ = z3.shape
    hps = GLA_HEADS_PER_STEP
    dk, dv = hps * GLA_DK, hps * GLA_DV
    qb, kb, vb, gb_ = Z_GQ // dk, Z_GK // dk, Z_GV // dv, Z_GG // dv
    return pl.pallas_call(
        _gla_kernel,
        grid=(b, GLA_HEADS // hps),
        in_specs=[pl.BlockSpec((None, s, dk), lambda i, h: (i, 0, qb + h)),
                  pl.BlockSpec((None, s, dk), lambda i, h: (i, 0, kb + h)),
                  pl.BlockSpec((None, s, dv), lambda i, h: (i, 0, vb + h)),
                  pl.BlockSpec((None, s, dv), lambda i, h: (i, 0, gb_ + h)),
                  pl.BlockSpec((None, s, LANES), lambda i, h: (i, 0, 0)),
                  pl.BlockSpec((LANES, dk), lambda i, h: (0, h)),
                  pl.BlockSpec((1, dk), lambda i, h: (0, h)),
                  pl.BlockSpec((1, dv), lambda i, h: (0, h))],
        out_specs=pl.BlockSpec((None, s, dv), lambda i, h: (i, 0, h)),
        out_shape=jax.ShapeDtypeStruct((b, s, GLA_HEADS * GLA_DV), BF16),
        scratch_shapes=[pltpu.VMEM((s, dk), F32), pltpu.VMEM((hps, GLA_DV, GLA_DK), F32)],
        compiler_params=_params("parallel", "arbitrary"),
        name="gla",
    )(z3, z3, z3, z3, zs3, up, gb, nw)


def _chunk_scan_lanes(x, combine, identity, chunk):
    lane_in_chunk = lax.broadcasted_iota(jnp.int32, x.shape, 1) % chunk
    step = 1
    while step < chunk:
        x = combine(x, jnp.where(lane_in_chunk >= step, pltpu.roll(x, step, 1), identity))
        step *= 2
    return x


def _replicate_rows(x, row):
    hi = x.astype(BF16)
    r1 = x - hi.astype(F32)
    mid = r1.astype(BF16)
    lo = (r1 - mid.astype(F32)).astype(BF16)
    parts = jnp.concatenate([hi, mid, lo, jnp.zeros_like(hi)], axis=0)
    pick = lax.broadcasted_iota(jnp.int32, (4 * SUBLANES, LANES), 0) % SUBLANES == row
    return _dot_tn(parts, jnp.where(pick, 1.0, 0.0).astype(BF16))


def _mlstm_gates_kernel(gt_ref, gb_ref, arow_ref, rep_ref):
    c = M_CHUNK
    gates_t = (gt_ref[...] + gb_ref[...]).T
    top = gates_t[0:SUBLANES, :]
    odd = lax.broadcasted_iota(jnp.int32, top.shape, 0) % 2 == 1
    lf = jnp.where(odd, _log_sigmoid(top), 0.0)
    b_odd = _chunk_scan_lanes(lf, jnp.add, 0.0, c)
    a_even = top - pltpu.roll(b_odd, SUBLANES - 1, 0)
    amax_even = _chunk_scan_lanes(jnp.where(odd, -jnp.inf, a_even), jnp.maximum, -jnp.inf, c)
    for h in range(M_HEADS):
        arow_ref[h] = pltpu.roll(a_even, (SUBLANES - 2 * h) % SUBLANES, 0)
        rep_ref[h, 0] = _replicate_rows(b_odd, 2 * h + 1)
        rep_ref[h, 1] = _replicate_rows(a_even, 2 * h)
        rep_ref[h, 2] = _replicate_rows(jnp.where(odd, 0.0, amax_even), 2 * h)


def _mlstm_gates(zs3, gbias):
    b, s, _ = zs3.shape
    return pl.pallas_call(
        _mlstm_gates_kernel,
        grid=(b,),
        in_specs=[pl.BlockSpec((None, s, LANES), lambda i: (i, 0, 1)),
                  pl.BlockSpec((1, LANES), lambda i: (0, 0))],
        out_specs=[pl.BlockSpec((None, M_HEADS, SUBLANES, s), lambda i: (i, 0, 0, 0)),
                   pl.BlockSpec((None, M_HEADS, 3, s, LANES), lambda i: (i, 0, 0, 0, 0))],
        out_shape=[jax.ShapeDtypeStruct((b, M_HEADS, SUBLANES, s), F32),
                   jax.ShapeDtypeStruct((b, M_HEADS, 3, s, LANES), F32)],
        compiler_params=_params("parallel"),
        name="mlstm_gates",
    )(zs3, gbias)


def _mlstm_conv_kernel(cur_ref, prev_ref, w_ref, shift_ref, o_ref):
    c = M_CHUNK
    d = cur_ref.shape[1]
    prev = jnp.where(pl.program_id(1) > 0, prev_ref[...], jnp.zeros_like(prev_ref))
    lane = lax.broadcasted_iota(jnp.int32, (1, d), 1)
    col_scale = jnp.where(lane < d // 2, M_DK ** -0.5, 1.0)
    for i in range(cur_ref.shape[0] // c):
        cur = cur_ref[i * c:(i + 1) * c, :]
        before = prev if i == 0 else cur_ref[(i - 1) * c:i * c, :]
        shifted = _dot(shift_ref[...], jnp.concatenate([before, cur], axis=0))
        acc = w_ref[CONV_W - 1:CONV_W, :] * cur.astype(F32)
        for s in range(1, CONV_W):
            acc = acc + w_ref[CONV_W - 1 - s:CONV_W - s, :] * shifted[(s - 1) * c:s * c, :]
        o_ref[i * c:(i + 1) * c, :] = (acc * _sigmoid(acc) * col_scale).astype(o_ref.dtype)


def _mlstm_conv(z3, conv, *, ts):
    b, s, _ = z3.shape
    c = M_CHUNK
    d = conv.shape[1]
    blk = Z_MQ // d
    assert Z_MQ % d == 0 and Z_MK == Z_MQ + d // 2
    t_idx = jnp.arange(c)[:, None]
    col = jnp.arange(2 * c)[None, :]
    shift = jnp.concatenate([(col == c + t_idx - s) for s in range(1, CONV_W)], axis=0).astype(BF16)
    return pl.pallas_call(
        _mlstm_conv_kernel,
        grid=(b, s // ts),
        in_specs=[pl.BlockSpec((None, ts, d), lambda i, j: (i, j, blk)),
                  pl.BlockSpec((None, c, d), lambda i, j: (i, jnp.maximum(j * (ts // c) - 1, 0), blk)),
                  pl.BlockSpec((CONV_W, d), lambda i, j: (0, 0)),
                  pl.BlockSpec(((CONV_W - 1) * c, 2 * c), lambda i, j: (0, 0))],
        out_specs=pl.BlockSpec((None, ts, d), lambda i, j: (i, j, 0)),
        out_shape=jax.ShapeDtypeStruct((b, s, d), BF16),
        compiler_params=_params("parallel", "parallel"),
        name="mlstm_conv",
    )(z3, z3, conv, shift)


def _mlstm_kernel(q_ref, k_ref, v_ref, og_ref, arow_ref, rep_ref, nw_ref, o_ref, st_scr, m_scr):
    seq = q_ref.shape[0]
    c = M_CHUNK
    st_scr[...] = jnp.zeros_like(st_scr)
    m_scr[...] = jnp.zeros_like(m_scr)
    causal = (lax.broadcasted_iota(jnp.int32, (c, c), 0) >= lax.broadcasted_iota(jnp.int32, (c, c), 1))
    ones_cols = jnp.ones((c, LANES), BF16)
    ones_dv = jnp.ones((M_DV, LANES), BF16)

    def wide(x, n):
        return jnp.concatenate([x] * n, axis=1)

    def body(n, carry):
        r0 = pl.multiple_of(n * c, c)
        rows = pl.ds(r0, c)
        for hh in range(M_HEADS_PER_STEP):
            kcols = slice(hh * M_DK, (hh + 1) * M_DK)
            vcols = slice(hh * M_DV, (hh + 1) * M_DV)
            qb = q_ref[rows, kcols]
            kb = k_ref[rows, kcols]
            v_ext = jnp.concatenate([v_ref[rows, vcols], ones_cols], axis=1)
            b = rep_ref[hh, 0, rows, :]
            a = rep_ref[hh, 1, rows, :]
            amax = rep_ref[hh, 2, rows, :]
            a_row = arow_ref[hh, 0:1, rows]
            m_prev = m_scr[hh]
            st = st_scr[hh]

            dmat = jnp.where(causal, b + a_row, -jnp.inf)
            m_inter = b + m_prev
            m_t = jnp.maximum(m_inter, b + amax)
            w_ts = jnp.exp(dmat - m_t) * _dot_nt(qb, kb)
            s_inter = jnp.exp(m_inter - m_t)
            nd = _dot(w_ts.astype(BF16), v_ext) + wide(s_inter, 3) * _dot(qb, st.astype(BF16))
            inv = 1.0 / jnp.maximum(jnp.abs(nd[:, M_DV:]), jnp.exp(-m_t))
            h = nd[:, :M_DV] * wide(inv, 2)

            b_last = b[c - 1:c, :]
            m_new = jnp.maximum(b_last + m_prev, b_last + amax[c - 1:c, :])
            kw = kb.astype(F32) * jnp.exp(b_last + a - m_new)
            st_scr[hh] = wide(jnp.exp(b_last + m_prev - m_new), 3) * st + _dot_tn(kw.astype(BF16), v_ext)
            m_scr[hh] = m_new

            mean_sq = _dot((h * h).astype(BF16), ones_dv) * (1.0 / M_DV)
            gate = og_ref[rows, vcols].astype(F32)
            y = h * wide(lax.rsqrt(mean_sq + EPS), 2) * nw_ref[:, vcols]
            o_ref[rows, vcols] = (y * _sigmoid(gate)).astype(o_ref.dtype)
        return carry

    lax.fori_loop(0, seq // c, body, 0, unroll=M_UNROLL)


def _mlstm(qk3, z3, arow, rep, nw):
    b, s, _ = z3.shape
    hps = M_HEADS_PER_STEP
    dk, dv = hps * M_DK, hps * M_DV
    vb, ob = Z_MV // dv, Z_MO // dv
    return pl.pallas_call(
        _mlstm_kernel,
        grid=(b, M_HEADS // hps),
        in_specs=[pl.BlockSpec((None, s, dk), lambda i, h: (i, 0, h)),
                  pl.BlockSpec((None, s, dk), lambda i, h: (i, 0, M_HEADS // hps + h)),
                  pl.BlockSpec((None, s, dv), lambda i, h: (i, 0, vb + h)),
                  pl.BlockSpec((None, s, dv), lambda i, h: (i, 0, ob + h)),
                  pl.BlockSpec((None, hps, SUBLANES, s), lambda i, h: (i, h, 0, 0)),
                  pl.BlockSpec((None, hps, 3, s, LANES), lambda i, h: (i, h, 0, 0, 0)),
                  pl.BlockSpec((1, dv), lambda i, h: (0, h))],
        out_specs=pl.BlockSpec((None, s, dv), lambda i, h: (i, 0, h)),
        out_shape=jax.ShapeDtypeStruct((b, s, M_HEADS * M_DV), BF16),
        scratch_shapes=[pltpu.VMEM((hps, M_DK, M_DV + LANES), F32), pltpu.VMEM((hps, 1, LANES), F32)],
        compiler_params=_params("parallel", "arbitrary"),
        name="mlstm",
    )(qk3, qk3, z3, z3, arow, rep, nw)


def _merge_kernel(x_ref, og_ref, om_ref, ag_ref, am_ref, wg_ref, wm_ref, wo_ref, o_ref):
    merged = (_sigmoid(ag_ref[...].astype(F32)) * _dot(og_ref[...], wg_ref[...])
              + _sigmoid(am_ref[...].astype(F32)) * _dot(om_ref[...], wm_ref[...]))
    o_ref[...] = x_ref[...] + _dot(merged.astype(BF16), wo_ref[...])


def _merge(x, o_gla, o_m, z, wg, wm, wo, *, tm):
    t, d = x.shape
    row = lambda i: (i, 0)
    full = lambda i: (0, 0)
    return pl.pallas_call(
        _merge_kernel,
        grid=(t // tm,),
        in_specs=[pl.BlockSpec((tm, d), row), pl.BlockSpec((tm, d), row), pl.BlockSpec((tm, d), row),
                  pl.BlockSpec((tm, d), lambda i: (i, Z_AG // D_MODEL)),
                  pl.BlockSpec((tm, d), lambda i: (i, Z_AM // D_MODEL)),
                  pl.BlockSpec((d, d), full), pl.BlockSpec((d, d), full), pl.BlockSpec((d, d), full)],
        out_specs=pl.BlockSpec((tm, d), row),
        out_shape=jax.ShapeDtypeStruct((t, d), F32),
        compiler_params=_params("parallel"),
        name="merge",
    )(x, o_gla, o_m, z, z, wg, wm, wo)


def _xattn_kernel(x_ref, g_ref, wq_ref, kv_ref, wo_ref, o_ref):
    d = x_ref.shape[1]
    dh = d // X_HEADS
    x = x_ref[...]
    q = _dot(_rms(x, g_ref[...]).astype(BF16), wq_ref[...]).astype(BF16)
    heads = []
    ones_m = jnp.ones((kv_ref.shape[0], LANES), BF16)
    for h in range(X_HEADS):
        k = kv_ref[:, h * dh:(h + 1) * dh]
        v = kv_ref[:, d + h * dh:d + (h + 1) * dh]
        s = _dot_nt(q[:, h * dh:(h + 1) * dh], k)
        p = jnp.exp2((s - jnp.max(s, axis=-1, keepdims=True)) * (dh ** -0.5 * LOG2E)).astype(BF16)
        inv_l = 1.0 / _dot(p, ones_m)
        heads.append((_dot(p, v) * jnp.concatenate([inv_l] * (dh // LANES), axis=1)).astype(BF16))
    o_ref[...] = x + _dot(jnp.concatenate(heads, axis=-1), wo_ref[...])


def _xattn(x3, g, wq, kv, wo, *, tq):
    b, s, d = x3.shape
    m = kv.shape[1]
    return pl.pallas_call(
        _xattn_kernel,
        grid=(b, s // tq),
        in_specs=[pl.BlockSpec((None, tq, d), lambda i, j: (i, j, 0)),
                  pl.BlockSpec((1, d), lambda i, j: (0, 0)),
                  pl.BlockSpec((d, d), lambda i, j: (0, 0)),
                  pl.BlockSpec((None, m, 2 * d), lambda i, j: (i, 0, 0)),
                  pl.BlockSpec((d, d), lambda i, j: (0, 0))],
        out_specs=pl.BlockSpec((None, tq, d), lambda i, j: (i, j, 0)),
        out_shape=jax.ShapeDtypeStruct((b, s, d), F32),
        compiler_params=_params("parallel", "parallel"),
        name="xattn",
    )(x3, g, wq, kv, wo)


def _ffn_kernel(*refs, chunks, final_norm):
    if final_norm:
        x_ref, g_ref, w1_ref, w3_ref, w2_ref, gf_ref, o_ref = refs
    else:
        x_ref, g_ref, w1_ref, w3_ref, w2_ref, o_ref = refs
    x = x_ref[...]
    h = _rms(x, g_ref[...]).astype(BF16)
    out = x
    for c0, c1 in chunks:
        a = _dot(h, w1_ref[:, c0:c1])
        act = (a * _sigmoid(a) * _dot(h, w3_ref[:, c0:c1])).astype(BF16)
        out = out + _dot(act, w2_ref[c0:c1, :])
    if final_norm:
        out = _rms(out, gf_ref[...])
    o_ref[...] = out


def _ffn(x, g, w1, w3, w2, g_final, *, tm):
    t, d = x.shape
    f = w2.shape[0]
    assert f % MXU_TILE == 0
    half = (f // MXU_TILE + 1) // 2 * MXU_TILE
    chunks = ((0, half), (half, f))
    resident = pl.Buffered(1)
    in_specs = [pl.BlockSpec((tm, d), lambda i: (i, 0)),
                pl.BlockSpec((1, d), lambda i: (0, 0)),
                pl.BlockSpec((d, f), lambda i: (0, 0), pipeline_mode=resident),
                pl.BlockSpec((d, f), lambda i: (0, 0), pipeline_mode=resident),
                pl.BlockSpec((f, d), lambda i: (0, 0), pipeline_mode=resident)]
    args = [x, g, w1, w3, w2]
    if g_final is not None:
        in_specs.append(pl.BlockSpec((1, d), lambda i: (0, 0)))
        args.append(g_final)
    return pl.pallas_call(
        functools.partial(_ffn_kernel, chunks=chunks, final_norm=g_final is not None),
        grid=(t // tm,),
        in_specs=in_specs,
        out_specs=pl.BlockSpec((tm, d), lambda i: (i, 0)),
        out_shape=jax.ShapeDtypeStruct((t, d), F32),
        compiler_params=_params("parallel"),
        name="dense_ffn",
    )(*args)


def _store_packed(ref, y):
    half = y.shape[1] // 2
    bits = lax.bitcast_convert_type(y.astype(BF16).astype(F32), jnp.int32)
    lo = lax.shift_right_logical(bits[:, :half], 16)
    hi = jnp.bitwise_and(bits[:, half:], jnp.int32(-65536))
    packed = jnp.bitwise_or(hi, lo)
    w = half // PACK_HALVES
    for p in range(PACK_HALVES):
        ref[p] = packed[:, p * w:(p + 1) * w]


def _load_packed(ref):
    packed = jnp.concatenate([ref[p] for p in range(PACK_HALVES)], axis=1)
    lo = lax.bitcast_convert_type(lax.shift_left(packed, 16), F32)
    hi = lax.bitcast_convert_type(jnp.bitwise_and(packed, jnp.int32(-65536)), F32)
    return jnp.concatenate([lo, hi], axis=1)


def _router_kernel(x_ref, g_ref, r_ref, hp_ref, mi_ref, mw_ref, cnt_ref, carry_scr):
    @pl.when(pl.program_id(0) == 0)
    def _():
        carry_scr[...] = jnp.zeros_like(carry_scr)

    rows = x_ref.shape[0]
    h = _rms(x_ref[...], g_ref[...])
    _store_packed(hp_ref, h)
    r = r_ref[...]
    h_hi = h.astype(BF16)
    h_lo = (h - h_hi.astype(F32)).astype(BF16)
    r_hi = r.astype(BF16)
    r_lo = (r - r_hi.astype(F32)).astype(BF16)
    hi_terms = _dot(h_hi, jnp.concatenate([r_hi, r_lo], axis=1))
    logits = hi_terms[:, :LANES] + (hi_terms[:, LANES:] + _dot(h_lo, r_hi))
    lane = lax.broadcasted_iota(jnp.int32, logits.shape, 1)
    logits = jnp.where(lane < N_EXPERTS, logits, -jnp.inf)
    m1 = jnp.max(logits, axis=-1, keepdims=True)
    i1 = jnp.min(jnp.where(logits == m1, lane, LANES), axis=-1, keepdims=True)
    rest = jnp.where(lane == i1, -jnp.inf, logits)
    m2 = jnp.max(rest, axis=-1, keepdims=True)
    i2 = jnp.min(jnp.where(rest == m2, lane, LANES), axis=-1, keepdims=True)
    e2 = jnp.exp(m2 - m1)
    w1 = 1.0 / (1.0 + e2)
    w2 = e2 / (1.0 + e2)
    oh1 = (lane == i1).astype(F32)
    oh2 = (lane == i2).astype(F32)
    oh = oh1 + oh2
    earlier = (lax.broadcasted_iota(jnp.int32, (rows, rows), 0) > lax.broadcasted_iota(jnp.int32, (rows, rows), 1))
    before = _dot(earlier.astype(BF16), oh.astype(BF16)) + carry_scr[...]
    rank1 = jnp.sum(oh1 * before, axis=-1, keepdims=True)
    rank2 = jnp.sum(oh2 * before, axis=-1, keepdims=True)
    carry_scr[...] += jnp.sum(oh, axis=0, keepdims=True)
    cnt_ref[...] = carry_scr[...]
    meta = jnp.where(lane == 0, i1.astype(F32), jnp.where(lane == 1, i2.astype(F32), jnp.where(lane == 2, rank1,
                     jnp.where(lane == 3, rank2, 0.0))))
    mi_ref[...] = meta.T[0:SUBLANES, :]
    mw_ref[...] = jnp.where(lane == 0, w1, jnp.where(lane == 1, w2, 0.0))


def _router(x, g, router, *, tm):
    t, d = x.shape
    row = lambda i: (i, 0)
    fix = lambda i: (0, 0)
    return pl.pallas_call(
        _router_kernel,
        grid=(t // tm,),
        in_specs=[pl.BlockSpec((tm, d), row), pl.BlockSpec((1, d), fix), pl.BlockSpec((d, LANES), fix)],
        out_specs=[pl.BlockSpec((PACK_HALVES, tm, PACK_W), lambda i: (0, i, 0)),
                   pl.BlockSpec((SUBLANES, tm), lambda i: (0, i)),
                   pl.BlockSpec((tm, LANES), row), pl.BlockSpec((1, LANES), fix)],
        out_shape=[jax.ShapeDtypeStruct((PACK_HALVES, t, PACK_W), jnp.int32),
                   jax.ShapeDtypeStruct((SUBLANES, t), F32),
                   jax.ShapeDtypeStruct((t, LANES), F32), jax.ShapeDtypeStruct((1, LANES), F32)],
        scratch_shapes=[pltpu.VMEM((1, LANES), F32)],
        compiler_params=_params("arbitrary"),
        name="router",
    )(x, g, router)


def _grouped_kernel(te_ref, tv_ref, xs_ref, w1_ref, w3_ref, w2_ref, ys_ref, h_scr, acc_scr):
    i = pl.program_id(0)
    j = pl.program_id(1)
    last = pl.num_programs(1) - 1
    valid = tv_ref[i]
    active = valid > 0

    @pl.when(jnp.logical_and(active, j == 0))
    def _():
        h = _load_packed(xs_ref)
        row = lax.broadcasted_iota(jnp.int32, h.shape, 0)
        h_scr[...] = jnp.where(row < valid, h, 0.0).astype(BF16)
        acc_scr[...] = jnp.zeros_like(acc_scr)

    @pl.when(active)
    def _():
        h = h_scr[...]
        a = _dot(h, w1_ref[...])
        act = (a * _sigmoid(a) * _dot(h, w3_ref[...])).astype(BF16)
        acc_scr[...] += _dot(act, w2_ref[...])

    @pl.when(jnp.logical_and(active, j == last))
    def _():
        _store_packed(ys_ref, acc_scr[...])

    @pl.when(jnp.logical_and(jnp.logical_not(active), j == last))
    def _():
        ys_ref[...] = jnp.zeros_like(ys_ref)


def _grouped_swiglu(tile_expert, tile_valid, xs, w13, w2, *, tmg, tf):
    _, p, _ = xs.shape
    d = w2.shape[2]
    f = w2.shape[1]
    nj = f // tf

    def jj(i, j, tv):
        return jnp.where(tv[i] > 0, j, nj - 1)

    grid_spec = pltpu.PrefetchScalarGridSpec(
        num_scalar_prefetch=2,
        grid=(p // tmg, nj),
        in_specs=[pl.BlockSpec((PACK_HALVES, tmg, PACK_W), lambda i, j, te, tv: (0, i, 0)),
                  pl.BlockSpec((None, d, tf), lambda i, j, te, tv: (te[i], 0, jj(i, j, tv))),
                  pl.BlockSpec((None, d, tf), lambda i, j, te, tv: (te[i], 0, nj + jj(i, j, tv))),
                  pl.BlockSpec((None, tf, d), lambda i, j, te, tv: (te[i], jj(i, j, tv), 0))],
        out_specs=pl.BlockSpec((PACK_HALVES, tmg, PACK_W), lambda i, j, te, tv: (0, i, 0)),
        scratch_shapes=[pltpu.VMEM((tmg, d), BF16), pltpu.VMEM((tmg, d), F32)],
    )
    return pl.pallas_call(
        _grouped_kernel,
        grid_spec=grid_spec,
        out_shape=jax.ShapeDtypeStruct((PACK_HALVES, p, PACK_W), jnp.int32),
        compiler_params=_params("arbitrary", "arbitrary"),
        name="grouped_swiglu",
    )(tile_expert, tile_valid, xs, w13, w13, w2)


def _combine_kernel(*refs, final_norm, has_prev):
    refs = list(refs)
    o_ref = refs.pop()
    if has_prev:
        refs.pop()
    x_ref, mw_ref, y1_ref, y2_ref = refs[:4]
    mw = mw_ref[...]
    out = x_ref[...] + mw[:, 0:1] * _load_packed(y1_ref) + mw[:, 1:2] * _load_packed(y2_ref)
    if final_norm:
        out = _rms(out, refs[4][...])
    o_ref[...] = out


def _combine(x, mw, y12, g_final, *, tm, part, parts, prev):
    t, d = x.shape
    blocks = t // parts // tm
    first = part * blocks
    row = lambda i: (first + i, 0)
    in_specs = [pl.BlockSpec((tm, d), row), pl.BlockSpec((tm, LANES), row),
                pl.BlockSpec((PACK_HALVES, tm, PACK_W), lambda i: (0, i, 0)),
                pl.BlockSpec((PACK_HALVES, tm, PACK_W), lambda i: (0, blocks + i, 0))]
    args = [x, mw, y12, y12]
    if g_final is not None:
        in_specs.append(pl.BlockSpec((1, d), lambda i: (0, 0)))
        args.append(g_final)
    aliases = {}
    if prev is not None:
        in_specs.append(pl.BlockSpec(memory_space=pl.ANY))
        aliases = {len(args): 0}
        args.append(prev)
    return pl.pallas_call(
        functools.partial(_combine_kernel, final_norm=g_final is not None, has_prev=prev is not None),
        grid=(blocks,),
        in_specs=in_specs,
        out_specs=pl.BlockSpec((tm, d), row),
        out_shape=jax.ShapeDtypeStruct((t, d), F32),
        input_output_aliases=aliases,
        compiler_params=_params("parallel"),
        name="combine",
    )(*args)


def _gather_rows(table, idx):
    halves, n_rows, width = table.shape
    n = halves * idx.shape[0]
    assert n % SC_GATHER_WINDOW == 0
    flat_idx = jnp.concatenate([idx + h * n_rows for h in range(halves)]).reshape(1, n)
    mesh = plsc.VectorSubcoreMesh(core_axis_name="core", subcore_axis_name="subcore")

    @pl.kernel(out_type=jax.ShapeDtypeStruct((n, width), table.dtype), mesh=mesh)
    def gather(table_hbm, idx_hbm, out_hbm):
        def body(idx_vmem, out_vmem):
            pltpu.sync_copy(table_hbm.at[idx_vmem.at[0]], out_vmem)

        pltpu.emit_pipeline(
            body,
            grid=(n // SC_GATHER_WINDOW,),
            in_specs=[pl.BlockSpec((1, SC_GATHER_WINDOW), lambda i: (0, i))],
            out_specs=[pl.BlockSpec((SC_GATHER_WINDOW, width), lambda i: (i, 0))],
            core_axis_name=("core", "subcore"),
            dimension_semantics=(pltpu.PARALLEL,),
        )(idx_hbm, out_hbm)

    return gather(table.reshape(halves * n_rows, width), flat_idx).reshape(halves, idx.shape[0], width)


def _dispatch_rows(table, pos_list, n_out):
    halves, t, width = table.shape
    copies = len(pos_list)
    idx = jnp.concatenate([pos + h * n_out for h in range(halves) for pos in pos_list])
    n = idx.shape[0]
    win_t = t // SC_GATHER_WINDOW
    assert t % SC_GATHER_WINDOW == 0
    mesh = plsc.VectorSubcoreMesh(core_axis_name="core", subcore_axis_name="subcore")

    @pl.kernel(out_type=jax.ShapeDtypeStruct((halves * n_out, width), table.dtype), mesh=mesh)
    def scatter(table_hbm, idx_hbm, out_hbm):
        def body(rows_vmem, idx_vmem):
            pltpu.sync_copy(rows_vmem, out_hbm.at[idx_vmem.at[0]])

        pltpu.emit_pipeline(
            body,
            grid=(n // SC_GATHER_WINDOW,),
            in_specs=[pl.BlockSpec((SC_GATHER_WINDOW, width),
                                   lambda i: ((i // (copies * win_t)) * win_t + i % win_t, 0)),
                      pl.BlockSpec((1, SC_GATHER_WINDOW), lambda i: (0, i))],
            out_specs=[],
            core_axis_name=("core", "subcore"),
            dimension_semantics=(pltpu.PARALLEL,),
        )(table_hbm, idx_hbm)

    return scatter(table.reshape(halves * t, width), idx.reshape(1, n)).reshape(halves, n_out, width)


def _moe_sparse(x, g, router, w13, w2, g_final, *, tm, tmg, tf):
    t, d = x.shape
    hp, mi, mw, cnt = _router(x, g, router, tm=tm)
    counts = cnt[0, :N_EXPERTS].astype(jnp.int32)
    tiles_per_expert = (counts + tmg - 1) // tmg
    tile_end = jnp.cumsum(tiles_per_expert)
    group_start = (tile_end - tiles_per_expert) * tmg
    e1, e2, r1, r2 = (mi[r].astype(jnp.int32) for r in range(4))
    pos1 = group_start[e1] + r1
    pos2 = group_start[e2] + r2
    n_tiles = 2 * t // tmg + N_EXPERTS
    tile = jnp.arange(n_tiles, dtype=jnp.int32)
    tile_expert = jnp.minimum(jnp.sum(tile[:, None] >= tile_end[None, :], axis=1), N_EXPERTS - 1).astype(jnp.int32)
    rows_left = counts[tile_expert] - (tile - (tile_end - tiles_per_expert)[tile_expert]) * tmg
    tile_valid = jnp.where(tile < tile_end[-1], jnp.clip(rows_left, 0, tmg), 0).astype(jnp.int32)

    xs = _dispatch_rows(hp, [pos1, pos2], n_tiles * tmg)
    ys = _grouped_swiglu(tile_expert, tile_valid, xs, w13, w2, tmg=tmg, tf=tf)
    out = None
    span = t // COMBINE_PARTS
    for part in range(COMBINE_PARTS):
        tok = slice(part * span, (part + 1) * span)
        y12 = _gather_rows(ys, jnp.concatenate([pos1[tok], pos2[tok]]))
        out = _combine(x, mw, y12, g_final, tm=_pick(span, tm), part=part, parts=COMBINE_PARTS, prev=out)
    return out


def _cast_kernel(w_ref, o_ref):
    o_ref[...] = w_ref[...].astype(o_ref.dtype)


def _cast_bf16(w, *, rows, cols):
    shape = w.shape
    w2 = w.reshape(-1, shape[-1])
    r, c = w2.shape
    out = pl.pallas_call(
        _cast_kernel,
        grid=(r // rows, c // cols),
        in_specs=[pl.BlockSpec((rows, cols), lambda i, j: (i, j))],
        out_specs=pl.BlockSpec((rows, cols), lambda i, j: (i, j)),
        out_shape=jax.ShapeDtypeStruct((r, c), BF16),
        compiler_params=_params("parallel", "parallel"),
        name="cast_bf16",
    )(w2)
    return out.reshape(shape)


def _split_w_in(w_in):
    sizes = (512, 512, 1024, GLA_RANK, 1024, 512, 512, 1024, M_HEADS, M_HEADS, 1024, 1024, 1024)
    parts, off = [], 0
    for n in sizes:
        parts.append(w_in[:, off:off + n])
        off += n
    g_q, g_k, g_v, g_lr, g_g, m_q, m_k, m_v, m_i, m_f, m_o, a_g, a_m = parts
    wz = jnp.concatenate([g_q, g_k, g_v, g_g, m_q, m_k, m_v, m_o, a_g, a_m], axis=1).astype(BF16)
    d = w_in.shape[0]
    gate_cols = jnp.stack([m_i, m_f], axis=2).reshape(d, 2 * M_HEADS)
    ws = jnp.concatenate([jnp.pad(g_lr, ((0, 0), (0, LANES - GLA_RANK))),
                          jnp.pad(gate_cols, ((0, 0), (0, LANES - 2 * M_HEADS)))], axis=1).astype(BF16)
    assert wz.shape == (d, Z_COLS) and ws.shape == (d, ZS_COLS)
    return wz, ws


def _pick(total, want):
    t = min(total, want)
    while total % t:
        t -= 1
    return t


def kernel(x, mem, norm_mix, w_in, gla_gk_up, gla_gk_bias, gla_norm, m_conv, m_gate_bias, m_norm, gla_proj, m_proj, w_out, norm_xattn, norm_mem, x_wq, x_wkv, x_wo, norm_ffn, ffn_w13, ffn_w2, moe_router, moe_w13, moe_w2, norm_final):
    b, s, d = x.shape
    m_len = mem.shape[1]
    t = b * s
    depth = norm_mix.shape[0]
    tm = _pick(t, 512)
    row2 = lambda a: a.reshape(1, -1)

    xt = x.reshape(t, d)
    mem_t = mem.reshape(b * m_len, d)
    for l in range(depth):
        last_layer = l == depth - 1
        wz, ws = _split_w_in(w_in[l])
        z, zs = _norm_matmul(xt, row2(norm_mix[l]), wz, ws, tm=tm, tn=2048)
        z3 = z.reshape(b, s, Z_COLS)
        zs3 = zs.reshape(b, s, ZS_COLS)
        up = jnp.pad(gla_gk_up[l], ((0, LANES - GLA_RANK), (0, 0))).astype(BF16)
        o_gla = _gla(z3, zs3, up, row2(gla_gk_bias[l]), row2(gla_norm[l]))
        gbias = jnp.pad(jnp.stack([m_gate_bias[l, :M_HEADS], m_gate_bias[l, M_HEADS:]], axis=1).reshape(1, -1),
                        ((0, 0), (0, LANES - 2 * M_HEADS)))
        arow, rep = _mlstm_gates(zs3, gbias)
        qk3 = _mlstm_conv(z3, m_conv[l], ts=_pick(s, 512))
        o_m = _mlstm(qk3, z3, arow, rep, row2(m_norm[l]))
        xt = _merge(xt, o_gla.reshape(t, d), o_m.reshape(t, d), z,
                    gla_proj[l].astype(BF16), m_proj[l].astype(BF16), w_out[l].astype(BF16), tm=tm)
        kv = _norm_matmul(mem_t, row2(norm_mem[l]), x_wkv[l].astype(BF16), tm=_pick(b * m_len, 512), tn=1024)
        xt = _xattn(xt.reshape(b, s, d), row2(norm_xattn[l]), x_wq[l].astype(BF16),
                    kv.reshape(b, m_len, 2 * d), x_wo[l].astype(BF16), tq=_pick(s, 1024)).reshape(t, d)
        g_final = row2(norm_final) if last_layer else None
        if l % 2 == 0:
            w13, w2 = ffn_w13[l // 2], ffn_w2[l // 2]
            f = w2.shape[0]
            fp = -(-f // MXU_TILE) * MXU_TILE
            w1 = jnp.pad(w13[:, :f], ((0, 0), (0, fp - f))).astype(BF16)
            w3 = jnp.pad(w13[:, f:], ((0, 0), (0, fp - f))).astype(BF16)
            w2p = jnp.pad(w2, ((0, fp - f), (0, 0))).astype(BF16)
            xt = _ffn(xt, row2(norm_ffn[l]), w1, w3, w2p, g_final, tm=tm)
        else:
            router = jnp.pad(moe_router[l // 2], ((0, 0), (0, LANES - N_EXPERTS)))
            w13 = _cast_bf16(moe_w13[l // 2], rows=CAST_ROWS, cols=d)
            w2 = _cast_bf16(moe_w2[l // 2], rows=CAST_ROWS, cols=d)
            xt = _moe_sparse(xt, row2(norm_ffn[l]), router, w13, w2, g_final, tm=tm, tmg=tm, tf=w2.shape[1] // 2)
    return xt.reshape(b, s, d)
```

```python
import functools

import jax
import jax.numpy as jnp
from jax import lax
from jax.experimental import pallas as pl
from jax.experimental.pallas import tpu as pltpu
from jax.experimental.pallas import tpu_sc as plsc

F32 = jnp.float32
BF16 = jnp.bfloat16

EPS = 1e-6
LOG2E = 1.4426950408889634
D_MODEL = 1024
GLA_HEADS = 4
GLA_DK = 128
GLA_DV = 256
GLA_RANK = 16
GLA_GATE_NORM = 16.0
GLA_LOG_DECAY_MIN = -1.0
GLA_CHUNK = 64
GLA_HEADS_PER_STEP = 4
GLA_UNROLL = 8
M_HEADS = 4
M_DK = 128
M_DV = 256
M_CHUNK = 128
M_HEADS_PER_STEP = 2
M_UNROLL = 8
CONV_W = 4
X_HEADS = 4
N_EXPERTS = 8
LANES = 128
MXU_TILE = 256
SUBLANES = 8
VMEM_LIMIT = 48 * 1024 * 1024
SC_GATHER_WINDOW = 128
PACK_HALVES = 2
PACK_W = D_MODEL // 2 // PACK_HALVES
CAST_ROWS = 512

Z_GQ, Z_GK, Z_GV, Z_GG = 0, 512, 1024, 2048
Z_MQ, Z_MK, Z_MV, Z_MO = 3072, 3584, 4096, 5120
Z_AG, Z_AM = 6144, 7168
Z_COLS = 8192
ZS_COLS = 2 * LANES


def _params(*sem):
    return pltpu.CompilerParams(dimension_semantics=sem, vmem_limit_bytes=VMEM_LIMIT)


def _rms(x, g):
    return x * lax.rsqrt(jnp.mean(x * x, axis=-1, keepdims=True) + EPS) * g


def _log_sigmoid(u):
    return jnp.minimum(u, 0.0) - jnp.log(1.0 + jnp.exp(-jnp.abs(u)))


def _sigmoid(u):
    return 1.0 / (1.0 + jnp.exp(-u))


def _dot(a, b):
    return jnp.dot(a, b, preferred_element_type=F32)


def _dot_nt(a, b):
    return lax.dot_general(a, b, (((1,), (1,)), ((), ())), preferred_element_type=F32)


def _dot_tn(a, b):
    return lax.dot_general(a, b, (((0,), (0,)), ((), ())), preferred_element_type=F32)


def _cumsum_rows(x):
    n = x.shape[0]
    row = lax.broadcasted_iota(jnp.int32, x.shape, 0)
    s = 1
    while s < n:
        x = x + jnp.where(row >= s, pltpu.roll(x, s, 0), 0.0)
        s *= 2
    return x


def _norm_matmul_kernel(*refs, tn, with_small):
    if with_small:
        x_ref, g_ref, w_ref, ws_ref, z_ref, zs_ref = refs
    else:
        x_ref, g_ref, w_ref, z_ref = refs
    h = _rms(x_ref[...], g_ref[...]).astype(BF16)
    if with_small:
        zs_ref[...] = _dot(h, ws_ref[...])
    for c0 in range(0, z_ref.shape[1], tn):
        z_ref[:, c0:c0 + tn] = _dot(h, w_ref[:, c0:c0 + tn]).astype(z_ref.dtype)


def _norm_matmul(x, g, w, ws=None, *, tm, tn):
    t, d = x.shape
    n = w.shape[1]
    resident = pl.Buffered(1)
    in_specs = [pl.BlockSpec((tm, d), lambda i: (i, 0)),
                pl.BlockSpec((1, d), lambda i: (0, 0)),
                pl.BlockSpec((d, n), lambda i: (0, 0), pipeline_mode=resident)]
    out_specs = [pl.BlockSpec((tm, n), lambda i: (i, 0))]
    out_shape = [jax.ShapeDtypeStruct((t, n), BF16)]
    args = [x, g, w]
    if ws is not None:
        ns = ws.shape[1]
        in_specs.append(pl.BlockSpec((d, ns), lambda i: (0, 0), pipeline_mode=resident))
        out_specs.append(pl.BlockSpec((tm, ns), lambda i: (i, 0)))
        out_shape.append(jax.ShapeDtypeStruct((t, ns), F32))
        args.append(ws)
    out = pl.pallas_call(
        functools.partial(_norm_matmul_kernel, tn=tn, with_small=ws is not None),
        grid=(t // tm,),
        in_specs=in_specs, out_specs=out_specs, out_shape=out_shape,
        compiler_params=_params("parallel"),
        name="norm_matmul",
    )(*args)
    return out if ws is not None else out[0]


def _gla_kernel(q_ref, k_ref, v_ref, gg_ref, lr_ref, up_ref, gb_ref, nw_ref, o_ref, la_scr, st_scr):
    seq = q_ref.shape[0]
    c = GLA_CHUNK
    scale = GLA_DK ** -0.5

    u = _dot(lr_ref[...].astype(BF16), up_ref[...]) + gb_ref[...]
    la_scr[...] = jnp.maximum(_log_sigmoid(u) * (1.0 / GLA_GATE_NORM), GLA_LOG_DECAY_MIN)
    st_scr[...] = jnp.zeros_like(st_scr)
    causal = (lax.broadcasted_iota(jnp.int32, (c, c), 0) >= lax.broadcasted_iota(jnp.int32, (c, c), 1))

    def body(n, carry):
        r0 = pl.multiple_of(n * c, c)
        rows = pl.ds(r0, c)
        for hh in range(GLA_HEADS_PER_STEP):
            kcols = slice(hh * GLA_DK, (hh + 1) * GLA_DK)
            vcols = slice(hh * GLA_DV, (hh + 1) * GLA_DV)
            cum = _cumsum_rows(la_scr[rows, kcols])
            cum_last = cum[c - 1:c, :]
            q = q_ref[rows, kcols].astype(F32) * scale
            k = k_ref[rows, kcols].astype(F32)
            v = v_ref[rows, vcols]
            q_dec = (q * jnp.exp(cum)).astype(BF16)
            k_inv = (k * jnp.exp(-cum)).astype(BF16)
            k_end = (k * jnp.exp(cum_last - cum)).astype(BF16)
            scores = jnp.where(causal, _dot_nt(q_dec, k_inv), 0.0).astype(BF16)
            st = st_scr[hh]
            o = _dot(scores, v) + _dot_nt(q_dec, st.astype(BF16))
            st_scr[hh] = st * jnp.exp(cum_last) + _dot_tn(v, k_end)
            gate = gg_ref[rows, vcols].astype(F32)
            o_ref[rows, vcols] = (_rms(o, nw_ref[:, vcols]) * (gate * _sigmoid(gate))).astype(o_ref.dtype)
        return carry

    lax.fori_loop(0, seq // c, body, 0, unroll=GLA_UNROLL)


def _gla(z3, zs3, up, gb, nw):
    b, s, _ = z3.shape
    hps = GLA_HEADS_PER_STEP
    dk, dv = hps * GLA_DK, hps * GLA_DV
    qb, kb, vb, gb_ = Z_GQ // dk, Z_GK // dk, Z_GV // dv, Z_GG // dv
    return pl.pallas_call(
        _gla_kernel,
        grid=(b, GLA_HEADS // hps),
        in_specs=[pl.BlockSpec((None, s, dk), lambda i, h: (i, 0, qb + h)),
                  pl.BlockSpec((None, s, dk), lambda i, h: (i, 0, kb + h)),
                  pl.BlockSpec((None, s, dv), lambda i, h: (i, 0, vb + h)),
                  pl.BlockSpec((None, s, dv), lambda i, h: (i, 0, gb_ + h)),
                  pl.BlockSpec((None, s, LANES), lambda i, h: (i, 0, 0)),
                  pl.BlockSpec((LANES, dk), lambda i, h: (0, h)),
                  pl.BlockSpec((1, dk), lambda i, h: (0, h)),
                  pl.BlockSpec((1, dv), lambda i, h: (0, h))],
        out_specs=pl.BlockSpec((None, s, dv), lambda i, h: (i, 0, h)),
        out_shape=jax.ShapeDtypeStruct((b, s, GLA_HEADS * GLA_DV), BF16),
        scratch_shapes=[pltpu.VMEM((s, dk), F32), pltpu.VMEM((hps, GLA_DV, GLA_DK), F32)],
        compiler_params=_params("parallel", "arbitrary"),
        name="gla",
    )(z3, z3, z3, z3, zs3, up, gb, nw)


def _chunk_scan_lanes(x, combine, identity, chunk):
    lane_in_chunk = lax.broadcasted_iota(jnp.int32, x.shape, 1) % chunk
    step = 1
    while step < chunk:
        x = combine(x, jnp.where(lane_in_chunk >= step, pltpu.roll(x, step, 1), identity))
        step *= 2
    return x


def _replicate_rows(x, row):
    hi = x.astype(BF16)
    r1 = x - hi.astype(F32)
    mid = r1.astype(BF16)
    lo = (r1 - mid.astype(F32)).astype(BF16)
    parts = jnp.concatenate([hi, mid, lo, jnp.zeros_like(hi)], axis=0)
    pick = lax.broadcasted_iota(jnp.int32, (4 * SUBLANES, LANES), 0) % SUBLANES == row
    return _dot_tn(parts, jnp.where(pick, 1.0, 0.0).astype(BF16))


def _mlstm_gates_kernel(gt_ref, gb_ref, arow_ref, rep_ref):
    c = M_CHUNK
    gates_t = (gt_ref[...] + gb_ref[...]).T
    top = gates_t[0:SUBLANES, :]
    odd = lax.broadcasted_iota(jnp.int32, top.shape, 0) % 2 == 1
    lf = jnp.where(odd, _log_sigmoid(top), 0.0)
    b_odd = _chunk_scan_lanes(lf, jnp.add, 0.0, c)
    a_even = top - pltpu.roll(b_odd, SUBLANES - 1, 0)
    amax_even = _chunk_scan_lanes(jnp.where(odd, -jnp.inf, a_even), jnp.maximum, -jnp.inf, c)
    for h in range(M_HEADS):
        arow_ref[h] = pltpu.roll(a_even, (SUBLANES - 2 * h) % SUBLANES, 0)
        rep_ref[h, 0] = _replicate_rows(b_odd, 2 * h + 1)
        rep_ref[h, 1] = _replicate_rows(a_even, 2 * h)
        rep_ref[h, 2] = _replicate_rows(jnp.where(odd, 0.0, amax_even), 2 * h)


def _mlstm_gates(zs3, gbias):
    b, s, _ = zs3.shape
    return pl.pallas_call(
        _mlstm_gates_kernel,
        grid=(b,),
        in_specs=[pl.BlockSpec((None, s, LANES), lambda i: (i, 0, 1)),
                  pl.BlockSpec((1, LANES), lambda i: (0, 0))],
        out_specs=[pl.BlockSpec((None, M_HEADS, SUBLANES, s), lambda i: (i, 0, 0, 0)),
                   pl.BlockSpec((None, M_HEADS, 3, s, LANES), lambda i: (i, 0, 0, 0, 0))],
        out_shape=[jax.ShapeDtypeStruct((b, M_HEADS, SUBLANES, s), F32),
                   jax.ShapeDtypeStruct((b, M_HEADS, 3, s, LANES), F32)],
        compiler_params=_params("parallel"),
        name="mlstm_gates",
    )(zs3, gbias)


def _mlstm_conv_kernel(cur_ref, prev_ref, w_ref, shift_ref, o_ref):
    c = M_CHUNK
    d = cur_ref.shape[1]
    prev = jnp.where(pl.program_id(1) > 0, prev_ref[...], jnp.zeros_like(prev_ref))
    lane = lax.broadcasted_iota(jnp.int32, (1, d), 1)
    col_scale = jnp.where(lane < d // 2, M_DK ** -0.5, 1.0)
    for i in range(cur_ref.shape[0] // c):
        cur = cur_ref[i * c:(i + 1) * c, :]
        before = prev if i == 0 else cur_ref[(i - 1) * c:i * c, :]
        shifted = _dot(shift_ref[...], jnp.concatenate([before, cur], axis=0))
        acc = w_ref[CONV_W - 1:CONV_W, :] * cur.astype(F32)
        for s in range(1, CONV_W):
            acc = acc + w_ref[CONV_W - 1 - s:CONV_W - s, :] * shifted[(s - 1) * c:s * c, :]
        o_ref[i * c:(i + 1) * c, :] = (acc * _sigmoid(acc) * col_scale).astype(o_ref.dtype)


def _mlstm_conv(z3, conv, *, ts):
    b, s, _ = z3.shape
    c = M_CHUNK
    d = conv.shape[1]
    blk = Z_MQ // d
    assert Z_MQ % d == 0 and Z_MK == Z_MQ + d // 2
    t_idx = jnp.arange(c)[:, None]
    col = jnp.arange(2 * c)[None, :]
    shift = jnp.concatenate([(col == c + t_idx - s) for s in range(1, CONV_W)], axis=0).astype(BF16)
    return pl.pallas_call(
        _mlstm_conv_kernel,
        grid=(b, s // ts),
        in_specs=[pl.BlockSpec((None, ts, d), lambda i, j: (i, j, blk)),
                  pl.BlockSpec((None, c, d), lambda i, j: (i, jnp.maximum(j * (ts // c) - 1, 0), blk)),
                  pl.BlockSpec((CONV_W, d), lambda i, j: (0, 0)),
                  pl.BlockSpec(((CONV_W - 1) * c, 2 * c), lambda i, j: (0, 0))],
        out_specs=pl.BlockSpec((None, ts, d), lambda i, j: (i, j, 0)),
        out_shape=jax.ShapeDtypeStruct((b, s, d), BF16),
        compiler_params=_params("parallel", "parallel"),
        name="mlstm_conv",
    )(z3, z3, conv, shift)


def _mlstm_kernel(q_ref, k_ref, v_ref, og_ref, arow_ref, rep_ref, nw_ref, o_ref, st_scr, m_scr):
    seq = q_ref.shape[0]
    c = M_CHUNK
    st_scr[...] = jnp.zeros_like(st_scr)
    m_scr[...] = jnp.zeros_like(m_scr)
    causal = (lax.broadcasted_iota(jnp.int32, (c, c), 0) >= lax.broadcasted_iota(jnp.int32, (c, c), 1))
    ones_cols = jnp.ones((c, LANES), BF16)
    ones_dv = jnp.ones((M_DV, LANES), BF16)

    def wide(x, n):
        return jnp.concatenate([x] * n, axis=1)

    def body(n, carry):
        r0 = pl.multiple_of(n * c, c)
        rows = pl.ds(r0, c)
        for hh in range(M_HEADS_PER_STEP):
            kcols = slice(hh * M_DK, (hh + 1) * M_DK)
            vcols = slice(hh * M_DV, (hh + 1) * M_DV)
            qb = q_ref[rows, kcols]
            kb = k_ref[rows, kcols]
            v_ext = jnp.concatenate([v_ref[rows, vcols], ones_cols], axis=1)
            b = rep_ref[hh, 0, rows, :]
            a = rep_ref[hh, 1, rows, :]
            amax = rep_ref[hh, 2, rows, :]
            a_row = arow_ref[hh, 0:1, rows]
            m_prev = m_scr[hh]
            st = st_scr[hh]

            dmat = jnp.where(causal, b + a_row, -jnp.inf)
            m_inter = b + m_prev
            m_t = jnp.maximum(m_inter, b + amax)
            w_ts = jnp.exp(dmat - m_t) * _dot_nt(qb, kb)
            s_inter = jnp.exp(m_inter - m_t)
            nd = _dot(w_ts.astype(BF16), v_ext) + wide(s_inter, 3) * _dot(qb, st.astype(BF16))
            inv = 1.0 / jnp.maximum(jnp.abs(nd[:, M_DV:]), jnp.exp(-m_t))
            h = nd[:, :M_DV] * wide(inv, 2)

            b_last = b[c - 1:c, :]
            m_new = jnp.maximum(b_last + m_prev, b_last + amax[c - 1:c, :])
            kw = kb.astype(F32) * jnp.exp(b_last + a - m_new)
            st_scr[hh] = wide(jnp.exp(b_last + m_prev - m_new), 3) * st + _dot_tn(kw.astype(BF16), v_ext)
            m_scr[hh] = m_new

            mean_sq = _dot((h * h).astype(BF16), ones_dv) * (1.0 / M_DV)
            gate = og_ref[rows, vcols].astype(F32)
            y = h * wide(lax.rsqrt(mean_sq + EPS), 2) * nw_ref[:, vcols]
            o_ref[rows, vcols] = (y * _sigmoid(gate)).astype(o_ref.dtype)
        return carry

    lax.fori_loop(0, seq // c, body, 0, unroll=M_UNROLL)


def _mlstm(qk3, z3, arow, rep, nw):
    b, s, _ = z3.shape
    hps = M_HEADS_PER_STEP
    dk, dv = hps * M_DK, hps * M_DV
    vb, ob = Z_MV // dv, Z_MO // dv
    return pl.pallas_call(
        _mlstm_kernel,
        grid=(b, M_HEADS // hps),
        in_specs=[pl.BlockSpec((None, s, dk), lambda i, h: (i, 0, h)),
                  pl.BlockSpec((None, s, dk), lambda i, h: (i, 0, M_HEADS // hps + h)),
                  pl.BlockSpec((None, s, dv), lambda i, h: (i, 0, vb + h)),
                  pl.BlockSpec((None, s, dv), lambda i, h: (i, 0, ob + h)),
                  pl.BlockSpec((None, hps, SUBLANES, s), lambda i, h: (i, h, 0, 0)),
                  pl.BlockSpec((None, hps, 3, s, LANES), lambda i, h: (i, h, 0, 0, 0)),
                  pl.BlockSpec((1, dv), lambda i, h: (0, h))],
        out_specs=pl.BlockSpec((None, s, dv), lambda i, h: (i, 0, h)),
        out_shape=jax.ShapeDtypeStruct((b, s, M_HEADS * M_DV), BF16),
        scratch_shapes=[pltpu.VMEM((hps, M_DK, M_DV + LANES), F32), pltpu.VMEM((hps, 1, LANES), F32)],
        compiler_params=_params("parallel", "arbitrary"),
        name="mlstm",
    )(qk3, qk3, z3, z3, arow, rep, nw)


def _merge_kernel(x_ref, og_ref, om_ref, ag_ref, am_ref, wg_ref, wm_ref, wo_ref, o_ref):
    merged = (_sigmoid(ag_ref[...].astype(F32)) * _dot(og_ref[...], wg_ref[...])
              + _sigmoid(am_ref[...].astype(F32)) * _dot(om_ref[...], wm_ref[...]))
    o_ref[...] = x_ref[...] + _dot(merged.astype(BF16), wo_ref[...])


def _merge(x, o_gla, o_m, z, wg, wm, wo, *, tm):
    t, d = x.shape
    row = lambda i: (i, 0)
    full = lambda i: (0, 0)
    return pl.pallas_call(
        _merge_kernel,
        grid=(t // tm,),
        in_specs=[pl.BlockSpec((tm, d), row), pl.BlockSpec((tm, d), row), pl.BlockSpec((tm, d), row),
                  pl.BlockSpec((tm, d), lambda i: (i, Z_AG // D_MODEL)),
                  pl.BlockSpec((tm, d), lambda i: (i, Z_AM // D_MODEL)),
                  pl.BlockSpec((d, d), full), pl.BlockSpec((d, d), full), pl.BlockSpec((d, d), full)],
        out_specs=pl.BlockSpec((tm, d), row),
        out_shape=jax.ShapeDtypeStruct((t, d), F32),
        compiler_params=_params("parallel"),
        name="merge",
    )(x, o_gla, o_m, z, z, wg, wm, wo)


def _xattn_kernel(x_ref, g_ref, wq_ref, kv_ref, wo_ref, o_ref):
    d = x_ref.shape[1]
    dh = d // X_HEADS
    x = x_ref[...]
    q = _dot(_rms(x, g_ref[...]).astype(BF16), wq_ref[...]).astype(BF16)
    heads = []
    ones_m = jnp.ones((kv_ref.shape[0], LANES), BF16)
    for h in range(X_HEADS):
        k = kv_ref[:, h * dh:(h + 1) * dh]
        v = kv_ref[:, d + h * dh:d + (h + 1) * dh]
        s = _dot_nt(q[:, h * dh:(h + 1) * dh], k)
        p = jnp.exp2((s - jnp.max(s, axis=-1, keepdims=True)) * (dh ** -0.5 * LOG2E)).astype(BF16)
        inv_l = 1.0 / _dot(p, ones_m)
        heads.append((_dot(p, v) * jnp.concatenate([inv_l] * (dh // LANES), axis=1)).astype(BF16))
    o_ref[...] = x + _dot(jnp.concatenate(heads, axis=-1), wo_ref[...])


def _xattn(x3, g, wq, kv, wo, *, tq):
    b, s, d = x3.shape
    m = kv.shape[1]
    return pl.pallas_call(
        _xattn_kernel,
        grid=(b, s // tq),
        in_specs=[pl.BlockSpec((None, tq, d), lambda i, j: (i, j, 0)),
                  pl.BlockSpec((1, d), lambda i, j: (0, 0)),
                  pl.BlockSpec((d, d), lambda i, j: (0, 0)),
                  pl.BlockSpec((None, m, 2 * d), lambda i, j: (i, 0, 0)),
                  pl.BlockSpec((d, d), lambda i, j: (0, 0))],
        out_specs=pl.BlockSpec((None, tq, d), lambda i, j: (i, j, 0)),
        out_shape=jax.ShapeDtypeStruct((b, s, d), F32),
        compiler_params=_params("parallel", "parallel"),
        name="xattn",
    )(x3, g, wq, kv, wo)


def _ffn_kernel(*refs, chunks, final_norm):
    if final_norm:
        x_ref, g_ref, w1_ref, w3_ref, w2_ref, gf_ref, o_ref = refs
    else:
        x_ref, g_ref, w1_ref, w3_ref, w2_ref, o_ref = refs
    x = x_ref[...]
    h = _rms(x, g_ref[...]).astype(BF16)
    out = x
    for c0, c1 in chunks:
        a = _dot(h, w1_ref[:, c0:c1])
        act = (a * _sigmoid(a) * _dot(h, w3_ref[:, c0:c1])).astype(BF16)
        out = out + _dot(act, w2_ref[c0:c1, :])
    if final_norm:
        out = _rms(out, gf_ref[...])
    o_ref[...] = out


def _ffn(x, g, w1, w3, w2, g_final, *, tm):
    t, d = x.shape
    f = w2.shape[0]
    assert f % MXU_TILE == 0
    half = (f // MXU_TILE + 1) // 2 * MXU_TILE
    chunks = ((0, half), (half, f))
    resident = pl.Buffered(1)
    in_specs = [pl.BlockSpec((tm, d), lambda i: (i, 0)),
                pl.BlockSpec((1, d), lambda i: (0, 0)),
                pl.BlockSpec((d, f), lambda i: (0, 0), pipeline_mode=resident),
                pl.BlockSpec((d, f), lambda i: (0, 0), pipeline_mode=resident),
                pl.BlockSpec((f, d), lambda i: (0, 0), pipeline_mode=resident)]
    args = [x, g, w1, w3, w2]
    if g_final is not None:
        in_specs.append(pl.BlockSpec((1, d), lambda i: (0, 0)))
        args.append(g_final)
    return pl.pallas_call(
        functools.partial(_ffn_kernel, chunks=chunks, final_norm=g_final is not None),
        grid=(t // tm,),
        in_specs=in_specs,
        out_specs=pl.BlockSpec((tm, d), lambda i: (i, 0)),
        out_shape=jax.ShapeDtypeStruct((t, d), F32),
        compiler_params=_params("parallel"),
        name="dense_ffn",
    )(*args)


def _store_packed(ref, y):
    half = y.shape[1] // 2
    bits = lax.bitcast_convert_type(y.astype(BF16).astype(F32), jnp.int32)
    lo = lax.shift_right_logical(bits[:, :half], 16)
    hi = jnp.bitwise_and(bits[:, half:], jnp.int32(-65536))
    packed = jnp.bitwise_or(hi, lo)
    w = half // PACK_HALVES
    for p in range(PACK_HALVES):
        ref[p] = packed[:, p * w:(p + 1) * w]


def _load_packed(ref):
    packed = jnp.concatenate([ref[p] for p in range(PACK_HALVES)], axis=1)
    lo = lax.bitcast_convert_type(lax.shift_left(packed, 16), F32)
    hi = lax.bitcast_convert_type(jnp.bitwise_and(packed, jnp.int32(-65536)), F32)
    return jnp.concatenate([lo, hi], axis=1)


def _router_kernel(x_ref, g_ref, r_ref, hp_ref, mi_ref, mw_ref, cnt_ref, carry_scr):
    @pl.when(pl.program_id(0) == 0)
    def _():
        carry_scr[...] = jnp.zeros_like(carry_scr)

    rows = x_ref.shape[0]
    h = _rms(x_ref[...], g_ref[...])
    _store_packed(hp_ref, h)
    r = r_ref[...]
    h_hi = h.astype(BF16)
    h_lo = (h - h_hi.astype(F32)).astype(BF16)
    r_hi = r.astype(BF16)
    r_lo = (r - r_hi.astype(F32)).astype(BF16)
    hi_terms = _dot(h_hi, jnp.concatenate([r_hi, r_lo], axis=1))
    logits = hi_terms[:, :LANES] + (hi_terms[:, LANES:] + _dot(h_lo, r_hi))
    lane = lax.broadcasted_iota(jnp.int32, logits.shape, 1)
    logits = jnp.where(lane < N_EXPERTS, logits, -jnp.inf)
    m1 = jnp.max(logits, axis=-1, keepdims=True)
    i1 = jnp.min(jnp.where(logits == m1, lane, LANES), axis=-1, keepdims=True)
    rest = jnp.where(lane == i1, -jnp.inf, logits)
    m2 = jnp.max(rest, axis=-1, keepdims=True)
    i2 = jnp.min(jnp.where(rest == m2, lane, LANES), axis=-1, keepdims=True)
    e2 = jnp.exp(m2 - m1)
    w1 = 1.0 / (1.0 + e2)
    w2 = e2 / (1.0 + e2)
    oh1 = (lane == i1).astype(F32)
    oh2 = (lane == i2).astype(F32)
    oh = oh1 + oh2
    earlier = (lax.broadcasted_iota(jnp.int32, (rows, rows), 0) > lax.broadcasted_iota(jnp.int32, (rows, rows), 1))
    before = _dot(earlier.astype(BF16), oh.astype(BF16)) + carry_scr[...]
    rank1 = jnp.sum(oh1 * before, axis=-1, keepdims=True)
    rank2 = jnp.sum(oh2 * before, axis=-1, keepdims=True)
    carry_scr[...] += jnp.sum(oh, axis=0, keepdims=True)
    cnt_ref[...] = carry_scr[...]
    meta = jnp.where(lane == 0, i1.astype(F32), jnp.where(lane == 1, i2.astype(F32), jnp.where(lane == 2, rank1,
                     jnp.where(lane == 3, rank2, 0.0))))
    mi_ref[...] = meta.T[0:SUBLANES, :]
    mw_ref[...] = jnp.where(lane == 0, w1, jnp.where(lane == 1, w2, 0.0))


def _router(x, g, router, *, tm):
    t, d = x.shape
    row = lambda i: (i, 0)
    fix = lambda i: (0, 0)
    return pl.pallas_call(
        _router_kernel,
        grid=(t // tm,),
        in_specs=[pl.BlockSpec((tm, d), row), pl.BlockSpec((1, d), fix), pl.BlockSpec((d, LANES), fix)],
        out_specs=[pl.BlockSpec((PACK_HALVES, tm, PACK_W), lambda i: (0, i, 0)),
                   pl.BlockSpec((SUBLANES, tm), lambda i: (0, i)),
                   pl.BlockSpec((tm, LANES), row), pl.BlockSpec((1, LANES), fix)],
        out_shape=[jax.ShapeDtypeStruct((PACK_HALVES, t, PACK_W), jnp.int32),
                   jax.ShapeDtypeStruct((SUBLANES, t), F32),
                   jax.ShapeDtypeStruct((t, LANES), F32), jax.ShapeDtypeStruct((1, LANES), F32)],
        scratch_shapes=[pltpu.VMEM((1, LANES), F32)],
        compiler_params=_params("arbitrary"),
        name="router",
    )(x, g, router)


def _grouped_kernel(te_ref, tv_ref, xs_ref, w1_ref, w3_ref, w2_ref, ys_ref, h_scr, acc_scr):
    i = pl.program_id(0)
    j = pl.program_id(1)
    last = pl.num_programs(1) - 1
    valid = tv_ref[i]
    active = valid > 0

    @pl.when(jnp.logical_and(active, j == 0))
    def _():
        h = _load_packed(xs_ref)
        row = lax.broadcasted_iota(jnp.int32, h.shape, 0)
        h_scr[...] = jnp.where(row < valid, h, 0.0).astype(BF16)
        acc_scr[...] = jnp.zeros_like(acc_scr)

    @pl.when(active)
    def _():
        h = h_scr[...]
        a = _dot(h, w1_ref[...])
        act = (a * _sigmoid(a) * _dot(h, w3_ref[...])).astype(BF16)
        acc_scr[...] += _dot(act, w2_ref[...])

    @pl.when(jnp.logical_and(active, j == last))
    def _():
        _store_packed(ys_ref, acc_scr[...])

    @pl.when(jnp.logical_and(jnp.logical_not(active), j == last))
    def _():
        ys_ref[...] = jnp.zeros_like(ys_ref)


def _grouped_swiglu(tile_expert, tile_valid, xs, w13, w2, *, tmg, tf):
    _, p, _ = xs.shape
    d = w2.shape[2]
    f = w2.shape[1]
    nj = f // tf

    def jj(i, j, tv):
        return jnp.where(tv[i] > 0, j, nj - 1)

    grid_spec = pltpu.PrefetchScalarGridSpec(
        num_scalar_prefetch=2,
        grid=(p // tmg, nj),
        in_specs=[pl.BlockSpec((PACK_HALVES, tmg, PACK_W), lambda i, j, te, tv: (0, i, 0)),
                  pl.BlockSpec((None, d, tf), lambda i, j, te, tv: (te[i], 0, jj(i, j, tv))),
                  pl.BlockSpec((None, d, tf), lambda i, j, te, tv: (te[i], 0, nj + jj(i, j, tv))),
                  pl.BlockSpec((None, tf, d), lambda i, j, te, tv: (te[i], jj(i, j, tv), 0))],
        out_specs=pl.BlockSpec((PACK_HALVES, tmg, PACK_W), lambda i, j, te, tv: (0, i, 0)),
        scratch_shapes=[pltpu.VMEM((tmg, d), BF16), pltpu.VMEM((tmg, d), F32)],
    )
    return pl.pallas_call(
        _grouped_kernel,
        grid_spec=grid_spec,
        out_shape=jax.ShapeDtypeStruct((PACK_HALVES, p, PACK_W), jnp.int32),
        compiler_params=_params("arbitrary", "arbitrary"),
        name="grouped_swiglu",
    )(tile_expert, tile_valid, xs, w13, w13, w2)


def _combine_kernel(*refs, final_norm):
    if final_norm:
        x_ref, mw_ref, y1_ref, y2_ref, gf_ref, o_ref = refs
    else:
        x_ref, mw_ref, y1_ref, y2_ref, o_ref = refs
    mw = mw_ref[...]
    out = x_ref[...] + mw[:, 0:1] * _load_packed(y1_ref) + mw[:, 1:2] * _load_packed(y2_ref)
    if final_norm:
        out = _rms(out, gf_ref[...])
    o_ref[...] = out


def _combine(x, mw, y12, g_final, *, tm):
    t, d = x.shape
    row = lambda i: (i, 0)
    in_specs = [pl.BlockSpec((tm, d), row), pl.BlockSpec((tm, LANES), row),
                pl.BlockSpec((PACK_HALVES, tm, PACK_W), lambda i: (0, i, 0)),
                pl.BlockSpec((PACK_HALVES, tm, PACK_W), lambda i: (0, t // tm + i, 0))]
    args = [x, mw, y12, y12]
    if g_final is not None:
        in_specs.append(pl.BlockSpec((1, d), lambda i: (0, 0)))
        args.append(g_final)
    return pl.pallas_call(
        functools.partial(_combine_kernel, final_norm=g_final is not None),
        grid=(t // tm,),
        in_specs=in_specs,
        out_specs=pl.BlockSpec((tm, d), row),
        out_shape=jax.ShapeDtypeStruct((t, d), F32),
        compiler_params=_params("parallel"),
        name="combine",
    )(*args)


def _gather_rows(table, idx):
    halves, n_rows, width = table.shape
    n = halves * idx.shape[0]
    assert n % SC_GATHER_WINDOW == 0
    flat_idx = jnp.concatenate([idx + h * n_rows for h in range(halves)]).reshape(1, n)
    mesh = plsc.VectorSubcoreMesh(core_axis_name="core", subcore_axis_name="subcore")

    @pl.kernel(out_type=jax.ShapeDtypeStruct((n, width), table.dtype), mesh=mesh)
    def gather(table_hbm, idx_hbm, out_hbm):
        def body(idx_vmem, out_vmem):
            pltpu.sync_copy(table_hbm.at[idx_vmem.at[0]], out_vmem)

        pltpu.emit_pipeline(
            body,
            grid=(n // SC_GATHER_WINDOW,),
            in_specs=[pl.BlockSpec((1, SC_GATHER_WINDOW), lambda i: (0, i))],
            out_specs=[pl.BlockSpec((SC_GATHER_WINDOW, width), lambda i: (i, 0))],
            core_axis_name=("core", "subcore"),
            dimension_semantics=(pltpu.PARALLEL,),
        )(idx_hbm, out_hbm)

    return gather(table.reshape(halves * n_rows, width), flat_idx).reshape(halves, idx.shape[0], width)


def _dispatch_rows(table, pos_list, n_out):
    halves, t, width = table.shape
    copies = len(pos_list)
    idx = jnp.concatenate([pos + h * n_out for h in range(halves) for pos in pos_list])
    n = idx.shape[0]
    win_t = t // SC_GATHER_WINDOW
    assert t % SC_GATHER_WINDOW == 0
    mesh = plsc.VectorSubcoreMesh(core_axis_name="core", subcore_axis_name="subcore")

    @pl.kernel(out_type=jax.ShapeDtypeStruct((halves * n_out, width), table.dtype), mesh=mesh)
    def scatter(table_hbm, idx_hbm, out_hbm):
        def body(rows_vmem, idx_vmem):
            pltpu.sync_copy(rows_vmem, out_hbm.at[idx_vmem.at[0]])

        pltpu.emit_pipeline(
            body,
            grid=(n // SC_GATHER_WINDOW,),
            in_specs=[pl.BlockSpec((SC_GATHER_WINDOW, width),
                                   lambda i: ((i // (copies * win_t)) * win_t + i % win_t, 0)),
                      pl.BlockSpec((1, SC_GATHER_WINDOW), lambda i: (0, i))],
            out_specs=[],
            core_axis_name=("core", "subcore"),
            dimension_semantics=(pltpu.PARALLEL,),
        )(table_hbm, idx_hbm)

    return scatter(table.reshape(halves * t, width), idx.reshape(1, n)).reshape(halves, n_out, width)


def _moe_sparse(x, g, router, w13, w2, g_final, *, tm, tmg, tf):
    t, d = x.shape
    hp, mi, mw, cnt = _router(x, g, router, tm=tm)
    counts = cnt[0, :N_EXPERTS].astype(jnp.int32)
    tiles_per_expert = (counts + tmg - 1) // tmg
    tile_end = jnp.cumsum(tiles_per_expert)
    group_start = (tile_end - tiles_per_expert) * tmg
    e1, e2, r1, r2 = (mi[r].astype(jnp.int32) for r in range(4))
    pos1 = group_start[e1] + r1
    pos2 = group_start[e2] + r2
    n_tiles = 2 * t // tmg + N_EXPERTS
    tile = jnp.arange(n_tiles, dtype=jnp.int32)
    tile_expert = jnp.minimum(jnp.sum(tile[:, None] >= tile_end[None, :], axis=1), N_EXPERTS - 1).astype(jnp.int32)
    rows_left = counts[tile_expert] - (tile - (tile_end - tiles_per_expert)[tile_expert]) * tmg
    tile_valid = jnp.where(tile < tile_end[-1], jnp.clip(rows_left, 0, tmg), 0).astype(jnp.int32)

    xs = _dispatch_rows(hp, [pos1, pos2], n_tiles * tmg)
    ys = _grouped_swiglu(tile_expert, tile_valid, xs, w13, w2, tmg=tmg, tf=tf)
    y12 = _gather_rows(ys, jnp.concatenate([pos1, pos2]))
    return _combine(x, mw, y12, g_final, tm=tm)


def _cast_kernel(w_ref, o_ref):
    o_ref[...] = w_ref[...].astype(o_ref.dtype)


def _cast_bf16(w, *, rows, cols):
    shape = w.shape
    w2 = w.reshape(-1, shape[-1])
    r, c = w2.shape
    out = pl.pallas_call(
        _cast_kernel,
        grid=(r // rows, c // cols),
        in_specs=[pl.BlockSpec((rows, cols), lambda i, j: (i, j))],
        out_specs=pl.BlockSpec((rows, cols), lambda i, j: (i, j)),
        out_shape=jax.ShapeDtypeStruct((r, c), BF16),
        compiler_params=_params("parallel", "parallel"),
        name="cast_bf16",
    )(w2)
    return out.reshape(shape)


def _split_w_in(w_in):
    sizes = (512, 512, 1024, GLA_RANK, 1024, 512, 512, 1024, M_HEADS, M_HEADS, 1024, 1024, 1024)
    parts, off = [], 0
    for n in sizes:
        parts.append(w_in[:, off:off + n])
        off += n
    g_q, g_k, g_v, g_lr, g_g, m_q, m_k, m_v, m_i, m_f, m_o, a_g, a_m = parts
    wz = jnp.concatenate([g_q, g_k, g_v, g_g, m_q, m_k, m_v, m_o, a_g, a_m], axis=1).astype(BF16)
    d = w_in.shape[0]
    gate_cols = jnp.stack([m_i, m_f], axis=2).reshape(d, 2 * M_HEADS)
    ws = jnp.concatenate([jnp.pad(g_lr, ((0, 0), (0, LANES - GLA_RANK))),
                          jnp.pad(gate_cols, ((0, 0), (0, LANES - 2 * M_HEADS)))], axis=1).astype(BF16)
    assert wz.shape == (d, Z_COLS) and ws.shape == (d, ZS_COLS)
    return wz, ws


def _pick(total, want):
    t = min(total, want)
    while total % t:
        t -= 1
    return t


def kernel(x, mem, norm_mix, w_in, gla_gk_up, gla_gk_bias, gla_norm, m_conv, m_gate_bias, m_norm, gla_proj, m_proj, w_out, norm_xattn, norm_mem, x_wq, x_wkv, x_wo, norm_ffn, ffn_w13, ffn_w2, moe_router, moe_w13, moe_w2, norm_final):
    b, s, d = x.shape
    m_len = mem.shape[1]
    t = b * s
    depth = norm_mix.shape[0]
    tm = _pick(t, 512)
    row2 = lambda a: a.reshape(1, -1)

    xt = x.reshape(t, d)
    mem_t = mem.reshape(b * m_len, d)
    for l in range(depth):
        last_layer = l == depth - 1
        wz, ws = _split_w_in(w_in[l])
        z, zs = _norm_matmul(xt, row2(norm_mix[l]), wz, ws, tm=tm, tn=2048)
        z3 = z.reshape(b, s, Z_COLS)
        zs3 = zs.reshape(b, s, ZS_COLS)
        up = jnp.pad(gla_gk_up[l], ((0, LANES - GLA_RANK), (0, 0))).astype(BF16)
        o_gla = _gla(z3, zs3, up, row2(gla_gk_bias[l]), row2(gla_norm[l]))
        gbias = jnp.pad(jnp.stack([m_gate_bias[l, :M_HEADS], m_gate_bias[l, M_HEADS:]], axis=1).reshape(1, -1),
                        ((0, 0), (0, LANES - 2 * M_HEADS)))
        arow, rep = _mlstm_gates(zs3, gbias)
        qk3 = _mlstm_conv(z3, m_conv[l], ts=_pick(s, 512))
        o_m = _mlstm(qk3, z3, arow, rep, row2(m_norm[l]))
        xt = _merge(xt, o_gla.reshape(t, d), o_m.reshape(t, d), z,
                    gla_proj[l].astype(BF16), m_proj[l].astype(BF16), w_out[l].astype(BF16), tm=tm)
        kv = _norm_matmul(mem_t, row2(norm_mem[l]), x_wkv[l].astype(BF16), tm=_pick(b * m_len, 512), tn=1024)
        xt = _xattn(xt.reshape(b, s, d), row2(norm_xattn[l]), x_wq[l].astype(BF16),
                    kv.reshape(b, m_len, 2 * d), x_wo[l].astype(BF16), tq=_pick(s, 1024)).reshape(t, d)
        g_final = row2(norm_final) if last_layer else None
        if l % 2 == 0:
            w13, w2 = ffn_w13[l // 2], ffn_w2[l // 2]
            f = w2.shape[0]
            fp = -(-f // MXU_TILE) * MXU_TILE
            w1 = jnp.pad(w13[:, :f], ((0, 0), (0, fp - f))).astype(BF16)
            w3 = jnp.pad(w13[:, f:], ((0, 0), (0, fp - f))).astype(BF16)
            w2p = jnp.pad(w2, ((0, fp - f), (0, 0))).astype(BF16)
            xt = _ffn(xt, row2(norm_ffn[l]), w1, w3, w2p, g_final, tm=tm)
        else:
            router = jnp.pad(moe_router[l // 2], ((0, 0), (0, LANES - N_EXPERTS)))
            w13 = _cast_bf16(moe_w13[l // 2], rows=CAST_ROWS, cols=d)
            w2 = _cast_bf16(moe_w2[l // 2], rows=CAST_ROWS, cols=d)
            xt = _moe_sparse(xt, row2(norm_ffn[l]), router, w13, w2, g_final, tm=tm, tmg=tm, tf=w2.shape[1] // 2)
    return xt.reshape(b, s, d)
```

```python
import functools

import jax
import jax.numpy as jnp
from jax import lax
from jax.experimental import pallas as pl
from jax.experimental.pallas import tpu as pltpu
from jax.experimental.pallas import tpu_sc as plsc

F32 = jnp.float32
BF16 = jnp.bfloat16

EPS = 1e-6
LOG2E = 1.4426950408889634
D_MODEL = 1024
GLA_HEADS = 4
GLA_DK = 128
GLA_DV = 256
GLA_RANK = 16
GLA_GATE_NORM = 16.0
GLA_LOG_DECAY_MIN = -1.0
GLA_CHUNK = 64
GLA_HEADS_PER_STEP = 4
GLA_UNROLL = 8
M_HEADS = 4
M_DK = 128
M_DV = 256
M_CHUNK = 128
M_HEADS_PER_STEP = 2
M_UNROLL = 8
CONV_W = 4
X_HEADS = 4
N_EXPERTS = 8
LANES = 128
MXU_TILE = 256
SUBLANES = 8
VMEM_LIMIT = 48 * 1024 * 1024
SC_GATHER_WINDOW = 128
PACK_HALVES = 2
PACK_W = D_MODEL // 2 // PACK_HALVES
CAST_ROWS = 1024

Z_GQ, Z_GK, Z_GV, Z_GG = 0, 512, 1024, 2048
Z_MQ, Z_MK, Z_MV, Z_MO = 3072, 3584, 4096, 5120
Z_AG, Z_AM = 6144, 7168
Z_COLS = 8192
ZS_COLS = 2 * LANES


def _params(*sem):
    return pltpu.CompilerParams(dimension_semantics=sem, vmem_limit_bytes=VMEM_LIMIT)


def _rms(x, g):
    return x * lax.rsqrt(jnp.mean(x * x, axis=-1, keepdims=True) + EPS) * g


def _log_sigmoid(u):
    return jnp.minimum(u, 0.0) - jnp.log(1.0 + jnp.exp(-jnp.abs(u)))


def _sigmoid(u):
    return 1.0 / (1.0 + jnp.exp(-u))


def _dot(a, b):
    return jnp.dot(a, b, preferred_element_type=F32)


def _dot_nt(a, b):
    return lax.dot_general(a, b, (((1,), (1,)), ((), ())), preferred_element_type=F32)


def _dot_tn(a, b):
    return lax.dot_general(a, b, (((0,), (0,)), ((), ())), preferred_element_type=F32)


def _cumsum_rows(x):
    n = x.shape[0]
    row = lax.broadcasted_iota(jnp.int32, x.shape, 0)
    s = 1
    while s < n:
        x = x + jnp.where(row >= s, pltpu.roll(x, s, 0), 0.0)
        s *= 2
    return x


def _norm_matmul_kernel(*refs, tn, with_small):
    if with_small:
        x_ref, g_ref, w_ref, ws_ref, z_ref, zs_ref = refs
    else:
        x_ref, g_ref, w_ref, z_ref = refs
    h = _rms(x_ref[...], g_ref[...]).astype(BF16)
    if with_small:
        zs_ref[...] = _dot(h, ws_ref[...])
    for c0 in range(0, z_ref.shape[1], tn):
        z_ref[:, c0:c0 + tn] = _dot(h, w_ref[:, c0:c0 + tn]).astype(z_ref.dtype)


def _norm_matmul(x, g, w, ws=None, *, tm, tn):
    t, d = x.shape
    n = w.shape[1]
    resident = pl.Buffered(1)
    in_specs = [pl.BlockSpec((tm, d), lambda i: (i, 0)),
                pl.BlockSpec((1, d), lambda i: (0, 0)),
                pl.BlockSpec((d, n), lambda i: (0, 0), pipeline_mode=resident)]
    out_specs = [pl.BlockSpec((tm, n), lambda i: (i, 0))]
    out_shape = [jax.ShapeDtypeStruct((t, n), BF16)]
    args = [x, g, w]
    if ws is not None:
        ns = ws.shape[1]
        in_specs.append(pl.BlockSpec((d, ns), lambda i: (0, 0), pipeline_mode=resident))
        out_specs.append(pl.BlockSpec((tm, ns), lambda i: (i, 0)))
        out_shape.append(jax.ShapeDtypeStruct((t, ns), F32))
        args.append(ws)
    out = pl.pallas_call(
        functools.partial(_norm_matmul_kernel, tn=tn, with_small=ws is not None),
        grid=(t // tm,),
        in_specs=in_specs, out_specs=out_specs, out_shape=out_shape,
        compiler_params=_params("parallel"),
        name="norm_matmul",
    )(*args)
    return out if ws is not None else out[0]


def _gla_kernel(q_ref, k_ref, v_ref, gg_ref, lr_ref, up_ref, gb_ref, nw_ref, o_ref, la_scr, st_scr):
    seq = q_ref.shape[0]
    c = GLA_CHUNK
    scale = GLA_DK ** -0.5

    u = _dot(lr_ref[...].astype(BF16), up_ref[...]) + gb_ref[...]
    la_scr[...] = jnp.maximum(_log_sigmoid(u) * (1.0 / GLA_GATE_NORM), GLA_LOG_DECAY_MIN)
    st_scr[...] = jnp.zeros_like(st_scr)
    causal = (lax.broadcasted_iota(jnp.int32, (c, c), 0) >= lax.broadcasted_iota(jnp.int32, (c, c), 1))

    def body(n, carry):
        r0 = pl.multiple_of(n * c, c)
        rows = pl.ds(r0, c)
        for hh in range(GLA_HEADS_PER_STEP):
            kcols = slice(hh * GLA_DK, (hh + 1) * GLA_DK)
            vcols = slice(hh * GLA_DV, (hh + 1) * GLA_DV)
            cum = _cumsum_rows(la_scr[rows, kcols])
            cum_last = cum[c - 1:c, :]
            q = q_ref[rows, kcols].astype(F32) * scale
            k = k_ref[rows, kcols].astype(F32)
            v = v_ref[rows, vcols]
            q_dec = (q * jnp.exp(cum)).astype(BF16)
            k_inv = (k * jnp.exp(-cum)).astype(BF16)
            k_end = (k * jnp.exp(cum_last - cum)).astype(BF16)
            scores = jnp.where(causal, _dot_nt(q_dec, k_inv), 0.0).astype(BF16)
            st = st_scr[hh]
            o = _dot(scores, v) + _dot_nt(q_dec, st.astype(BF16))
            st_scr[hh] = st * jnp.exp(cum_last) + _dot_tn(v, k_end)
            gate = gg_ref[rows, vcols].astype(F32)
            o_ref[rows, vcols] = (_rms(o, nw_ref[:, vcols]) * (gate * _sigmoid(gate))).astype(o_ref.dtype)
        return carry

    lax.fori_loop(0, seq // c, body, 0, unroll=GLA_UNROLL)


def _gla(z3, zs3, up, gb, nw):
    b, s, _ = z3.shape
    hps = GLA_HEADS_PER_STEP
    dk, dv = hps * GLA_DK, hps * GLA_DV
    qb, kb, vb, gb_ = Z_GQ // dk, Z_GK // dk, Z_GV // dv, Z_GG // dv
    return pl.pallas_call(
        _gla_kernel,
        grid=(b, GLA_HEADS // hps),
        in_specs=[pl.BlockSpec((None, s, dk), lambda i, h: (i, 0, qb + h)),
                  pl.BlockSpec((None, s, dk), lambda i, h: (i, 0, kb + h)),
                  pl.BlockSpec((None, s, dv), lambda i, h: (i, 0, vb + h)),
                  pl.BlockSpec((None, s, dv), lambda i, h: (i, 0, gb_ + h)),
                  pl.BlockSpec((None, s, LANES), lambda i, h: (i, 0, 0)),
                  pl.BlockSpec((LANES, dk), lambda i, h: (0, h)),
                  pl.BlockSpec((1, dk), lambda i, h: (0, h)),
                  pl.BlockSpec((1, dv), lambda i, h: (0, h))],
        out_specs=pl.BlockSpec((None, s, dv), lambda i, h: (i, 0, h)),
        out_shape=jax.ShapeDtypeStruct((b, s, GLA_HEADS * GLA_DV), BF16),
        scratch_shapes=[pltpu.VMEM((s, dk), F32), pltpu.VMEM((hps, GLA_DV, GLA_DK), F32)],
        compiler_params=_params("parallel", "arbitrary"),
        name="gla",
    )(z3, z3, z3, z3, zs3, up, gb, nw)


def _chunk_scan_lanes(x, combine, identity, chunk):
    lane_in_chunk = lax.broadcasted_iota(jnp.int32, x.shape, 1) % chunk
    step = 1
    while step < chunk:
        x = combine(x, jnp.where(lane_in_chunk >= step, pltpu.roll(x, step, 1), identity))
        step *= 2
    return x


def _replicate_rows(x, row):
    hi = x.astype(BF16)
    r1 = x - hi.astype(F32)
    mid = r1.astype(BF16)
    lo = (r1 - mid.astype(F32)).astype(BF16)
    parts = jnp.concatenate([hi, mid, lo, jnp.zeros_like(hi)], axis=0)
    pick = lax.broadcasted_iota(jnp.int32, (4 * SUBLANES, LANES), 0) % SUBLANES == row
    return _dot_tn(parts, jnp.where(pick, 1.0, 0.0).astype(BF16))


def _mlstm_gates_kernel(gt_ref, gb_ref, arow_ref, rep_ref):
    c = M_CHUNK
    gates_t = (gt_ref[...] + gb_ref[...]).T
    top = gates_t[0:SUBLANES, :]
    odd = lax.broadcasted_iota(jnp.int32, top.shape, 0) % 2 == 1
    lf = jnp.where(odd, _log_sigmoid(top), 0.0)
    b_odd = _chunk_scan_lanes(lf, jnp.add, 0.0, c)
    a_even = top - pltpu.roll(b_odd, SUBLANES - 1, 0)
    amax_even = _chunk_scan_lanes(jnp.where(odd, -jnp.inf, a_even), jnp.maximum, -jnp.inf, c)
    for h in range(M_HEADS):
        arow_ref[h] = pltpu.roll(a_even, (SUBLANES - 2 * h) % SUBLANES, 0)
        rep_ref[h, 0] = _replicate_rows(b_odd, 2 * h + 1)
        rep_ref[h, 1] = _replicate_rows(a_even, 2 * h)
        rep_ref[h, 2] = _replicate_rows(jnp.where(odd, 0.0, amax_even), 2 * h)


def _mlstm_gates(zs3, gbias):
    b, s, _ = zs3.shape
    return pl.pallas_call(
        _mlstm_gates_kernel,
        grid=(b,),
        in_specs=[pl.BlockSpec((None, s, LANES), lambda i: (i, 0, 1)),
                  pl.BlockSpec((1, LANES), lambda i: (0, 0))],
        out_specs=[pl.BlockSpec((None, M_HEADS, SUBLANES, s), lambda i: (i, 0, 0, 0)),
                   pl.BlockSpec((None, M_HEADS, 3, s, LANES), lambda i: (i, 0, 0, 0, 0))],
        out_shape=[jax.ShapeDtypeStruct((b, M_HEADS, SUBLANES, s), F32),
                   jax.ShapeDtypeStruct((b, M_HEADS, 3, s, LANES), F32)],
        compiler_params=_params("parallel"),
        name="mlstm_gates",
    )(zs3, gbias)


def _mlstm_conv_kernel(cur_ref, prev_ref, w_ref, shift_ref, o_ref):
    c = M_CHUNK
    d = cur_ref.shape[1]
    prev = jnp.where(pl.program_id(1) > 0, prev_ref[...], jnp.zeros_like(prev_ref))
    lane = lax.broadcasted_iota(jnp.int32, (1, d), 1)
    col_scale = jnp.where(lane < d // 2, M_DK ** -0.5, 1.0)
    for i in range(cur_ref.shape[0] // c):
        cur = cur_ref[i * c:(i + 1) * c, :]
        before = prev if i == 0 else cur_ref[(i - 1) * c:i * c, :]
        shifted = _dot(shift_ref[...], jnp.concatenate([before, cur], axis=0))
        acc = w_ref[CONV_W - 1:CONV_W, :] * cur.astype(F32)
        for s in range(1, CONV_W):
            acc = acc + w_ref[CONV_W - 1 - s:CONV_W - s, :] * shifted[(s - 1) * c:s * c, :]
        o_ref[i * c:(i + 1) * c, :] = (acc * _sigmoid(acc) * col_scale).astype(o_ref.dtype)


def _mlstm_conv(z3, conv, *, ts):
    b, s, _ = z3.shape
    c = M_CHUNK
    d = conv.shape[1]
    blk = Z_MQ // d
    assert Z_MQ % d == 0 and Z_MK == Z_MQ + d // 2
    t_idx = jnp.arange(c)[:, None]
    col = jnp.arange(2 * c)[None, :]
    shift = jnp.concatenate([(col == c + t_idx - s) for s in range(1, CONV_W)], axis=0).astype(BF16)
    return pl.pallas_call(
        _mlstm_conv_kernel,
        grid=(b, s // ts),
        in_specs=[pl.BlockSpec((None, ts, d), lambda i, j: (i, j, blk)),
                  pl.BlockSpec((None, c, d), lambda i, j: (i, jnp.maximum(j * (ts // c) - 1, 0), blk)),
                  pl.BlockSpec((CONV_W, d), lambda i, j: (0, 0)),
                  pl.BlockSpec(((CONV_W - 1) * c, 2 * c), lambda i, j: (0, 0))],
        out_specs=pl.BlockSpec((None, ts, d), lambda i, j: (i, j, 0)),
        out_shape=jax.ShapeDtypeStruct((b, s, d), BF16),
        compiler_params=_params("parallel", "parallel"),
        name="mlstm_conv",
    )(z3, z3, conv, shift)


def _mlstm_kernel(q_ref, k_ref, v_ref, og_ref, arow_ref, rep_ref, nw_ref, o_ref, st_scr, m_scr):
    seq = q_ref.shape[0]
    c = M_CHUNK
    st_scr[...] = jnp.zeros_like(st_scr)
    m_scr[...] = jnp.zeros_like(m_scr)
    causal = (lax.broadcasted_iota(jnp.int32, (c, c), 0) >= lax.broadcasted_iota(jnp.int32, (c, c), 1))
    ones_cols = jnp.ones((c, LANES), BF16)
    ones_dv = jnp.ones((M_DV, LANES), BF16)

    def wide(x, n):
        return jnp.concatenate([x] * n, axis=1)

    def body(n, carry):
        r0 = pl.multiple_of(n * c, c)
        rows = pl.ds(r0, c)
        for hh in range(M_HEADS_PER_STEP):
            kcols = slice(hh * M_DK, (hh + 1) * M_DK)
            vcols = slice(hh * M_DV, (hh + 1) * M_DV)
            qb = q_ref[rows, kcols]
            kb = k_ref[rows, kcols]
            v_ext = jnp.concatenate([v_ref[rows, vcols], ones_cols], axis=1)
            b = rep_ref[hh, 0, rows, :]
            a = rep_ref[hh, 1, rows, :]
            amax = rep_ref[hh, 2, rows, :]
            a_row = arow_ref[hh, 0:1, rows]
            m_prev = m_scr[hh]
            st = st_scr[hh]

            dmat = jnp.where(causal, b + a_row, -jnp.inf)
            m_inter = b + m_prev
            m_t = jnp.maximum(m_inter, b + amax)
            w_ts = jnp.exp(dmat - m_t) * _dot_nt(qb, kb)
            s_inter = jnp.exp(m_inter - m_t)
            nd = _dot(w_ts.astype(BF16), v_ext) + wide(s_inter, 3) * _dot(qb, st.astype(BF16))
            inv = 1.0 / jnp.maximum(jnp.abs(nd[:, M_DV:]), jnp.exp(-m_t))
            h = nd[:, :M_DV] * wide(inv, 2)

            b_last = b[c - 1:c, :]
            m_new = jnp.maximum(b_last + m_prev, b_last + amax[c - 1:c, :])
            kw = kb.astype(F32) * jnp.exp(b_last + a - m_new)
            st_scr[hh] = wide(jnp.exp(b_last + m_prev - m_new), 3) * st + _dot_tn(kw.astype(BF16), v_ext)
            m_scr[hh] = m_new

            mean_sq = _dot((h * h).astype(BF16), ones_dv) * (1.0 / M_DV)
            gate = og_ref[rows, vcols].astype(F32)
            y = h * wide(lax.rsqrt(mean_sq + EPS), 2) * nw_ref[:, vcols]
            o_ref[rows, vcols] = (y * _sigmoid(gate)).astype(o_ref.dtype)
        return carry

    lax.fori_loop(0, seq // c, body, 0, unroll=M_UNROLL)


def _mlstm(qk3, z3, arow, rep, nw):
    b, s, _ = z3.shape
    hps = M_HEADS_PER_STEP
    dk, dv = hps * M_DK, hps * M_DV
    vb, ob = Z_MV // dv, Z_MO // dv
    return pl.pallas_call(
        _mlstm_kernel,
        grid=(b, M_HEADS // hps),
        in_specs=[pl.BlockSpec((None, s, dk), lambda i, h: (i, 0, h)),
                  pl.BlockSpec((None, s, dk), lambda i, h: (i, 0, M_HEADS // hps + h)),
                  pl.BlockSpec((None, s, dv), lambda i, h: (i, 0, vb + h)),
                  pl.BlockSpec((None, s, dv), lambda i, h: (i, 0, ob + h)),
                  pl.BlockSpec((None, hps, SUBLANES, s), lambda i, h: (i, h, 0, 0)),
                  pl.BlockSpec((None, hps, 3, s, LANES), lambda i, h: (i, h, 0, 0, 0)),
                  pl.BlockSpec((1, dv), lambda i, h: (0, h))],
        out_specs=pl.BlockSpec((None, s, dv), lambda i, h: (i, 0, h)),
        out_shape=jax.ShapeDtypeStruct((b, s, M_HEADS * M_DV), BF16),
        scratch_shapes=[pltpu.VMEM((hps, M_DK, M_DV + LANES), F32), pltpu.VMEM((hps, 1, LANES), F32)],
        compiler_params=_params("parallel", "arbitrary"),
        name="mlstm",
    )(qk3, qk3, z3, z3, arow, rep, nw)


def _merge_kernel(x_ref, og_ref, om_ref, ag_ref, am_ref, wg_ref, wm_ref, wo_ref, o_ref):
    merged = (_sigmoid(ag_ref[...].astype(F32)) * _dot(og_ref[...], wg_ref[...])
              + _sigmoid(am_ref[...].astype(F32)) * _dot(om_ref[...], wm_ref[...]))
    o_ref[...] = x_ref[...] + _dot(merged.astype(BF16), wo_ref[...])


def _merge(x, o_gla, o_m, z, wg, wm, wo, *, tm):
    t, d = x.shape
    row = lambda i: (i, 0)
    full = lambda i: (0, 0)
    return pl.pallas_call(
        _merge_kernel,
        grid=(t // tm,),
        in_specs=[pl.BlockSpec((tm, d), row), pl.BlockSpec((tm, d), row), pl.BlockSpec((tm, d), row),
                  pl.BlockSpec((tm, d), lambda i: (i, Z_AG // D_MODEL)),
                  pl.BlockSpec((tm, d), lambda i: (i, Z_AM // D_MODEL)),
                  pl.BlockSpec((d, d), full), pl.BlockSpec((d, d), full), pl.BlockSpec((d, d), full)],
        out_specs=pl.BlockSpec((tm, d), row),
        out_shape=jax.ShapeDtypeStruct((t, d), F32),
        compiler_params=_params("parallel"),
        name="merge",
    )(x, o_gla, o_m, z, z, wg, wm, wo)


def _xattn_kernel(x_ref, g_ref, wq_ref, kv_ref, wo_ref, o_ref):
    d = x_ref.shape[1]
    dh = d // X_HEADS
    x = x_ref[...]
    q = _dot(_rms(x, g_ref[...]).astype(BF16), wq_ref[...]).astype(BF16)
    heads = []
    ones_m = jnp.ones((kv_ref.shape[0], LANES), BF16)
    for h in range(X_HEADS):
        k = kv_ref[:, h * dh:(h + 1) * dh]
        v = kv_ref[:, d + h * dh:d + (h + 1) * dh]
        s = _dot_nt(q[:, h * dh:(h + 1) * dh], k)
        p = jnp.exp2((s - jnp.max(s, axis=-1, keepdims=True)) * (dh ** -0.5 * LOG2E)).astype(BF16)
        inv_l = 1.0 / _dot(p, ones_m)
        heads.append((_dot(p, v) * jnp.concatenate([inv_l] * (dh // LANES), axis=1)).astype(BF16))
    o_ref[...] = x + _dot(jnp.concatenate(heads, axis=-1), wo_ref[...])


def _xattn(x3, g, wq, kv, wo, *, tq):
    b, s, d = x3.shape
    m = kv.shape[1]
    return pl.pallas_call(
        _xattn_kernel,
        grid=(b, s // tq),
        in_specs=[pl.BlockSpec((None, tq, d), lambda i, j: (i, j, 0)),
                  pl.BlockSpec((1, d), lambda i, j: (0, 0)),
                  pl.BlockSpec((d, d), lambda i, j: (0, 0)),
                  pl.BlockSpec((None, m, 2 * d), lambda i, j: (i, 0, 0)),
                  pl.BlockSpec((d, d), lambda i, j: (0, 0))],
        out_specs=pl.BlockSpec((None, tq, d), lambda i, j: (i, j, 0)),
        out_shape=jax.ShapeDtypeStruct((b, s, d), F32),
        compiler_params=_params("parallel", "parallel"),
        name="xattn",
    )(x3, g, wq, kv, wo)


def _ffn_kernel(*refs, chunks, final_norm):
    if final_norm:
        x_ref, g_ref, w1_ref, w3_ref, w2_ref, gf_ref, o_ref = refs
    else:
        x_ref, g_ref, w1_ref, w3_ref, w2_ref, o_ref = refs
    x = x_ref[...]
    h = _rms(x, g_ref[...]).astype(BF16)
    out = x
    for c0, c1 in chunks:
        a = _dot(h, w1_ref[:, c0:c1])
        act = (a * _sigmoid(a) * _dot(h, w3_ref[:, c0:c1])).astype(BF16)
        out = out + _dot(act, w2_ref[c0:c1, :])
    if final_norm:
        out = _rms(out, gf_ref[...])
    o_ref[...] = out


def _ffn(x, g, w1, w3, w2, g_final, *, tm):
    t, d = x.shape
    f = w2.shape[0]
    assert f % MXU_TILE == 0
    half = (f // MXU_TILE + 1) // 2 * MXU_TILE
    chunks = ((0, half), (half, f))
    resident = pl.Buffered(1)
    in_specs = [pl.BlockSpec((tm, d), lambda i: (i, 0)),
                pl.BlockSpec((1, d), lambda i: (0, 0)),
                pl.BlockSpec((d, f), lambda i: (0, 0), pipeline_mode=resident),
                pl.BlockSpec((d, f), lambda i: (0, 0), pipeline_mode=resident),
                pl.BlockSpec((f, d), lambda i: (0, 0), pipeline_mode=resident)]
    args = [x, g, w1, w3, w2]
    if g_final is not None:
        in_specs.append(pl.BlockSpec((1, d), lambda i: (0, 0)))
        args.append(g_final)
    return pl.pallas_call(
        functools.partial(_ffn_kernel, chunks=chunks, final_norm=g_final is not None),
        grid=(t // tm,),
        in_specs=in_specs,
        out_specs=pl.BlockSpec((tm, d), lambda i: (i, 0)),
        out_shape=jax.ShapeDtypeStruct((t, d), F32),
        compiler_params=_params("parallel"),
        name="dense_ffn",
    )(*args)


def _store_packed(ref, y):
    half = y.shape[1] // 2
    bits = lax.bitcast_convert_type(y.astype(BF16).astype(F32), jnp.int32)
    lo = lax.shift_right_logical(bits[:, :half], 16)
    hi = jnp.bitwise_and(bits[:, half:], jnp.int32(-65536))
    packed = jnp.bitwise_or(hi, lo)
    w = half // PACK_HALVES
    for p in range(PACK_HALVES):
        ref[p] = packed[:, p * w:(p + 1) * w]


def _load_packed(ref):
    packed = jnp.concatenate([ref[p] for p in range(PACK_HALVES)], axis=1)
    lo = lax.bitcast_convert_type(lax.shift_left(packed, 16), F32)
    hi = lax.bitcast_convert_type(jnp.bitwise_and(packed, jnp.int32(-65536)), F32)
    return jnp.concatenate([lo, hi], axis=1)


def _router_kernel(x_ref, g_ref, r_ref, hp_ref, mi_ref, mw_ref, cnt_ref, carry_scr):
    @pl.when(pl.program_id(0) == 0)
    def _():
        carry_scr[...] = jnp.zeros_like(carry_scr)

    rows = x_ref.shape[0]
    h = _rms(x_ref[...], g_ref[...])
    _store_packed(hp_ref, h)
    r = r_ref[...]
    h_hi = h.astype(BF16)
    h_lo = (h - h_hi.astype(F32)).astype(BF16)
    r_hi = r.astype(BF16)
    r_lo = (r - r_hi.astype(F32)).astype(BF16)
    hi_terms = _dot(h_hi, jnp.concatenate([r_hi, r_lo], axis=1))
    logits = hi_terms[:, :LANES] + (hi_terms[:, LANES:] + _dot(h_lo, r_hi))
    lane = lax.broadcasted_iota(jnp.int32, logits.shape, 1)
    logits = jnp.where(lane < N_EXPERTS, logits, -jnp.inf)
    m1 = jnp.max(logits, axis=-1, keepdims=True)
    i1 = jnp.min(jnp.where(logits == m1, lane, LANES), axis=-1, keepdims=True)
    rest = jnp.where(lane == i1, -jnp.inf, logits)
    m2 = jnp.max(rest, axis=-1, keepdims=True)
    i2 = jnp.min(jnp.where(rest == m2, lane, LANES), axis=-1, keepdims=True)
    e2 = jnp.exp(m2 - m1)
    w1 = 1.0 / (1.0 + e2)
    w2 = e2 / (1.0 + e2)
    oh1 = (lane == i1).astype(F32)
    oh2 = (lane == i2).astype(F32)
    oh = oh1 + oh2
    earlier = (lax.broadcasted_iota(jnp.int32, (rows, rows), 0) > lax.broadcasted_iota(jnp.int32, (rows, rows), 1))
    before = _dot(earlier.astype(BF16), oh.astype(BF16)) + carry_scr[...]
    rank1 = jnp.sum(oh1 * before, axis=-1, keepdims=True)
    rank2 = jnp.sum(oh2 * before, axis=-1, keepdims=True)
    carry_scr[...] += jnp.sum(oh, axis=0, keepdims=True)
    cnt_ref[...] = carry_scr[...]
    meta = jnp.where(lane == 0, i1.astype(F32), jnp.where(lane == 1, i2.astype(F32), jnp.where(lane == 2, rank1,
                     jnp.where(lane == 3, rank2, 0.0))))
    mi_ref[...] = meta.T[0:SUBLANES, :]
    mw_ref[...] = jnp.where(lane == 0, w1, jnp.where(lane == 1, w2, 0.0))


def _router(x, g, router, *, tm):
    t, d = x.shape
    row = lambda i: (i, 0)
    fix = lambda i: (0, 0)
    return pl.pallas_call(
        _router_kernel,
        grid=(t // tm,),
        in_specs=[pl.BlockSpec((tm, d), row), pl.BlockSpec((1, d), fix), pl.BlockSpec((d, LANES), fix)],
        out_specs=[pl.BlockSpec((PACK_HALVES, tm, PACK_W), lambda i: (0, i, 0)),
                   pl.BlockSpec((SUBLANES, tm), lambda i: (0, i)),
                   pl.BlockSpec((tm, LANES), row), pl.BlockSpec((1, LANES), fix)],
        out_shape=[jax.ShapeDtypeStruct((PACK_HALVES, t, PACK_W), jnp.int32),
                   jax.ShapeDtypeStruct((SUBLANES, t), F32),
                   jax.ShapeDtypeStruct((t, LANES), F32), jax.ShapeDtypeStruct((1, LANES), F32)],
        scratch_shapes=[pltpu.VMEM((1, LANES), F32)],
        compiler_params=_params("arbitrary"),
        name="router",
    )(x, g, router)


def _grouped_kernel(te_ref, tv_ref, xs_ref, w1_ref, w3_ref, w2_ref, ys_ref, h_scr, acc_scr):
    i = pl.program_id(0)
    j = pl.program_id(1)
    last = pl.num_programs(1) - 1
    valid = tv_ref[i]
    active = valid > 0

    @pl.when(jnp.logical_and(active, j == 0))
    def _():
        h = _load_packed(xs_ref)
        row = lax.broadcasted_iota(jnp.int32, h.shape, 0)
        h_scr[...] = jnp.where(row < valid, h, 0.0).astype(BF16)
        acc_scr[...] = jnp.zeros_like(acc_scr)

    @pl.when(active)
    def _():
        h = h_scr[...]
        a = _dot(h, w1_ref[...])
        act = (a * _sigmoid(a) * _dot(h, w3_ref[...])).astype(BF16)
        acc_scr[...] += _dot(act, w2_ref[...])

    @pl.when(jnp.logical_and(active, j == last))
    def _():
        _store_packed(ys_ref, acc_scr[...])

    @pl.when(jnp.logical_and(jnp.logical_not(active), j == last))
    def _():
        ys_ref[...] = jnp.zeros_like(ys_ref)


def _grouped_swiglu(tile_expert, tile_valid, xs, w13, w2, *, tmg, tf):
    _, p, _ = xs.shape
    d = w2.shape[2]
    f = w2.shape[1]
    nj = f // tf

    def jj(i, j, tv):
        return jnp.where(tv[i] > 0, j, nj - 1)

    grid_spec = pltpu.PrefetchScalarGridSpec(
        num_scalar_prefetch=2,
        grid=(p // tmg, nj),
        in_specs=[pl.BlockSpec((PACK_HALVES, tmg, PACK_W), lambda i, j, te, tv: (0, i, 0)),
                  pl.BlockSpec((None, d, tf), lambda i, j, te, tv: (te[i], 0, jj(i, j, tv))),
                  pl.BlockSpec((None, d, tf), lambda i, j, te, tv: (te[i], 0, nj + jj(i, j, tv))),
                  pl.BlockSpec((None, tf, d), lambda i, j, te, tv: (te[i], jj(i, j, tv), 0))],
        out_specs=pl.BlockSpec((PACK_HALVES, tmg, PACK_W), lambda i, j, te, tv: (0, i, 0)),
        scratch_shapes=[pltpu.VMEM((tmg, d), BF16), pltpu.VMEM((tmg, d), F32)],
    )
    return pl.pallas_call(
        _grouped_kernel,
        grid_spec=grid_spec,
        out_shape=jax.ShapeDtypeStruct((PACK_HALVES, p, PACK_W), jnp.int32),
        compiler_params=_params("arbitrary", "arbitrary"),
        name="grouped_swiglu",
    )(tile_expert, tile_valid, xs, w13, w13, w2)


def _combine_kernel(*refs, final_norm):
    if final_norm:
        x_ref, mw_ref, y1_ref, y2_ref, gf_ref, o_ref = refs
    else:
        x_ref, mw_ref, y1_ref, y2_ref, o_ref = refs
    mw = mw_ref[...]
    out = x_ref[...] + mw[:, 0:1] * _load_packed(y1_ref) + mw[:, 1:2] * _load_packed(y2_ref)
    if final_norm:
        out = _rms(out, gf_ref[...])
    o_ref[...] = out


def _combine(x, mw, y12, g_final, *, tm):
    t, d = x.shape
    row = lambda i: (i, 0)
    in_specs = [pl.BlockSpec((tm, d), row), pl.BlockSpec((tm, LANES), row),
                pl.BlockSpec((PACK_HALVES, tm, PACK_W), lambda i: (0, i, 0)),
                pl.BlockSpec((PACK_HALVES, tm, PACK_W), lambda i: (0, t // tm + i, 0))]
    args = [x, mw, y12, y12]
    if g_final is not None:
        in_specs.append(pl.BlockSpec((1, d), lambda i: (0, 0)))
        args.append(g_final)
    return pl.pallas_call(
        functools.partial(_combine_kernel, final_norm=g_final is not None),
        grid=(t // tm,),
        in_specs=in_specs,
        out_specs=pl.BlockSpec((tm, d), row),
        out_shape=jax.ShapeDtypeStruct((t, d), F32),
        compiler_params=_params("parallel"),
        name="combine",
    )(*args)


def _gather_rows(table, idx):
    halves, n_rows, width = table.shape
    n = halves * idx.shape[0]
    assert n % SC_GATHER_WINDOW == 0
    flat_idx = jnp.concatenate([idx + h * n_rows for h in range(halves)]).reshape(1, n)
    mesh = plsc.VectorSubcoreMesh(core_axis_name="core", subcore_axis_name="subcore")

    @pl.kernel(out_type=jax.ShapeDtypeStruct((n, width), table.dtype), mesh=mesh)
    def gather(table_hbm, idx_hbm, out_hbm):
        def body(idx_vmem, out_vmem):
            pltpu.sync_copy(table_hbm.at[idx_vmem.at[0]], out_vmem)

        pltpu.emit_pipeline(
            body,
            grid=(n // SC_GATHER_WINDOW,),
            in_specs=[pl.BlockSpec((1, SC_GATHER_WINDOW), lambda i: (0, i))],
            out_specs=[pl.BlockSpec((SC_GATHER_WINDOW, width), lambda i: (i, 0))],
            core_axis_name=("core", "subcore"),
            dimension_semantics=(pltpu.PARALLEL,),
        )(idx_hbm, out_hbm)

    return gather(table.reshape(halves * n_rows, width), flat_idx).reshape(halves, idx.shape[0], width)


def _dispatch_rows(table, pos_list, n_out):
    halves, t, width = table.shape
    copies = len(pos_list)
    idx = jnp.concatenate([pos + h * n_out for h in range(halves) for pos in pos_list])
    n = idx.shape[0]
    win_t = t // SC_GATHER_WINDOW
    assert t % SC_GATHER_WINDOW == 0
    mesh = plsc.VectorSubcoreMesh(core_axis_name="core", subcore_axis_name="subcore")

    @pl.kernel(out_type=jax.ShapeDtypeStruct((halves * n_out, width), table.dtype), mesh=mesh)
    def scatter(table_hbm, idx_hbm, out_hbm):
        def body(rows_vmem, idx_vmem):
            pltpu.sync_copy(rows_vmem, out_hbm.at[idx_vmem.at[0]])

        pltpu.emit_pipeline(
            body,
            grid=(n // SC_GATHER_WINDOW,),
            in_specs=[pl.BlockSpec((SC_GATHER_WINDOW, width),
                                   lambda i: ((i // (copies * win_t)) * win_t + i % win_t, 0)),
                      pl.BlockSpec((1, SC_GATHER_WINDOW), lambda i: (0, i))],
            out_specs=[],
            core_axis_name=("core", "subcore"),
            dimension_semantics=(pltpu.PARALLEL,),
        )(table_hbm, idx_hbm)

    return scatter(table.reshape(halves * t, width), idx.reshape(1, n)).reshape(halves, n_out, width)


def _moe_sparse(x, g, router, w13, w2, g_final, *, tm, tmg, tf):
    t, d = x.shape
    hp, mi, mw, cnt = _router(x, g, router, tm=tm)
    counts = cnt[0, :N_EXPERTS].astype(jnp.int32)
    tiles_per_expert = (counts + tmg - 1) // tmg
    tile_end = jnp.cumsum(tiles_per_expert)
    group_start = (tile_end - tiles_per_expert) * tmg
    e1, e2, r1, r2 = (mi[r].astype(jnp.int32) for r in range(4))
    pos1 = group_start[e1] + r1
    pos2 = group_start[e2] + r2
    n_tiles = 2 * t // tmg + N_EXPERTS
    tile = jnp.arange(n_tiles, dtype=jnp.int32)
    tile_expert = jnp.minimum(jnp.sum(tile[:, None] >= tile_end[None, :], axis=1), N_EXPERTS - 1).astype(jnp.int32)
    rows_left = counts[tile_expert] - (tile - (tile_end - tiles_per_expert)[tile_expert]) * tmg
    tile_valid = jnp.where(tile < tile_end[-1], jnp.clip(rows_left, 0, tmg), 0).astype(jnp.int32)

    xs = _dispatch_rows(hp, [pos1, pos2], n_tiles * tmg)
    ys = _grouped_swiglu(tile_expert, tile_valid, xs, w13, w2, tmg=tmg, tf=tf)
    y12 = _gather_rows(ys, jnp.concatenate([pos1, pos2]))
    return _combine(x, mw, y12, g_final, tm=_pick(t, 2 * tm))


def _cast_kernel(w_ref, o_ref):
    o_ref[...] = w_ref[...].astype(o_ref.dtype)


def _cast_bf16(w, *, rows, cols):
    shape = w.shape
    w2 = w.reshape(-1, shape[-1])
    r, c = w2.shape
    out = pl.pallas_call(
        _cast_kernel,
        grid=(r // rows, c // cols),
        in_specs=[pl.BlockSpec((rows, cols), lambda i, j: (i, j))],
        out_specs=pl.BlockSpec((rows, cols), lambda i, j: (i, j)),
        out_shape=jax.ShapeDtypeStruct((r, c), BF16),
        compiler_params=_params("parallel", "parallel"),
        name="cast_bf16",
    )(w2)
    return out.reshape(shape)


def _split_w_in(w_in):
    sizes = (512, 512, 1024, GLA_RANK, 1024, 512, 512, 1024, M_HEADS, M_HEADS, 1024, 1024, 1024)
    parts, off = [], 0
    for n in sizes:
        parts.append(w_in[:, off:off + n])
        off += n
    g_q, g_k, g_v, g_lr, g_g, m_q, m_k, m_v, m_i, m_f, m_o, a_g, a_m = parts
    wz = jnp.concatenate([g_q, g_k, g_v, g_g, m_q, m_k, m_v, m_o, a_g, a_m], axis=1).astype(BF16)
    d = w_in.shape[0]
    gate_cols = jnp.stack([m_i, m_f], axis=2).reshape(d, 2 * M_HEADS)
    ws = jnp.concatenate([jnp.pad(g_lr, ((0, 0), (0, LANES - GLA_RANK))),
                          jnp.pad(gate_cols, ((0, 0), (0, LANES - 2 * M_HEADS)))], axis=1).astype(BF16)
    assert wz.shape == (d, Z_COLS) and ws.shape == (d, ZS_COLS)
    return wz, ws


def _pick(total, want):
    t = min(total, want)
    while total % t:
        t -= 1
    return t


def kernel(x, mem, norm_mix, w_in, gla_gk_up, gla_gk_bias, gla_norm, m_conv, m_gate_bias, m_norm, gla_proj, m_proj, w_out, norm_xattn, norm_mem, x_wq, x_wkv, x_wo, norm_ffn, ffn_w13, ffn_w2, moe_router, moe_w13, moe_w2, norm_final):
    b, s, d = x.shape
    m_len = mem.shape[1]
    t = b * s
    depth = norm_mix.shape[0]
    tm = _pick(t, 512)
    row2 = lambda a: a.reshape(1, -1)

    xt = x.reshape(t, d)
    mem_t = mem.reshape(b * m_len, d)
    for l in range(depth):
        last_layer = l == depth - 1
        wz, ws = _split_w_in(w_in[l])
        z, zs = _norm_matmul(xt, row2(norm_mix[l]), wz, ws, tm=tm, tn=2048)
        z3 = z.reshape(b, s, Z_COLS)
        zs3 = zs.reshape(b, s, ZS_COLS)
        up = jnp.pad(gla_gk_up[l], ((0, LANES - GLA_RANK), (0, 0))).astype(BF16)
        o_gla = _gla(z3, zs3, up, row2(gla_gk_bias[l]), row2(gla_norm[l]))
        gbias = jnp.pad(jnp.stack([m_gate_bias[l, :M_HEADS], m_gate_bias[l, M_HEADS:]], axis=1).reshape(1, -1),
                        ((0, 0), (0, LANES - 2 * M_HEADS)))
        arow, rep = _mlstm_gates(zs3, gbias)
        qk3 = _mlstm_conv(z3, m_conv[l], ts=_pick(s, 512))
        o_m = _mlstm(qk3, z3, arow, rep, row2(m_norm[l]))
        xt = _merge(xt, o_gla.reshape(t, d), o_m.reshape(t, d), z,
                    gla_proj[l].astype(BF16), m_proj[l].astype(BF16), w_out[l].astype(BF16), tm=tm)
        kv = _norm_matmul(mem_t, row2(norm_mem[l]), x_wkv[l].astype(BF16), tm=_pick(b * m_len, 512), tn=1024)
        xt = _xattn(xt.reshape(b, s, d), row2(norm_xattn[l]), x_wq[l].astype(BF16),
                    kv.reshape(b, m_len, 2 * d), x_wo[l].astype(BF16), tq=_pick(s, 1024)).reshape(t, d)
        g_final = row2(norm_final) if last_layer else None
        if l % 2 == 0:
            w13, w2 = ffn_w13[l // 2], ffn_w2[l // 2]
            f = w2.shape[0]
            fp = -(-f // MXU_TILE) * MXU_TILE
            w1 = jnp.pad(w13[:, :f], ((0, 0), (0, fp - f))).astype(BF16)
            w3 = jnp.pad(w13[:, f:], ((0, 0), (0, fp - f))).astype(BF16)
            w2p = jnp.pad(w2, ((0, fp - f), (0, 0))).astype(BF16)
            xt = _ffn(xt, row2(norm_ffn[l]), w1, w3, w2p, g_final, tm=tm)
        else:
            router = jnp.pad(moe_router[l // 2], ((0, 0), (0, LANES - N_EXPERTS)))
            w13 = _cast_bf16(moe_w13[l // 2], rows=CAST_ROWS, cols=d)
            w2 = _cast_bf16(moe_w2[l // 2], rows=CAST_ROWS, cols=d)
            xt = _moe_sparse(xt, row2(norm_ffn[l]), router, w13, w2, g_final, tm=tm, tmg=tm, tf=w2.shape[1] // 2)
    return xt.reshape(b, s, d)
```

```python
import functools

import jax
import jax.numpy as jnp
from jax import lax
from jax.experimental import pallas as pl
from jax.experimental.pallas import tpu as pltpu
from jax.experimental.pallas import tpu_sc as plsc

F32 = jnp.float32
BF16 = jnp.bfloat16

EPS = 1e-6
LOG2E = 1.4426950408889634
D_MODEL = 1024
GLA_HEADS = 4
GLA_DK = 128
GLA_DV = 256
GLA_RANK = 16
GLA_GATE_NORM = 16.0
GLA_LOG_DECAY_MIN = -1.0
GLA_CHUNK = 64
GLA_HEADS_PER_STEP = 4
GLA_UNROLL = 8
M_HEADS = 4
M_DK = 128
M_DV = 256
M_CHUNK = 128
M_HEADS_PER_STEP = 2
M_UNROLL = 8
CONV_W = 4
X_HEADS = 4
N_EXPERTS = 8
LANES = 128
MXU_TILE = 256
SUBLANES = 8
VMEM_LIMIT = 48 * 1024 * 1024
SC_GATHER_WINDOW = 128
PACK_HALVES = 2
PACK_W = D_MODEL // 2 // PACK_HALVES
CAST_ROWS = 2048

Z_GQ, Z_GK, Z_GV, Z_GG = 0, 512, 1024, 2048
Z_MQ, Z_MK, Z_MV, Z_MO = 3072, 3584, 4096, 5120
Z_AG, Z_AM = 6144, 7168
Z_COLS = 8192
ZS_COLS = 2 * LANES


def _params(*sem):
    return pltpu.CompilerParams(dimension_semantics=sem, vmem_limit_bytes=VMEM_LIMIT)


def _rms(x, g):
    return x * lax.rsqrt(jnp.mean(x * x, axis=-1, keepdims=True) + EPS) * g


def _log_sigmoid(u):
    return jnp.minimum(u, 0.0) - jnp.log(1.0 + jnp.exp(-jnp.abs(u)))


def _sigmoid(u):
    return 1.0 / (1.0 + jnp.exp(-u))


def _dot(a, b):
    return jnp.dot(a, b, preferred_element_type=F32)


def _dot_nt(a, b):
    return lax.dot_general(a, b, (((1,), (1,)), ((), ())), preferred_element_type=F32)


def _dot_tn(a, b):
    return lax.dot_general(a, b, (((0,), (0,)), ((), ())), preferred_element_type=F32)


def _cumsum_rows(x):
    n = x.shape[0]
    row = lax.broadcasted_iota(jnp.int32, x.shape, 0)
    s = 1
    while s < n:
        x = x + jnp.where(row >= s, pltpu.roll(x, s, 0), 0.0)
        s *= 2
    return x


def _norm_matmul_kernel(*refs, tn, with_small):
    if with_small:
        x_ref, g_ref, w_ref, ws_ref, z_ref, zs_ref = refs
    else:
        x_ref, g_ref, w_ref, z_ref = refs
    h = _rms(x_ref[...], g_ref[...]).astype(BF16)
    if with_small:
        zs_ref[...] = _dot(h, ws_ref[...])
    for c0 in range(0, z_ref.shape[1], tn):
        z_ref[:, c0:c0 + tn] = _dot(h, w_ref[:, c0:c0 + tn]).astype(z_ref.dtype)


def _norm_matmul(x, g, w, ws=None, *, tm, tn):
    t, d = x.shape
    n = w.shape[1]
    resident = pl.Buffered(1)
    in_specs = [pl.BlockSpec((tm, d), lambda i: (i, 0)),
                pl.BlockSpec((1, d), lambda i: (0, 0)),
                pl.BlockSpec((d, n), lambda i: (0, 0), pipeline_mode=resident)]
    out_specs = [pl.BlockSpec((tm, n), lambda i: (i, 0))]
    out_shape = [jax.ShapeDtypeStruct((t, n), BF16)]
    args = [x, g, w]
    if ws is not None:
        ns = ws.shape[1]
        in_specs.append(pl.BlockSpec((d, ns), lambda i: (0, 0), pipeline_mode=resident))
        out_specs.append(pl.BlockSpec((tm, ns), lambda i: (i, 0)))
        out_shape.append(jax.ShapeDtypeStruct((t, ns), F32))
        args.append(ws)
    out = pl.pallas_call(
        functools.partial(_norm_matmul_kernel, tn=tn, with_small=ws is not None),
        grid=(t // tm,),
        in_specs=in_specs, out_specs=out_specs, out_shape=out_shape,
        compiler_params=_params("parallel"),
        name="norm_matmul",
    )(*args)
    return out if ws is not None else out[0]


def _gla_kernel(q_ref, k_ref, v_ref, gg_ref, lr_ref, up_ref, gb_ref, nw_ref, o_ref, la_scr, st_scr):
    seq = q_ref.shape[0]
    c = GLA_CHUNK
    scale = GLA_DK ** -0.5

    u = _dot(lr_ref[...].astype(BF16), up_ref[...]) + gb_ref[...]
    la_scr[...] = jnp.maximum(_log_sigmoid(u) * (1.0 / GLA_GATE_NORM), GLA_LOG_DECAY_MIN)
    st_scr[...] = jnp.zeros_like(st_scr)
    causal = (lax.broadcasted_iota(jnp.int32, (c, c), 0) >= lax.broadcasted_iota(jnp.int32, (c, c), 1))

    def body(n, carry):
        r0 = pl.multiple_of(n * c, c)
        rows = pl.ds(r0, c)
        for hh in range(GLA_HEADS_PER_STEP):
            kcols = slice(hh * GLA_DK, (hh + 1) * GLA_DK)
            vcols = slice(hh * GLA_DV, (hh + 1) * GLA_DV)
            cum = _cumsum_rows(la_scr[rows, kcols])
            cum_last = cum[c - 1:c, :]
            q = q_ref[rows, kcols].astype(F32) * scale
            k = k_ref[rows, kcols].astype(F32)
            v = v_ref[rows, vcols]
            q_dec = (q * jnp.exp(cum)).astype(BF16)
            k_inv = (k * jnp.exp(-cum)).astype(BF16)
            k_end = (k * jnp.exp(cum_last - cum)).astype(BF16)
            scores = jnp.where(causal, _dot_nt(q_dec, k_inv), 0.0).astype(BF16)
            st = st_scr[hh]
            o = _dot(scores, v) + _dot_nt(q_dec, st.astype(BF16))
            st_scr[hh] = st * jnp.exp(cum_last) + _dot_tn(v, k_end)
            gate = gg_ref[rows, vcols].astype(F32)
            o_ref[rows, vcols] = (_rms(o, nw_ref[:, vcols]) * (gate * _sigmoid(gate))).astype(o_ref.dtype)
        return carry

    lax.fori_loop(0, seq // c, body, 0, unroll=GLA_UNROLL)


def _gla(z3, zs3, up, gb, nw):
    b, s, _ = z3.shape
    hps = GLA_HEADS_PER_STEP
    dk, dv = hps * GLA_DK, hps * GLA_DV
    qb, kb, vb, gb_ = Z_GQ // dk, Z_GK // dk, Z_GV // dv, Z_GG // dv
    return pl.pallas_call(
        _gla_kernel,
        grid=(b, GLA_HEADS // hps),
        in_specs=[pl.BlockSpec((None, s, dk), lambda i, h: (i, 0, qb + h)),
                  pl.BlockSpec((None, s, dk), lambda i, h: (i, 0, kb + h)),
                  pl.BlockSpec((None, s, dv), lambda i, h: (i, 0, vb + h)),
                  pl.BlockSpec((None, s, dv), lambda i, h: (i, 0, gb_ + h)),
                  pl.BlockSpec((None, s, LANES), lambda i, h: (i, 0, 0)),
                  pl.BlockSpec((LANES, dk), lambda i, h: (0, h)),
                  pl.BlockSpec((1, dk), lambda i, h: (0, h)),
                  pl.BlockSpec((1, dv), lambda i, h: (0, h))],
        out_specs=pl.BlockSpec((None, s, dv), lambda i, h: (i, 0, h)),
        out_shape=jax.ShapeDtypeStruct((b, s, GLA_HEADS * GLA_DV), BF16),
        scratch_shapes=[pltpu.VMEM((s, dk), F32), pltpu.VMEM((hps, GLA_DV, GLA_DK), F32)],
        compiler_params=_params("parallel", "arbitrary"),
        name="gla",
    )(z3, z3, z3, z3, zs3, up, gb, nw)


def _chunk_scan_lanes(x, combine, identity, chunk):
    lane_in_chunk = lax.broadcasted_iota(jnp.int32, x.shape, 1) % chunk
    step = 1
    while step < chunk:
        x = combine(x, jnp.where(lane_in_chunk >= step, pltpu.roll(x, step, 1), identity))
        step *= 2
    return x


def _replicate_rows(x, row):
    hi = x.astype(BF16)
    r1 = x - hi.astype(F32)
    mid = r1.astype(BF16)
    lo = (r1 - mid.astype(F32)).astype(BF16)
    parts = jnp.concatenate([hi, mid, lo, jnp.zeros_like(hi)], axis=0)
    pick = lax.broadcasted_iota(jnp.int32, (4 * SUBLANES, LANES), 0) % SUBLANES == row
    return _dot_tn(parts, jnp.where(pick, 1.0, 0.0).astype(BF16))


def _mlstm_gates_kernel(gt_ref, gb_ref, arow_ref, rep_ref):
    c = M_CHUNK
    gates_t = (gt_ref[...] + gb_ref[...]).T
    top = gates_t[0:SUBLANES, :]
    odd = lax.broadcasted_iota(jnp.int32, top.shape, 0) % 2 == 1
    lf = jnp.where(odd, _log_sigmoid(top), 0.0)
    b_odd = _chunk_scan_lanes(lf, jnp.add, 0.0, c)
    a_even = top - pltpu.roll(b_odd, SUBLANES - 1, 0)
    amax_even = _chunk_scan_lanes(jnp.where(odd, -jnp.inf, a_even), jnp.maximum, -jnp.inf, c)
    for h in range(M_HEADS):
        arow_ref[h] = pltpu.roll(a_even, (SUBLANES - 2 * h) % SUBLANES, 0)
        rep_ref[h, 0] = _replicate_rows(b_odd, 2 * h + 1)
        rep_ref[h, 1] = _replicate_rows(a_even, 2 * h)
        rep_ref[h, 2] = _replicate_rows(jnp.where(odd, 0.0, amax_even), 2 * h)


def _mlstm_gates(zs3, gbias):
    b, s, _ = zs3.shape
    return pl.pallas_call(
        _mlstm_gates_kernel,
        grid=(b,),
        in_specs=[pl.BlockSpec((None, s, LANES), lambda i: (i, 0, 1)),
                  pl.BlockSpec((1, LANES), lambda i: (0, 0))],
        out_specs=[pl.BlockSpec((None, M_HEADS, SUBLANES, s), lambda i: (i, 0, 0, 0)),
                   pl.BlockSpec((None, M_HEADS, 3, s, LANES), lambda i: (i, 0, 0, 0, 0))],
        out_shape=[jax.ShapeDtypeStruct((b, M_HEADS, SUBLANES, s), F32),
                   jax.ShapeDtypeStruct((b, M_HEADS, 3, s, LANES), F32)],
        compiler_params=_params("parallel"),
        name="mlstm_gates",
    )(zs3, gbias)


def _mlstm_conv_kernel(cur_ref, prev_ref, w_ref, shift_ref, o_ref):
    c = M_CHUNK
    d = cur_ref.shape[1]
    prev = jnp.where(pl.program_id(1) > 0, prev_ref[...], jnp.zeros_like(prev_ref))
    lane = lax.broadcasted_iota(jnp.int32, (1, d), 1)
    col_scale = jnp.where(lane < d // 2, M_DK ** -0.5, 1.0)
    for i in range(cur_ref.shape[0] // c):
        cur = cur_ref[i * c:(i + 1) * c, :]
        before = prev if i == 0 else cur_ref[(i - 1) * c:i * c, :]
        shifted = _dot(shift_ref[...], jnp.concatenate([before, cur], axis=0))
        acc = w_ref[CONV_W - 1:CONV_W, :] * cur.astype(F32)
        for s in range(1, CONV_W):
            acc = acc + w_ref[CONV_W - 1 - s:CONV_W - s, :] * shifted[(s - 1) * c:s * c, :]
        o_ref[i * c:(i + 1) * c, :] = (acc * _sigmoid(acc) * col_scale).astype(o_ref.dtype)


def _mlstm_conv(z3, conv, *, ts):
    b, s, _ = z3.shape
    c = M_CHUNK
    d = conv.shape[1]
    blk = Z_MQ // d
    assert Z_MQ % d == 0 and Z_MK == Z_MQ + d // 2
    t_idx = jnp.arange(c)[:, None]
    col = jnp.arange(2 * c)[None, :]
    shift = jnp.concatenate([(col == c + t_idx - s) for s in range(1, CONV_W)], axis=0).astype(BF16)
    return pl.pallas_call(
        _mlstm_conv_kernel,
        grid=(b, s // ts),
        in_specs=[pl.BlockSpec((None, ts, d), lambda i, j: (i, j, blk)),
                  pl.BlockSpec((None, c, d), lambda i, j: (i, jnp.maximum(j * (ts // c) - 1, 0), blk)),
                  pl.BlockSpec((CONV_W, d), lambda i, j: (0, 0)),
                  pl.BlockSpec(((CONV_W - 1) * c, 2 * c), lambda i, j: (0, 0))],
        out_specs=pl.BlockSpec((None, ts, d), lambda i, j: (i, j, 0)),
        out_shape=jax.ShapeDtypeStruct((b, s, d), BF16),
        compiler_params=_params("parallel", "parallel"),
        name="mlstm_conv",
    )(z3, z3, conv, shift)


def _mlstm_kernel(q_ref, k_ref, v_ref, og_ref, arow_ref, rep_ref, nw_ref, o_ref, st_scr, m_scr):
    seq = q_ref.shape[0]
    c = M_CHUNK
    st_scr[...] = jnp.zeros_like(st_scr)
    m_scr[...] = jnp.zeros_like(m_scr)
    causal = (lax.broadcasted_iota(jnp.int32, (c, c), 0) >= lax.broadcasted_iota(jnp.int32, (c, c), 1))
    ones_cols = jnp.ones((c, LANES), BF16)
    ones_dv = jnp.ones((M_DV, LANES), BF16)

    def wide(x, n):
        return jnp.concatenate([x] * n, axis=1)

    def body(n, carry):
        r0 = pl.multiple_of(n * c, c)
        rows = pl.ds(r0, c)
        for hh in range(M_HEADS_PER_STEP):
            kcols = slice(hh * M_DK, (hh + 1) * M_DK)
            vcols = slice(hh * M_DV, (hh + 1) * M_DV)
            qb = q_ref[rows, kcols]
            kb = k_ref[rows, kcols]
            v_ext = jnp.concatenate([v_ref[rows, vcols], ones_cols], axis=1)
            b = rep_ref[hh, 0, rows, :]
            a = rep_ref[hh, 1, rows, :]
            amax = rep_ref[hh, 2, rows, :]
            a_row = arow_ref[hh, 0:1, rows]
            m_prev = m_scr[hh]
            st = st_scr[hh]

            dmat = jnp.where(causal, b + a_row, -jnp.inf)
            m_inter = b + m_prev
            m_t = jnp.maximum(m_inter, b + amax)
            w_ts = jnp.exp(dmat - m_t) * _dot_nt(qb, kb)
            s_inter = jnp.exp(m_inter - m_t)
            nd = _dot(w_ts.astype(BF16), v_ext) + wide(s_inter, 3) * _dot(qb, st.astype(BF16))
            inv = 1.0 / jnp.maximum(jnp.abs(nd[:, M_DV:]), jnp.exp(-m_t))
            h = nd[:, :M_DV] * wide(inv, 2)

            b_last = b[c - 1:c, :]
            m_new = jnp.maximum(b_last + m_prev, b_last + amax[c - 1:c, :])
            kw = kb.astype(F32) * jnp.exp(b_last + a - m_new)
            st_scr[hh] = wide(jnp.exp(b_last + m_prev - m_new), 3) * st + _dot_tn(kw.astype(BF16), v_ext)
            m_scr[hh] = m_new

            mean_sq = _dot((h * h).astype(BF16), ones_dv) * (1.0 / M_DV)
            gate = og_ref[rows, vcols].astype(F32)
            y = h * wide(lax.rsqrt(mean_sq + EPS), 2) * nw_ref[:, vcols]
            o_ref[rows, vcols] = (y * _sigmoid(gate)).astype(o_ref.dtype)
        return carry

    lax.fori_loop(0, seq // c, body, 0, unroll=M_UNROLL)


def _mlstm(qk3, z3, arow, rep, nw):
    b, s, _ = z3.shape
    hps = M_HEADS_PER_STEP
    dk, dv = hps * M_DK, hps * M_DV
    vb, ob = Z_MV // dv, Z_MO // dv
    return pl.pallas_call(
        _mlstm_kernel,
        grid=(b, M_HEADS // hps),
        in_specs=[pl.BlockSpec((None, s, dk), lambda i, h: (i, 0, h)),
                  pl.BlockSpec((None, s, dk), lambda i, h: (i, 0, M_HEADS // hps + h)),
                  pl.BlockSpec((None, s, dv), lambda i, h: (i, 0, vb + h)),
                  pl.BlockSpec((None, s, dv), lambda i, h: (i, 0, ob + h)),
                  pl.BlockSpec((None, hps, SUBLANES, s), lambda i, h: (i, h, 0, 0)),
                  pl.BlockSpec((None, hps, 3, s, LANES), lambda i, h: (i, h, 0, 0, 0)),
                  pl.BlockSpec((1, dv), lambda i, h: (0, h))],
        out_specs=pl.BlockSpec((None, s, dv), lambda i, h: (i, 0, h)),
        out_shape=jax.ShapeDtypeStruct((b, s, M_HEADS * M_DV), BF16),
        scratch_shapes=[pltpu.VMEM((hps, M_DK, M_DV + LANES), F32), pltpu.VMEM((hps, 1, LANES), F32)],
        compiler_params=_params("parallel", "arbitrary"),
        name="mlstm",
    )(qk3, qk3, z3, z3, arow, rep, nw)


def _merge_kernel(x_ref, og_ref, om_ref, ag_ref, am_ref, wg_ref, wm_ref, wo_ref, o_ref):
    merged = (_sigmoid(ag_ref[...].astype(F32)) * _dot(og_ref[...], wg_ref[...])
              + _sigmoid(am_ref[...].astype(F32)) * _dot(om_ref[...], wm_ref[...]))
    o_ref[...] = x_ref[...] + _dot(merged.astype(BF16), wo_ref[...])


def _merge(x, o_gla, o_m, z, wg, wm, wo, *, tm):
    t, d = x.shape
    row = lambda i: (i, 0)
    full = lambda i: (0, 0)
    return pl.pallas_call(
        _merge_kernel,
        grid=(t // tm,),
        in_specs=[pl.BlockSpec((tm, d), row), pl.BlockSpec((tm, d), row), pl.BlockSpec((tm, d), row),
                  pl.BlockSpec((tm, d), lambda i: (i, Z_AG // D_MODEL)),
                  pl.BlockSpec((tm, d), lambda i: (i, Z_AM // D_MODEL)),
                  pl.BlockSpec((d, d), full), pl.BlockSpec((d, d), full), pl.BlockSpec((d, d), full)],
        out_specs=pl.BlockSpec((tm, d), row),
        out_shape=jax.ShapeDtypeStruct((t, d), F32),
        compiler_params=_params("parallel"),
        name="merge",
    )(x, o_gla, o_m, z, z, wg, wm, wo)


def _xattn_kernel(x_ref, g_ref, wq_ref, kv_ref, wo_ref, o_ref):
    d = x_ref.shape[1]
    dh = d // X_HEADS
    x = x_ref[...]
    q = _dot(_rms(x, g_ref[...]).astype(BF16), wq_ref[...]).astype(BF16)
    heads = []
    ones_m = jnp.ones((kv_ref.shape[0], LANES), BF16)
    for h in range(X_HEADS):
        k = kv_ref[:, h * dh:(h + 1) * dh]
        v = kv_ref[:, d + h * dh:d + (h + 1) * dh]
        s = _dot_nt(q[:, h * dh:(h + 1) * dh], k)
        p = jnp.exp2((s - jnp.max(s, axis=-1, keepdims=True)) * (dh ** -0.5 * LOG2E)).astype(BF16)
        inv_l = 1.0 / _dot(p, ones_m)
        heads.append((_dot(p, v) * jnp.concatenate([inv_l] * (dh // LANES), axis=1)).astype(BF16))
    o_ref[...] = x + _dot(jnp.concatenate(heads, axis=-1), wo_ref[...])


def _xattn(x3, g, wq, kv, wo, *, tq):
    b, s, d = x3.shape
    m = kv.shape[1]
    return pl.pallas_call(
        _xattn_kernel,
        grid=(b, s // tq),
        in_specs=[pl.BlockSpec((None, tq, d), lambda i, j: (i, j, 0)),
                  pl.BlockSpec((1, d), lambda i, j: (0, 0)),
                  pl.BlockSpec((d, d), lambda i, j: (0, 0)),
                  pl.BlockSpec((None, m, 2 * d), lambda i, j: (i, 0, 0)),
                  pl.BlockSpec((d, d), lambda i, j: (0, 0))],
        out_specs=pl.BlockSpec((None, tq, d), lambda i, j: (i, j, 0)),
        out_shape=jax.ShapeDtypeStruct((b, s, d), F32),
        compiler_params=_params("parallel", "parallel"),
        name="xattn",
    )(x3, g, wq, kv, wo)


def _ffn_kernel(*refs, chunks, final_norm):
    if final_norm:
        x_ref, g_ref, w1_ref, w3_ref, w2_ref, gf_ref, o_ref = refs
    else:
        x_ref, g_ref, w1_ref, w3_ref, w2_ref, o_ref = refs
    x = x_ref[...]
    h = _rms(x, g_ref[...]).astype(BF16)
    out = x
    for c0, c1 in chunks:
        a = _dot(h, w1_ref[:, c0:c1])
        act = (a * _sigmoid(a) * _dot(h, w3_ref[:, c0:c1])).astype(BF16)
        out = out + _dot(act, w2_ref[c0:c1, :])
    if final_norm:
        out = _rms(out, gf_ref[...])
    o_ref[...] = out


def _ffn(x, g, w1, w3, w2, g_final, *, tm):
    t, d = x.shape
    f = w2.shape[0]
    assert f % MXU_TILE == 0
    half = (f // MXU_TILE + 1) // 2 * MXU_TILE
    chunks = ((0, half), (half, f))
    resident = pl.Buffered(1)
    in_specs = [pl.BlockSpec((tm, d), lambda i: (i, 0)),
                pl.BlockSpec((1, d), lambda i: (0, 0)),
                pl.BlockSpec((d, f), lambda i: (0, 0), pipeline_mode=resident),
                pl.BlockSpec((d, f), lambda i: (0, 0), pipeline_mode=resident),
                pl.BlockSpec((f, d), lambda i: (0, 0), pipeline_mode=resident)]
    args = [x, g, w1, w3, w2]
    if g_final is not None:
        in_specs.append(pl.BlockSpec((1, d), lambda i: (0, 0)))
        args.append(g_final)
    return pl.pallas_call(
        functools.partial(_ffn_kernel, chunks=chunks, final_norm=g_final is not None),
        grid=(t // tm,),
        in_specs=in_specs,
        out_specs=pl.BlockSpec((tm, d), lambda i: (i, 0)),
        out_shape=jax.ShapeDtypeStruct((t, d), F32),
        compiler_params=_params("parallel"),
        name="dense_ffn",
    )(*args)


def _store_packed(ref, y):
    half = y.shape[1] // 2
    bits = lax.bitcast_convert_type(y.astype(BF16).astype(F32), jnp.int32)
    lo = lax.shift_right_logical(bits[:, :half], 16)
    hi = jnp.bitwise_and(bits[:, half:], jnp.int32(-65536))
    packed = jnp.bitwise_or(hi, lo)
    w = half // PACK_HALVES
    for p in range(PACK_HALVES):
        ref[p] = packed[:, p * w:(p + 1) * w]


def _load_packed(ref):
    packed = jnp.concatenate([ref[p] for p in range(PACK_HALVES)], axis=1)
    lo = lax.bitcast_convert_type(lax.shift_left(packed, 16), F32)
    hi = lax.bitcast_convert_type(jnp.bitwise_and(packed, jnp.int32(-65536)), F32)
    return jnp.concatenate([lo, hi], axis=1)


def _router_kernel(x_ref, g_ref, r_ref, hp_ref, mi_ref, mw_ref, cnt_ref, carry_scr):
    @pl.when(pl.program_id(0) == 0)
    def _():
        carry_scr[...] = jnp.zeros_like(carry_scr)

    rows = x_ref.shape[0]
    h = _rms(x_ref[...], g_ref[...])
    _store_packed(hp_ref, h)
    r = r_ref[...]
    h_hi = h.astype(BF16)
    h_lo = (h - h_hi.astype(F32)).astype(BF16)
    r_hi = r.astype(BF16)
    r_lo = (r - r_hi.astype(F32)).astype(BF16)
    hi_terms = _dot(h_hi, jnp.concatenate([r_hi, r_lo], axis=1))
    logits = hi_terms[:, :LANES] + (hi_terms[:, LANES:] + _dot(h_lo, r_hi))
    lane = lax.broadcasted_iota(jnp.int32, logits.shape, 1)
    logits = jnp.where(lane < N_EXPERTS, logits, -jnp.inf)
    m1 = jnp.max(logits, axis=-1, keepdims=True)
    i1 = jnp.min(jnp.where(logits == m1, lane, LANES), axis=-1, keepdims=True)
    rest = jnp.where(lane == i1, -jnp.inf, logits)
    m2 = jnp.max(rest, axis=-1, keepdims=True)
    i2 = jnp.min(jnp.where(rest == m2, lane, LANES), axis=-1, keepdims=True)
    e2 = jnp.exp(m2 - m1)
    w1 = 1.0 / (1.0 + e2)
    w2 = e2 / (1.0 + e2)
    oh1 = (lane == i1).astype(F32)
    oh2 = (lane == i2).astype(F32)
    oh = oh1 + oh2
    earlier = (lax.broadcasted_iota(jnp.int32, (rows, rows), 0) > lax.broadcasted_iota(jnp.int32, (rows, rows), 1))
    before = _dot(earlier.astype(BF16), oh.astype(BF16)) + carry_scr[...]
    rank1 = jnp.sum(oh1 * before, axis=-1, keepdims=True)
    rank2 = jnp.sum(oh2 * before, axis=-1, keepdims=True)
    carry_scr[...] += jnp.sum(oh, axis=0, keepdims=True)
    cnt_ref[...] = carry_scr[...]
    meta = jnp.where(lane == 0, i1.astype(F32), jnp.where(lane == 1, i2.astype(F32), jnp.where(lane == 2, rank1,
                     jnp.where(lane == 3, rank2, 0.0))))
    mi_ref[...] = meta.T[0:SUBLANES, :]
    mw_ref[...] = jnp.where(lane == 0, w1, jnp.where(lane == 1, w2, 0.0))


def _router(x, g, router, *, tm):
    t, d = x.shape
    row = lambda i: (i, 0)
    fix = lambda i: (0, 0)
    return pl.pallas_call(
        _router_kernel,
        grid=(t // tm,),
        in_specs=[pl.BlockSpec((tm, d), row), pl.BlockSpec((1, d), fix), pl.BlockSpec((d, LANES), fix)],
        out_specs=[pl.BlockSpec((PACK_HALVES, tm, PACK_W), lambda i: (0, i, 0)),
                   pl.BlockSpec((SUBLANES, tm), lambda i: (0, i)),
                   pl.BlockSpec((tm, LANES), row), pl.BlockSpec((1, LANES), fix)],
        out_shape=[jax.ShapeDtypeStruct((PACK_HALVES, t, PACK_W), jnp.int32),
                   jax.ShapeDtypeStruct((SUBLANES, t), F32),
                   jax.ShapeDtypeStruct((t, LANES), F32), jax.ShapeDtypeStruct((1, LANES), F32)],
        scratch_shapes=[pltpu.VMEM((1, LANES), F32)],
        compiler_params=_params("arbitrary"),
        name="router",
    )(x, g, router)


def _grouped_kernel(te_ref, tv_ref, xs_ref, w1_ref, w3_ref, w2_ref, ys_ref, h_scr, acc_scr):
    i = pl.program_id(0)
    j = pl.program_id(1)
    last = pl.num_programs(1) - 1
    valid = tv_ref[i]
    active = valid > 0

    @pl.when(jnp.logical_and(active, j == 0))
    def _():
        h = _load_packed(xs_ref)
        row = lax.broadcasted_iota(jnp.int32, h.shape, 0)
        h_scr[...] = jnp.where(row < valid, h, 0.0).astype(BF16)
        acc_scr[...] = jnp.zeros_like(acc_scr)

    @pl.when(active)
    def _():
        h = h_scr[...]
        a = _dot(h, w1_ref[...])
        act = (a * _sigmoid(a) * _dot(h, w3_ref[...])).astype(BF16)
        acc_scr[...] += _dot(act, w2_ref[...])

    @pl.when(jnp.logical_and(active, j == last))
    def _():
        _store_packed(ys_ref, acc_scr[...])

    @pl.when(jnp.logical_and(jnp.logical_not(active), j == last))
    def _():
        ys_ref[...] = jnp.zeros_like(ys_ref)


def _grouped_swiglu(tile_expert, tile_valid, xs, w13, w2, *, tmg, tf):
    _, p, _ = xs.shape
    d = w2.shape[2]
    f = w2.shape[1]
    nj = f // tf

    def jj(i, j, tv):
        return jnp.where(tv[i] > 0, j, nj - 1)

    grid_spec = pltpu.PrefetchScalarGridSpec(
        num_scalar_prefetch=2,
        grid=(p // tmg, nj),
        in_specs=[pl.BlockSpec((PACK_HALVES, tmg, PACK_W), lambda i, j, te, tv: (0, i, 0)),
                  pl.BlockSpec((None, d, tf), lambda i, j, te, tv: (te[i], 0, jj(i, j, tv))),
                  pl.BlockSpec((None, d, tf), lambda i, j, te, tv: (te[i], 0, nj + jj(i, j, tv))),
                  pl.BlockSpec((None, tf, d), lambda i, j, te, tv: (te[i], jj(i, j, tv), 0))],
        out_specs=pl.BlockSpec((PACK_HALVES, tmg, PACK_W), lambda i, j, te, tv: (0, i, 0)),
        scratch_shapes=[pltpu.VMEM((tmg, d), BF16), pltpu.VMEM((tmg, d), F32)],
    )
    return pl.pallas_call(
        _grouped_kernel,
        grid_spec=grid_spec,
        out_shape=jax.ShapeDtypeStruct((PACK_HALVES, p, PACK_W), jnp.int32),
        compiler_params=_params("arbitrary", "arbitrary"),
        name="grouped_swiglu",
    )(tile_expert, tile_valid, xs, w13, w13, w2)


def _combine_kernel(*refs, final_norm):
    if final_norm:
        x_ref, mw_ref, y1_ref, y2_ref, gf_ref, o_ref = refs
    else:
        x_ref, mw_ref, y1_ref, y2_ref, o_ref = refs
    mw = mw_ref[...]
    out = x_ref[...] + mw[:, 0:1] * _load_packed(y1_ref) + mw[:, 1:2] * _load_packed(y2_ref)
    if final_norm:
        out = _rms(out, gf_ref[...])
    o_ref[...] = out


def _combine(x, mw, y12, g_final, *, tm):
    t, d = x.shape
    row = lambda i: (i, 0)
    in_specs = [pl.BlockSpec((tm, d), row), pl.BlockSpec((tm, LANES), row),
                pl.BlockSpec((PACK_HALVES, tm, PACK_W), lambda i: (0, i, 0)),
                pl.BlockSpec((PACK_HALVES, tm, PACK_W), lambda i: (0, t // tm + i, 0))]
    args = [x, mw, y12, y12]
    if g_final is not None:
        in_specs.append(pl.BlockSpec((1, d), lambda i: (0, 0)))
        args.append(g_final)
    return pl.pallas_call(
        functools.partial(_combine_kernel, final_norm=g_final is not None),
        grid=(t // tm,),
        in_specs=in_specs,
        out_specs=pl.BlockSpec((tm, d), row),
        out_shape=jax.ShapeDtypeStruct((t, d), F32),
        compiler_params=_params("parallel"),
        name="combine",
    )(*args)


def _gather_rows(table, idx):
    halves, n_rows, width = table.shape
    n = halves * idx.shape[0]
    assert n % SC_GATHER_WINDOW == 0
    flat_idx = jnp.concatenate([idx + h * n_rows for h in range(halves)]).reshape(1, n)
    mesh = plsc.VectorSubcoreMesh(core_axis_name="core", subcore_axis_name="subcore")

    @pl.kernel(out_type=jax.ShapeDtypeStruct((n, width), table.dtype), mesh=mesh)
    def gather(table_hbm, idx_hbm, out_hbm):
        def body(idx_vmem, out_vmem):
            pltpu.sync_copy(table_hbm.at[idx_vmem.at[0]], out_vmem)

        pltpu.emit_pipeline(
            body,
            grid=(n // SC_GATHER_WINDOW,),
            in_specs=[pl.BlockSpec((1, SC_GATHER_WINDOW), lambda i: (0, i))],
            out_specs=[pl.BlockSpec((SC_GATHER_WINDOW, width), lambda i: (i, 0))],
            core_axis_name=("core", "subcore"),
            dimension_semantics=(pltpu.PARALLEL,),
        )(idx_hbm, out_hbm)

    return gather(table.reshape(halves * n_rows, width), flat_idx).reshape(halves, idx.shape[0], width)


def _dispatch_rows(table, pos_list, n_out):
    halves, t, width = table.shape
    copies = len(pos_list)
    idx = jnp.concatenate([pos + h * n_out for h in range(halves) for pos in pos_list])
    n = idx.shape[0]
    win_t = t // SC_GATHER_WINDOW
    assert t % SC_GATHER_WINDOW == 0
    mesh = plsc.VectorSubcoreMesh(core_axis_name="core", subcore_axis_name="subcore")

    @pl.kernel(out_type=jax.ShapeDtypeStruct((halves * n_out, width), table.dtype), mesh=mesh)
    def scatter(table_hbm, idx_hbm, out_hbm):
        def body(rows_vmem, idx_vmem):
            pltpu.sync_copy(rows_vmem, out_hbm.at[idx_vmem.at[0]])

        pltpu.emit_pipeline(
            body,
            grid=(n // SC_GATHER_WINDOW,),
            in_specs=[pl.BlockSpec((SC_GATHER_WINDOW, width),
                                   lambda i: ((i // (copies * win_t)) * win_t + i % win_t, 0)),
                      pl.BlockSpec((1, SC_GATHER_WINDOW), lambda i: (0, i))],
            out_specs=[],
            core_axis_name=("core", "subcore"),
            dimension_semantics=(pltpu.PARALLEL,),
        )(table_hbm, idx_hbm)

    return scatter(table.reshape(halves * t, width), idx.reshape(1, n)).reshape(halves, n_out, width)


def _moe_sparse(x, g, router, w13, w2, g_final, *, tm, tmg, tf):
    t, d = x.shape
    hp, mi, mw, cnt = _router(x, g, router, tm=tm)
    counts = cnt[0, :N_EXPERTS].astype(jnp.int32)
    tiles_per_expert = (counts + tmg - 1) // tmg
    tile_end = jnp.cumsum(tiles_per_expert)
    group_start = (tile_end - tiles_per_expert) * tmg
    e1, e2, r1, r2 = (mi[r].astype(jnp.int32) for r in range(4))
    pos1 = group_start[e1] + r1
    pos2 = group_start[e2] + r2
    n_tiles = 2 * t // tmg + N_EXPERTS
    tile = jnp.arange(n_tiles, dtype=jnp.int32)
    tile_expert = jnp.minimum(jnp.sum(tile[:, None] >= tile_end[None, :], axis=1), N_EXPERTS - 1).astype(jnp.int32)
    rows_left = counts[tile_expert] - (tile - (tile_end - tiles_per_expert)[tile_expert]) * tmg
    tile_valid = jnp.where(tile < tile_end[-1], jnp.clip(rows_left, 0, tmg), 0).astype(jnp.int32)

    xs = _dispatch_rows(hp, [pos1, pos2], n_tiles * tmg)
    ys = _grouped_swiglu(tile_expert, tile_valid, xs, w13, w2, tmg=tmg, tf=tf)
    y12 = _gather_rows(ys, jnp.concatenate([pos1, pos2]))
    return _combine(x, mw, y12, g_final, tm=_pick(t, 2 * tm))


def _cast_kernel(w_ref, o_ref):
    o_ref[...] = w_ref[...].astype(o_ref.dtype)


def _cast_bf16(w, *, rows, cols):
    shape = w.shape
    w2 = w.reshape(-1, shape[-1])
    r, c = w2.shape
    out = pl.pallas_call(
        _cast_kernel,
        grid=(r // rows, c // cols),
        in_specs=[pl.BlockSpec((rows, cols), lambda i, j: (i, j))],
        out_specs=pl.BlockSpec((rows, cols), lambda i, j: (i, j)),
        out_shape=jax.ShapeDtypeStruct((r, c), BF16),
        compiler_params=_params("parallel", "parallel"),
        name="cast_bf16",
    )(w2)
    return out.reshape(shape)


def _split_w_in(w_in):
    sizes = (512, 512, 1024, GLA_RANK, 1024, 512, 512, 1024, M_HEADS, M_HEADS, 1024, 1024, 1024)
    parts, off = [], 0
    for n in sizes:
        parts.append(w_in[:, off:off + n])
        off += n
    g_q, g_k, g_v, g_lr, g_g, m_q, m_k, m_v, m_i, m_f, m_o, a_g, a_m = parts
    wz = jnp.concatenate([g_q, g_k, g_v, g_g, m_q, m_k, m_v, m_o, a_g, a_m], axis=1).astype(BF16)
    d = w_in.shape[0]
    gate_cols = jnp.stack([m_i, m_f], axis=2).reshape(d, 2 * M_HEADS)
    ws = jnp.concatenate([jnp.pad(g_lr, ((0, 0), (0, LANES - GLA_RANK))),
                          jnp.pad(gate_cols, ((0, 0), (0, LANES - 2 * M_HEADS)))], axis=1).astype(BF16)
    assert wz.shape == (d, Z_COLS) and ws.shape == (d, ZS_COLS)
    return wz, ws


def _pick(total, want):
    t = min(total, want)
    while total % t:
        t -= 1
    return t


def kernel(x, mem, norm_mix, w_in, gla_gk_up, gla_gk_bias, gla_norm, m_conv, m_gate_bias, m_norm, gla_proj, m_proj, w_out, norm_xattn, norm_mem, x_wq, x_wkv, x_wo, norm_ffn, ffn_w13, ffn_w2, moe_router, moe_w13, moe_w2, norm_final):
    b, s, d = x.shape
    m_len = mem.shape[1]
    t = b * s
    depth = norm_mix.shape[0]
    tm = _pick(t, 512)
    row2 = lambda a: a.reshape(1, -1)

    xt = x.reshape(t, d)
    mem_t = mem.reshape(b * m_len, d)
    for l in range(depth):
        last_layer = l == depth - 1
        wz, ws = _split_w_in(w_in[l])
        z, zs = _norm_matmul(xt, row2(norm_mix[l]), wz, ws, tm=tm, tn=2048)
        z3 = z.reshape(b, s, Z_COLS)
        zs3 = zs.reshape(b, s, ZS_COLS)
        up = jnp.pad(gla_gk_up[l], ((0, LANES - GLA_RANK), (0, 0))).astype(BF16)
        o_gla = _gla(z3, zs3, up, row2(gla_gk_bias[l]), row2(gla_norm[l]))
        gbias = jnp.pad(jnp.stack([m_gate_bias[l, :M_HEADS], m_gate_bias[l, M_HEADS:]], axis=1).reshape(1, -1),
                        ((0, 0), (0, LANES - 2 * M_HEADS)))
        arow, rep = _mlstm_gates(zs3, gbias)
        qk3 = _mlstm_conv(z3, m_conv[l], ts=_pick(s, 1024))
        o_m = _mlstm(qk3, z3, arow, rep, row2(m_norm[l]))
        xt = _merge(xt, o_gla.reshape(t, d), o_m.reshape(t, d), z,
                    gla_proj[l].astype(BF16), m_proj[l].astype(BF16), w_out[l].astype(BF16), tm=tm)
        kv = _norm_matmul(mem_t, row2(norm_mem[l]), x_wkv[l].astype(BF16), tm=_pick(b * m_len, 512), tn=1024)
        xt = _xattn(xt.reshape(b, s, d), row2(norm_xattn[l]), x_wq[l].astype(BF16),
                    kv.reshape(b, m_len, 2 * d), x_wo[l].astype(BF16), tq=_pick(s, 1024)).reshape(t, d)
        g_final = row2(norm_final) if last_layer else None
        if l % 2 == 0:
            w13, w2 = ffn_w13[l // 2], ffn_w2[l // 2]
            f = w2.shape[0]
            fp = -(-f // MXU_TILE) * MXU_TILE
            w1 = jnp.pad(w13[:, :f], ((0, 0), (0, fp - f))).astype(BF16)
            w3 = jnp.pad(w13[:, f:], ((0, 0), (0, fp - f))).astype(BF16)
            w2p = jnp.pad(w2, ((0, fp - f), (0, 0))).astype(BF16)
            xt = _ffn(xt, row2(norm_ffn[l]), w1, w3, w2p, g_final, tm=tm)
        else:
            router = jnp.pad(moe_router[l // 2], ((0, 0), (0, LANES - N_EXPERTS)))
            w13 = _cast_bf16(moe_w13[l // 2], rows=CAST_ROWS, cols=d)
            w2 = _cast_bf16(moe_w2[l // 2], rows=CAST_ROWS, cols=d)
            xt = _moe_sparse(xt, row2(norm_ffn[l]), router, w13, w2, g_final, tm=tm, tmg=tm, tf=w2.shape[1] // 2)
    return xt.reshape(b, s, d)
```

```python
import functools

import jax
import jax.numpy as jnp
from jax import lax
from jax.experimental import pallas as pl
from jax.experimental.pallas import tpu as pltpu
from jax.experimental.pallas import tpu_sc as plsc

F32 = jnp.float32
BF16 = jnp.bfloat16

EPS = 1e-6
LOG2E = 1.4426950408889634
D_MODEL = 1024
GLA_HEADS = 4
GLA_DK = 128
GLA_DV = 256
GLA_RANK = 16
GLA_GATE_NORM = 16.0
GLA_LOG_DECAY_MIN = -1.0
GLA_CHUNK = 64
GLA_HEADS_PER_STEP = 4
GLA_UNROLL = 8
M_HEADS = 4
M_DK = 128
M_DV = 256
M_CHUNK = 128
M_HEADS_PER_STEP = 2
M_UNROLL = 8
CONV_W = 4
X_HEADS = 4
N_EXPERTS = 8
LANES = 128
MXU_TILE = 256
SUBLANES = 8
VMEM_LIMIT = 48 * 1024 * 1024
SC_GATHER_WINDOW = 128
PACK_HALVES = 2
PACK_W = D_MODEL // 2 // PACK_HALVES
CAST_ROWS = 2048

Z_GQ, Z_GK, Z_GV, Z_GG = 0, 512, 1024, 2048
Z_MQ, Z_MK, Z_MV, Z_MO = 3072, 3584, 4096, 5120
Z_AG, Z_AM = 6144, 7168
Z_COLS = 8192
ZS_COLS = 2 * LANES


def _params(*sem):
    return pltpu.CompilerParams(dimension_semantics=sem, vmem_limit_bytes=VMEM_LIMIT)


def _rms(x, g):
    return x * lax.rsqrt(jnp.mean(x * x, axis=-1, keepdims=True) + EPS) * g


def _log_sigmoid(u):
    return jnp.minimum(u, 0.0) - jnp.log(1.0 + jnp.exp(-jnp.abs(u)))


def _sigmoid(u):
    return 1.0 / (1.0 + jnp.exp(-u))


def _dot(a, b):
    return jnp.dot(a, b, preferred_element_type=F32)


def _dot_nt(a, b):
    return lax.dot_general(a, b, (((1,), (1,)), ((), ())), preferred_element_type=F32)


def _dot_tn(a, b):
    return lax.dot_general(a, b, (((0,), (0,)), ((), ())), preferred_element_type=F32)


def _cumsum_rows(x):
    n = x.shape[0]
    row = lax.broadcasted_iota(jnp.int32, x.shape, 0)
    s = 1
    while s < n:
        x = x + jnp.where(row >= s, pltpu.roll(x, s, 0), 0.0)
        s *= 2
    return x


def _norm_matmul_kernel(*refs, tn, with_small):
    if with_small:
        x_ref, g_ref, w_ref, ws_ref, z_ref, zs_ref = refs
    else:
        x_ref, g_ref, w_ref, z_ref = refs
    h = _rms(x_ref[...], g_ref[...]).astype(BF16)
    if with_small:
        zs_ref[...] = _dot(h, ws_ref[...])
    for c0 in range(0, z_ref.shape[1], tn):
        z_ref[:, c0:c0 + tn] = _dot(h, w_ref[:, c0:c0 + tn]).astype(z_ref.dtype)


def _norm_matmul(x, g, w, ws=None, *, tm, tn):
    t, d = x.shape
    n = w.shape[1]
    resident = pl.Buffered(1)
    in_specs = [pl.BlockSpec((tm, d), lambda i: (i, 0)),
                pl.BlockSpec((1, d), lambda i: (0, 0)),
                pl.BlockSpec((d, n), lambda i: (0, 0), pipeline_mode=resident)]
    out_specs = [pl.BlockSpec((tm, n), lambda i: (i, 0))]
    out_shape = [jax.ShapeDtypeStruct((t, n), BF16)]
    args = [x, g, w]
    if ws is not None:
        ns = ws.shape[1]
        in_specs.append(pl.BlockSpec((d, ns), lambda i: (0, 0), pipeline_mode=resident))
        out_specs.append(pl.BlockSpec((tm, ns), lambda i: (i, 0)))
        out_shape.append(jax.ShapeDtypeStruct((t, ns), F32))
        args.append(ws)
    out = pl.pallas_call(
        functools.partial(_norm_matmul_kernel, tn=tn, with_small=ws is not None),
        grid=(t // tm,),
        in_specs=in_specs, out_specs=out_specs, out_shape=out_shape,
        compiler_params=_params("parallel"),
        name="norm_matmul",
    )(*args)
    return out if ws is not None else out[0]


def _gla_kernel(q_ref, k_ref, v_ref, gg_ref, lr_ref, up_ref, gb_ref, nw_ref, o_ref, la_scr, st_scr):
    seq = q_ref.shape[0]
    c = GLA_CHUNK
    scale = GLA_DK ** -0.5

    u = _dot(lr_ref[...].astype(BF16), up_ref[...]) + gb_ref[...]
    la_scr[...] = jnp.maximum(_log_sigmoid(u) * (1.0 / GLA_GATE_NORM), GLA_LOG_DECAY_MIN)
    st_scr[...] = jnp.zeros_like(st_scr)
    causal = (lax.broadcasted_iota(jnp.int32, (c, c), 0) >= lax.broadcasted_iota(jnp.int32, (c, c), 1))

    def body(n, carry):
        r0 = pl.multiple_of(n * c, c)
        rows = pl.ds(r0, c)
        for hh in range(GLA_HEADS_PER_STEP):
            kcols = slice(hh * GLA_DK, (hh + 1) * GLA_DK)
            vcols = slice(hh * GLA_DV, (hh + 1) * GLA_DV)
            cum = _cumsum_rows(la_scr[rows, kcols])
            cum_last = cum[c - 1:c, :]
            q = q_ref[rows, kcols].astype(F32) * scale
            k = k_ref[rows, kcols].astype(F32)
            v = v_ref[rows, vcols]
            q_dec = (q * jnp.exp(cum)).astype(BF16)
            k_inv = (k * jnp.exp(-cum)).astype(BF16)
            k_end = (k * jnp.exp(cum_last - cum)).astype(BF16)
            scores = jnp.where(causal, _dot_nt(q_dec, k_inv), 0.0).astype(BF16)
            st = st_scr[hh]
            o = _dot(scores, v) + _dot_nt(q_dec, st.astype(BF16))
            st_scr[hh] = st * jnp.exp(cum_last) + _dot_tn(v, k_end)
            gate = gg_ref[rows, vcols].astype(F32)
            o_ref[rows, vcols] = (_rms(o, nw_ref[:, vcols]) * (gate * _sigmoid(gate))).astype(o_ref.dtype)
        return carry

    lax.fori_loop(0, seq // c, body, 0, unroll=GLA_UNROLL)


def _gla(z3, zs3, up, gb, nw):
    b, s, _ = z3.shape
    hps = GLA_HEADS_PER_STEP
    dk, dv = hps * GLA_DK, hps * GLA_DV
    qb, kb, vb, gb_ = Z_GQ // dk, Z_GK // dk, Z_GV // dv, Z_GG // dv
    return pl.pallas_call(
        _gla_kernel,
        grid=(b, GLA_HEADS // hps),
        in_specs=[pl.BlockSpec((None, s, dk), lambda i, h: (i, 0, qb + h)),
                  pl.BlockSpec((None, s, dk), lambda i, h: (i, 0, kb + h)),
                  pl.BlockSpec((None, s, dv), lambda i, h: (i, 0, vb + h)),
                  pl.BlockSpec((None, s, dv), lambda i, h: (i, 0, gb_ + h)),
                  pl.BlockSpec((None, s, LANES), lambda i, h: (i, 0, 0)),
                  pl.BlockSpec((LANES, dk), lambda i, h: (0, h)),
                  pl.BlockSpec((1, dk), lambda i, h: (0, h)),
                  pl.BlockSpec((1, dv), lambda i, h: (0, h))],
        out_specs=pl.BlockSpec((None, s, dv), lambda i, h: (i, 0, h)),
        out_shape=jax.ShapeDtypeStruct((b, s, GLA_HEADS * GLA_DV), BF16),
        scratch_shapes=[pltpu.VMEM((s, dk), F32), pltpu.VMEM((hps, GLA_DV, GLA_DK), F32)],
        compiler_params=_params("parallel", "arbitrary"),
        name="gla",
    )(z3, z3, z3, z3, zs3, up, gb, nw)


def _chunk_scan_lanes(x, combine, identity, chunk):
    lane_in_chunk = lax.broadcasted_iota(jnp.int32, x.shape, 1) % chunk
    step = 1
    while step < chunk:
        x = combine(x, jnp.where(lane_in_chunk >= step, pltpu.roll(x, step, 1), identity))
        step *= 2
    return x


def _replicate_rows(x, row):
    hi = x.astype(BF16)
    r1 = x - hi.astype(F32)
    mid = r1.astype(BF16)
    lo = (r1 - mid.astype(F32)).astype(BF16)
    parts = jnp.concatenate([hi, mid, lo, jnp.zeros_like(hi)], axis=0)
    pick = lax.broadcasted_iota(jnp.int32, (4 * SUBLANES, LANES), 0) % SUBLANES == row
    return _dot_tn(parts, jnp.where(pick, 1.0, 0.0).astype(BF16))


def _mlstm_gates_kernel(gt_ref, gb_ref, arow_ref, rep_ref):
    c = M_CHUNK
    gates_t = (gt_ref[...] + gb_ref[...]).T
    top = gates_t[0:SUBLANES, :]
    odd = lax.broadcasted_iota(jnp.int32, top.shape, 0) % 2 == 1
    lf = jnp.where(odd, _log_sigmoid(top), 0.0)
    b_odd = _chunk_scan_lanes(lf, jnp.add, 0.0, c)
    a_even = top - pltpu.roll(b_odd, SUBLANES - 1, 0)
    amax_even = _chunk_scan_lanes(jnp.where(odd, -jnp.inf, a_even), jnp.maximum, -jnp.inf, c)
    for h in range(M_HEADS):
        arow_ref[h] = pltpu.roll(a_even, (SUBLANES - 2 * h) % SUBLANES, 0)
        rep_ref[h, 0] = _replicate_rows(b_odd, 2 * h + 1)
        rep_ref[h, 1] = _replicate_rows(a_even, 2 * h)
        rep_ref[h, 2] = _replicate_rows(jnp.where(odd, 0.0, amax_even), 2 * h)


def _mlstm_gates(zs3, gbias):
    b, s, _ = zs3.shape
    return pl.pallas_call(
        _mlstm_gates_kernel,
        grid=(b,),
        in_specs=[pl.BlockSpec((None, s, LANES), lambda i: (i, 0, 1)),
                  pl.BlockSpec((1, LANES), lambda i: (0, 0))],
        out_specs=[pl.BlockSpec((None, M_HEADS, SUBLANES, s), lambda i: (i, 0, 0, 0)),
                   pl.BlockSpec((None, M_HEADS, 3, s, LANES), lambda i: (i, 0, 0, 0, 0))],
        out_shape=[jax.ShapeDtypeStruct((b, M_HEADS, SUBLANES, s), F32),
                   jax.ShapeDtypeStruct((b, M_HEADS, 3, s, LANES), F32)],
        compiler_params=_params("parallel"),
        name="mlstm_gates",
    )(zs3, gbias)


def _mlstm_conv_kernel(cur_ref, prev_ref, w_ref, shift_ref, o_ref):
    c = M_CHUNK
    d = cur_ref.shape[1]
    prev = jnp.where(pl.program_id(1) > 0, prev_ref[...], jnp.zeros_like(prev_ref))
    lane = lax.broadcasted_iota(jnp.int32, (1, d), 1)
    col_scale = jnp.where(lane < d // 2, M_DK ** -0.5, 1.0)
    for i in range(cur_ref.shape[0] // c):
        cur = cur_ref[i * c:(i + 1) * c, :]
        before = prev if i == 0 else cur_ref[(i - 1) * c:i * c, :]
        shifted = _dot(shift_ref[...], jnp.concatenate([before, cur], axis=0))
        acc = w_ref[CONV_W - 1:CONV_W, :] * cur.astype(F32)
        for s in range(1, CONV_W):
            acc = acc + w_ref[CONV_W - 1 - s:CONV_W - s, :] * shifted[(s - 1) * c:s * c, :]
        o_ref[i * c:(i + 1) * c, :] = (acc * _sigmoid(acc) * col_scale).astype(o_ref.dtype)


def _mlstm_conv(z3, conv, *, ts):
    b, s, _ = z3.shape
    c = M_CHUNK
    d = conv.shape[1]
    blk = Z_MQ // d
    assert Z_MQ % d == 0 and Z_MK == Z_MQ + d // 2
    t_idx = jnp.arange(c)[:, None]
    col = jnp.arange(2 * c)[None, :]
    shift = jnp.concatenate([(col == c + t_idx - s) for s in range(1, CONV_W)], axis=0).astype(BF16)
    return pl.pallas_call(
        _mlstm_conv_kernel,
        grid=(b, s // ts),
        in_specs=[pl.BlockSpec((None, ts, d), lambda i, j: (i, j, blk)),
                  pl.BlockSpec((None, c, d), lambda i, j: (i, jnp.maximum(j * (ts // c) - 1, 0), blk)),
                  pl.BlockSpec((CONV_W, d), lambda i, j: (0, 0)),
                  pl.BlockSpec(((CONV_W - 1) * c, 2 * c), lambda i, j: (0, 0))],
        out_specs=pl.BlockSpec((None, ts, d), lambda i, j: (i, j, 0)),
        out_shape=jax.ShapeDtypeStruct((b, s, d), BF16),
        compiler_params=_params("parallel", "parallel"),
        name="mlstm_conv",
    )(z3, z3, conv, shift)


def _mlstm_kernel(q_ref, k_ref, v_ref, og_ref, arow_ref, rep_ref, nw_ref, o_ref, st_scr, m_scr):
    seq = q_ref.shape[0]
    c = M_CHUNK
    st_scr[...] = jnp.zeros_like(st_scr)
    m_scr[...] = jnp.zeros_like(m_scr)
    causal = (lax.broadcasted_iota(jnp.int32, (c, c), 0) >= lax.broadcasted_iota(jnp.int32, (c, c), 1))
    ones_cols = jnp.ones((c, LANES), BF16)
    ones_dv = jnp.ones((M_DV, LANES), BF16)

    def wide(x, n):
        return jnp.concatenate([x] * n, axis=1)

    def body(n, carry):
        r0 = pl.multiple_of(n * c, c)
        rows = pl.ds(r0, c)
        for hh in range(M_HEADS_PER_STEP):
            kcols = slice(hh * M_DK, (hh + 1) * M_DK)
            vcols = slice(hh * M_DV, (hh + 1) * M_DV)
            qb = q_ref[rows, kcols]
            kb = k_ref[rows, kcols]
            v_ext = jnp.concatenate([v_ref[rows, vcols], ones_cols], axis=1)
            b = rep_ref[hh, 0, rows, :]
            a = rep_ref[hh, 1, rows, :]
            amax = rep_ref[hh, 2, rows, :]
            a_row = arow_ref[hh, 0:1, rows]
            m_prev = m_scr[hh]
            st = st_scr[hh]

            dmat = jnp.where(causal, b + a_row, -jnp.inf)
            m_inter = b + m_prev
            m_t = jnp.maximum(m_inter, b + amax)
            w_ts = jnp.exp(dmat - m_t) * _dot_nt(qb, kb)
            s_inter = jnp.exp(m_inter - m_t)
            nd = _dot(w_ts.astype(BF16), v_ext) + wide(s_inter, 3) * _dot(qb, st.astype(BF16))
            inv = 1.0 / jnp.maximum(jnp.abs(nd[:, M_DV:]), jnp.exp(-m_t))
            h = nd[:, :M_DV] * wide(inv, 2)

            b_last = b[c - 1:c, :]
            m_new = jnp.maximum(b_last + m_prev, b_last + amax[c - 1:c, :])
            kw = kb.astype(F32) * jnp.exp(b_last + a - m_new)
            st_scr[hh] = wide(jnp.exp(b_last + m_prev - m_new), 3) * st + _dot_tn(kw.astype(BF16), v_ext)
            m_scr[hh] = m_new

            mean_sq = _dot((h * h).astype(BF16), ones_dv) * (1.0 / M_DV)
            gate = og_ref[rows, vcols].astype(F32)
            y = h * wide(lax.rsqrt(mean_sq + EPS), 2) * nw_ref[:, vcols]
            o_ref[rows, vcols] = (y * _sigmoid(gate)).astype(o_ref.dtype)
        return carry

    lax.fori_loop(0, seq // c, body, 0, unroll=M_UNROLL)


def _mlstm(qk3, z3, arow, rep, nw):
    b, s, _ = z3.shape
    hps = M_HEADS_PER_STEP
    dk, dv = hps * M_DK, hps * M_DV
    vb, ob = Z_MV // dv, Z_MO // dv
    return pl.pallas_call(
        _mlstm_kernel,
        grid=(b, M_HEADS // hps),
        in_specs=[pl.BlockSpec((None, s, dk), lambda i, h: (i, 0, h)),
                  pl.BlockSpec((None, s, dk), lambda i, h: (i, 0, M_HEADS // hps + h)),
                  pl.BlockSpec((None, s, dv), lambda i, h: (i, 0, vb + h)),
                  pl.BlockSpec((None, s, dv), lambda i, h: (i, 0, ob + h)),
                  pl.BlockSpec((None, hps, SUBLANES, s), lambda i, h: (i, h, 0, 0)),
                  pl.BlockSpec((None, hps, 3, s, LANES), lambda i, h: (i, h, 0, 0, 0)),
                  pl.BlockSpec((1, dv), lambda i, h: (0, h))],
        out_specs=pl.BlockSpec((None, s, dv), lambda i, h: (i, 0, h)),
        out_shape=jax.ShapeDtypeStruct((b, s, M_HEADS * M_DV), BF16),
        scratch_shapes=[pltpu.VMEM((hps, M_DK, M_DV + LANES), F32), pltpu.VMEM((hps, 1, LANES), F32)],
        compiler_params=_params("parallel", "arbitrary"),
        name="mlstm",
    )(qk3, qk3, z3, z3, arow, rep, nw)


def _merge_kernel(x_ref, og_ref, om_ref, ag_ref, am_ref, wg_ref, wm_ref, wo_ref, o_ref):
    merged = (_sigmoid(ag_ref[...].astype(F32)) * _dot(og_ref[...], wg_ref[...])
              + _sigmoid(am_ref[...].astype(F32)) * _dot(om_ref[...], wm_ref[...]))
    o_ref[...] = x_ref[...] + _dot(merged.astype(BF16), wo_ref[...])


def _merge(x, o_gla, o_m, z, wg, wm, wo, *, tm):
    t, d = x.shape
    row = lambda i: (i, 0)
    full = lambda i: (0, 0)
    return pl.pallas_call(
        _merge_kernel,
        grid=(t // tm,),
        in_specs=[pl.BlockSpec((tm, d), row), pl.BlockSpec((tm, d), row), pl.BlockSpec((tm, d), row),
                  pl.BlockSpec((tm, d), lambda i: (i, Z_AG // D_MODEL)),
                  pl.BlockSpec((tm, d), lambda i: (i, Z_AM // D_MODEL)),
                  pl.BlockSpec((d, d), full, pipeline_mode=pl.Buffered(1)),
                  pl.BlockSpec((d, d), full, pipeline_mode=pl.Buffered(1)),
                  pl.BlockSpec((d, d), full, pipeline_mode=pl.Buffered(1))],
        out_specs=pl.BlockSpec((tm, d), row),
        out_shape=jax.ShapeDtypeStruct((t, d), F32),
        compiler_params=_params("parallel"),
        name="merge",
    )(x, o_gla, o_m, z, z, wg, wm, wo)


def _xattn_kernel(x_ref, g_ref, wq_ref, kv_ref, wo_ref, o_ref):
    d = x_ref.shape[1]
    dh = d // X_HEADS
    x = x_ref[...]
    q = _dot(_rms(x, g_ref[...]).astype(BF16), wq_ref[...]).astype(BF16)
    heads = []
    ones_m = jnp.ones((kv_ref.shape[0], LANES), BF16)
    for h in range(X_HEADS):
        k = kv_ref[:, h * dh:(h + 1) * dh]
        v = kv_ref[:, d + h * dh:d + (h + 1) * dh]
        s = _dot_nt(q[:, h * dh:(h + 1) * dh], k)
        p = jnp.exp2((s - jnp.max(s, axis=-1, keepdims=True)) * (dh ** -0.5 * LOG2E)).astype(BF16)
        inv_l = 1.0 / _dot(p, ones_m)
        heads.append((_dot(p, v) * jnp.concatenate([inv_l] * (dh // LANES), axis=1)).astype(BF16))
    o_ref[...] = x + _dot(jnp.concatenate(heads, axis=-1), wo_ref[...])


def _xattn(x3, g, wq, kv, wo, *, tq):
    b, s, d = x3.shape
    m = kv.shape[1]
    return pl.pallas_call(
        _xattn_kernel,
        grid=(b, s // tq),
        in_specs=[pl.BlockSpec((None, tq, d), lambda i, j: (i, j, 0)),
                  pl.BlockSpec((1, d), lambda i, j: (0, 0)),
                  pl.BlockSpec((d, d), lambda i, j: (0, 0)),
                  pl.BlockSpec((None, m, 2 * d), lambda i, j: (i, 0, 0)),
                  pl.BlockSpec((d, d), lambda i, j: (0, 0))],
        out_specs=pl.BlockSpec((None, tq, d), lambda i, j: (i, j, 0)),
        out_shape=jax.ShapeDtypeStruct((b, s, d), F32),
        compiler_params=_params("parallel", "parallel"),
        name="xattn",
    )(x3, g, wq, kv, wo)


def _ffn_kernel(*refs, chunks, final_norm):
    if final_norm:
        x_ref, g_ref, w1_ref, w3_ref, w2_ref, gf_ref, o_ref = refs
    else:
        x_ref, g_ref, w1_ref, w3_ref, w2_ref, o_ref = refs
    x = x_ref[...]
    h = _rms(x, g_ref[...]).astype(BF16)
    out = x
    for c0, c1 in chunks:
        a = _dot(h, w1_ref[:, c0:c1])
        act = (a * _sigmoid(a) * _dot(h, w3_ref[:, c0:c1])).astype(BF16)
        out = out + _dot(act, w2_ref[c0:c1, :])
    if final_norm:
        out = _rms(out, gf_ref[...])
    o_ref[...] = out


def _ffn(x, g, w1, w3, w2, g_final, *, tm):
    t, d = x.shape
    f = w2.shape[0]
    assert f % MXU_TILE == 0
    half = (f // MXU_TILE + 1) // 2 * MXU_TILE
    chunks = ((0, half), (half, f))
    resident = pl.Buffered(1)
    in_specs = [pl.BlockSpec((tm, d), lambda i: (i, 0)),
                pl.BlockSpec((1, d), lambda i: (0, 0)),
                pl.BlockSpec((d, f), lambda i: (0, 0), pipeline_mode=resident),
                pl.BlockSpec((d, f), lambda i: (0, 0), pipeline_mode=resident),
                pl.BlockSpec((f, d), lambda i: (0, 0), pipeline_mode=resident)]
    args = [x, g, w1, w3, w2]
    if g_final is not None:
        in_specs.append(pl.BlockSpec((1, d), lambda i: (0, 0)))
        args.append(g_final)
    return pl.pallas_call(
        functools.partial(_ffn_kernel, chunks=chunks, final_norm=g_final is not None),
        grid=(t // tm,),
        in_specs=in_specs,
        out_specs=pl.BlockSpec((tm, d), lambda i: (i, 0)),
        out_shape=jax.ShapeDtypeStruct((t, d), F32),
        compiler_params=_params("parallel"),
        name="dense_ffn",
    )(*args)


def _store_packed(ref, y):
    half = y.shape[1] // 2
    bits = lax.bitcast_convert_type(y.astype(BF16).astype(F32), jnp.int32)
    lo = lax.shift_right_logical(bits[:, :half], 16)
    hi = jnp.bitwise_and(bits[:, half:], jnp.int32(-65536))
    packed = jnp.bitwise_or(hi, lo)
    w = half // PACK_HALVES
    for p in range(PACK_HALVES):
        ref[p] = packed[:, p * w:(p + 1) * w]


def _load_packed(ref):
    packed = jnp.concatenate([ref[p] for p in range(PACK_HALVES)], axis=1)
    lo = lax.bitcast_convert_type(lax.shift_left(packed, 16), F32)
    hi = lax.bitcast_convert_type(jnp.bitwise_and(packed, jnp.int32(-65536)), F32)
    return jnp.concatenate([lo, hi], axis=1)


def _router_kernel(x_ref, g_ref, r_ref, hp_ref, mi_ref, mw_ref, cnt_ref, carry_scr):
    @pl.when(pl.program_id(0) == 0)
    def _():
        carry_scr[...] = jnp.zeros_like(carry_scr)

    rows = x_ref.shape[0]
    h = _rms(x_ref[...], g_ref[...])
    _store_packed(hp_ref, h)
    r = r_ref[...]
    h_hi = h.astype(BF16)
    h_lo = (h - h_hi.astype(F32)).astype(BF16)
    r_hi = r.astype(BF16)
    r_lo = (r - r_hi.astype(F32)).astype(BF16)
    hi_terms = _dot(h_hi, jnp.concatenate([r_hi, r_lo], axis=1))
    logits = hi_terms[:, :LANES] + (hi_terms[:, LANES:] + _dot(h_lo, r_hi))
    lane = lax.broadcasted_iota(jnp.int32, logits.shape, 1)
    logits = jnp.where(lane < N_EXPERTS, logits, -jnp.inf)
    m1 = jnp.max(logits, axis=-1, keepdims=True)
    i1 = jnp.min(jnp.where(logits == m1, lane, LANES), axis=-1, keepdims=True)
    rest = jnp.where(lane == i1, -jnp.inf, logits)
    m2 = jnp.max(rest, axis=-1, keepdims=True)
    i2 = jnp.min(jnp.where(rest == m2, lane, LANES), axis=-1, keepdims=True)
    e2 = jnp.exp(m2 - m1)
    w1 = 1.0 / (1.0 + e2)
    w2 = e2 / (1.0 + e2)
    oh1 = (lane == i1).astype(F32)
    oh2 = (lane == i2).astype(F32)
    oh = oh1 + oh2
    earlier = (lax.broadcasted_iota(jnp.int32, (rows, rows), 0) > lax.broadcasted_iota(jnp.int32, (rows, rows), 1))
    before = _dot(earlier.astype(BF16), oh.astype(BF16)) + carry_scr[...]
    rank1 = jnp.sum(oh1 * before, axis=-1, keepdims=True)
    rank2 = jnp.sum(oh2 * before, axis=-1, keepdims=True)
    carry_scr[...] += jnp.sum(oh, axis=0, keepdims=True)
    cnt_ref[...] = carry_scr[...]
    meta = jnp.where(lane == 0, i1.astype(F32), jnp.where(lane == 1, i2.astype(F32), jnp.where(lane == 2, rank1,
                     jnp.where(lane == 3, rank2, 0.0))))
    mi_ref[...] = meta.T[0:SUBLANES, :]
    mw_ref[...] = jnp.where(lane == 0, w1, jnp.where(lane == 1, w2, 0.0))


def _router(x, g, router, *, tm):
    t, d = x.shape
    row = lambda i: (i, 0)
    fix = lambda i: (0, 0)
    return pl.pallas_call(
        _router_kernel,
        grid=(t // tm,),
        in_specs=[pl.BlockSpec((tm, d), row), pl.BlockSpec((1, d), fix), pl.BlockSpec((d, LANES), fix)],
        out_specs=[pl.BlockSpec((PACK_HALVES, tm, PACK_W), lambda i: (0, i, 0)),
                   pl.BlockSpec((SUBLANES, tm), lambda i: (0, i)),
                   pl.BlockSpec((tm, LANES), row), pl.BlockSpec((1, LANES), fix)],
        out_shape=[jax.ShapeDtypeStruct((PACK_HALVES, t, PACK_W), jnp.int32),
                   jax.ShapeDtypeStruct((SUBLANES, t), F32),
                   jax.ShapeDtypeStruct((t, LANES), F32), jax.ShapeDtypeStruct((1, LANES), F32)],
        scratch_shapes=[pltpu.VMEM((1, LANES), F32)],
        compiler_params=_params("arbitrary"),
        name="router",
    )(x, g, router)


def _grouped_kernel(te_ref, tv_ref, xs_ref, w1_ref, w3_ref, w2_ref, ys_ref, h_scr, acc_scr):
    i = pl.program_id(0)
    j = pl.program_id(1)
    last = pl.num_programs(1) - 1
    valid = tv_ref[i]
    active = valid > 0

    @pl.when(jnp.logical_and(active, j == 0))
    def _():
        h = _load_packed(xs_ref)
        row = lax.broadcasted_iota(jnp.int32, h.shape, 0)
        h_scr[...] = jnp.where(row < valid, h, 0.0).astype(BF16)
        acc_scr[...] = jnp.zeros_like(acc_scr)

    @pl.when(active)
    def _():
        h = h_scr[...]
        a = _dot(h, w1_ref[...])
        act = (a * _sigmoid(a) * _dot(h, w3_ref[...])).astype(BF16)
        acc_scr[...] += _dot(act, w2_ref[...])

    @pl.when(jnp.logical_and(active, j == last))
    def _():
        _store_packed(ys_ref, acc_scr[...])

    @pl.when(jnp.logical_and(jnp.logical_not(active), j == last))
    def _():
        ys_ref[...] = jnp.zeros_like(ys_ref)


def _grouped_swiglu(tile_expert, tile_valid, xs, w13, w2, *, tmg, tf):
    _, p, _ = xs.shape
    d = w2.shape[2]
    f = w2.shape[1]
    nj = f // tf

    def jj(i, j, tv):
        return jnp.where(tv[i] > 0, j, nj - 1)

    grid_spec = pltpu.PrefetchScalarGridSpec(
        num_scalar_prefetch=2,
        grid=(p // tmg, nj),
        in_specs=[pl.BlockSpec((PACK_HALVES, tmg, PACK_W), lambda i, j, te, tv: (0, i, 0)),
                  pl.BlockSpec((None, d, tf), lambda i, j, te, tv: (te[i], 0, jj(i, j, tv))),
                  pl.BlockSpec((None, d, tf), lambda i, j, te, tv: (te[i], 0, nj + jj(i, j, tv))),
                  pl.BlockSpec((None, tf, d), lambda i, j, te, tv: (te[i], jj(i, j, tv), 0))],
        out_specs=pl.BlockSpec((PACK_HALVES, tmg, PACK_W), lambda i, j, te, tv: (0, i, 0)),
        scratch_shapes=[pltpu.VMEM((tmg, d), BF16), pltpu.VMEM((tmg, d), F32)],
    )
    return pl.pallas_call(
        _grouped_kernel,
        grid_spec=grid_spec,
        out_shape=jax.ShapeDtypeStruct((PACK_HALVES, p, PACK_W), jnp.int32),
        compiler_params=_params("arbitrary", "arbitrary"),
        name="grouped_swiglu",
    )(tile_expert, tile_valid, xs, w13, w13, w2)


def _combine_kernel(*refs, final_norm):
    if final_norm:
        x_ref, mw_ref, y1_ref, y2_ref, gf_ref, o_ref = refs
    else:
        x_ref, mw_ref, y1_ref, y2_ref, o_ref = refs
    mw = mw_ref[...]
    out = x_ref[...] + mw[:, 0:1] * _load_packed(y1_ref) + mw[:, 1:2] * _load_packed(y2_ref)
    if final_norm:
        out = _rms(out, gf_ref[...])
    o_ref[...] = out


def _combine(x, mw, y12, g_final, *, tm):
    t, d = x.shape
    row = lambda i: (i, 0)
    in_specs = [pl.BlockSpec((tm, d), row), pl.BlockSpec((tm, LANES), row),
                pl.BlockSpec((PACK_HALVES, tm, PACK_W), lambda i: (0, i, 0)),
                pl.BlockSpec((PACK_HALVES, tm, PACK_W), lambda i: (0, t // tm + i, 0))]
    args = [x, mw, y12, y12]
    if g_final is not None:
        in_specs.append(pl.BlockSpec((1, d), lambda i: (0, 0)))
        args.append(g_final)
    return pl.pallas_call(
        functools.partial(_combine_kernel, final_norm=g_final is not None),
        grid=(t // tm,),
        in_specs=in_specs,
        out_specs=pl.BlockSpec((tm, d), row),
        out_shape=jax.ShapeDtypeStruct((t, d), F32),
        compiler_params=_params("parallel"),
        name="combine",
    )(*args)


def _gather_rows(table, idx):
    halves, n_rows, width = table.shape
    n = halves * idx.shape[0]
    assert n % SC_GATHER_WINDOW == 0
    flat_idx = jnp.concatenate([idx + h * n_rows for h in range(halves)]).reshape(1, n)
    mesh = plsc.VectorSubcoreMesh(core_axis_name="core", subcore_axis_name="subcore")

    @pl.kernel(out_type=jax.ShapeDtypeStruct((n, width), table.dtype), mesh=mesh)
    def gather(table_hbm, idx_hbm, out_hbm):
        def body(idx_vmem, out_vmem):
            pltpu.sync_copy(table_hbm.at[idx_vmem.at[0]], out_vmem)

        pltpu.emit_pipeline(
            body,
            grid=(n // SC_GATHER_WINDOW,),
            in_specs=[pl.BlockSpec((1, SC_GATHER_WINDOW), lambda i: (0, i))],
            out_specs=[pl.BlockSpec((SC_GATHER_WINDOW, width), lambda i: (i, 0))],
            core_axis_name=("core", "subcore"),
            dimension_semantics=(pltpu.PARALLEL,),
        )(idx_hbm, out_hbm)

    return gather(table.reshape(halves * n_rows, width), flat_idx).reshape(halves, idx.shape[0], width)


def _dispatch_rows(table, pos_list, n_out):
    halves, t, width = table.shape
    copies = len(pos_list)
    idx = jnp.concatenate([pos + h * n_out for h in range(halves) for pos in pos_list])
    n = idx.shape[0]
    win_t = t // SC_GATHER_WINDOW
    assert t % SC_GATHER_WINDOW == 0
    mesh = plsc.VectorSubcoreMesh(core_axis_name="core", subcore_axis_name="subcore")

    @pl.kernel(out_type=jax.ShapeDtypeStruct((halves * n_out, width), table.dtype), mesh=mesh)
    def scatter(table_hbm, idx_hbm, out_hbm):
        def body(rows_vmem, idx_vmem):
            pltpu.sync_copy(rows_vmem, out_hbm.at[idx_vmem.at[0]])

        pltpu.emit_pipeline(
            body,
            grid=(n // SC_GATHER_WINDOW,),
            in_specs=[pl.BlockSpec((SC_GATHER_WINDOW, width),
                                   lambda i: ((i // (copies * win_t)) * win_t + i % win_t, 0)),
                      pl.BlockSpec((1, SC_GATHER_WINDOW), lambda i: (0, i))],
            out_specs=[],
            core_axis_name=("core", "subcore"),
            dimension_semantics=(pltpu.PARALLEL,),
        )(table_hbm, idx_hbm)

    return scatter(table.reshape(halves * t, width), idx.reshape(1, n)).reshape(halves, n_out, width)


def _moe_sparse(x, g, router, w13, w2, g_final, *, tm, tmg, tf):
    t, d = x.shape
    hp, mi, mw, cnt = _router(x, g, router, tm=tm)
    counts = cnt[0, :N_EXPERTS].astype(jnp.int32)
    tiles_per_expert = (counts + tmg - 1) // tmg
    tile_end = jnp.cumsum(tiles_per_expert)
    group_start = (tile_end - tiles_per_expert) * tmg
    e1, e2, r1, r2 = (mi[r].astype(jnp.int32) for r in range(4))
    pos1 = group_start[e1] + r1
    pos2 = group_start[e2] + r2
    n_tiles = 2 * t // tmg + N_EXPERTS
    tile = jnp.arange(n_tiles, dtype=jnp.int32)
    tile_expert = jnp.minimum(jnp.sum(tile[:, None] >= tile_end[None, :], axis=1), N_EXPERTS - 1).astype(jnp.int32)
    rows_left = counts[tile_expert] - (tile - (tile_end - tiles_per_expert)[tile_expert]) * tmg
    tile_valid = jnp.where(tile < tile_end[-1], jnp.clip(rows_left, 0, tmg), 0).astype(jnp.int32)

    xs = _dispatch_rows(hp, [pos1, pos2], n_tiles * tmg)
    ys = _grouped_swiglu(tile_expert, tile_valid, xs, w13, w2, tmg=tmg, tf=tf)
    y12 = _gather_rows(ys, jnp.concatenate([pos1, pos2]))
    return _combine(x, mw, y12, g_final, tm=_pick(t, 2 * tm))


def _cast_kernel(w_ref, o_ref):
    o_ref[...] = w_ref[...].astype(o_ref.dtype)


def _cast_bf16(w, *, rows, cols):
    shape = w.shape
    w2 = w.reshape(-1, shape[-1])
    r, c = w2.shape
    out = pl.pallas_call(
        _cast_kernel,
        grid=(r // rows, c // cols),
        in_specs=[pl.BlockSpec((rows, cols), lambda i, j: (i, j))],
        out_specs=pl.BlockSpec((rows, cols), lambda i, j: (i, j)),
        out_shape=jax.ShapeDtypeStruct((r, c), BF16),
        compiler_params=_params("parallel", "parallel"),
        name="cast_bf16",
    )(w2)
    return out.reshape(shape)


def _split_w_in(w_in):
    sizes = (512, 512, 1024, GLA_RANK, 1024, 512, 512, 1024, M_HEADS, M_HEADS, 1024, 1024, 1024)
    parts, off = [], 0
    for n in sizes:
        parts.append(w_in[:, off:off + n])
        off += n
    g_q, g_k, g_v, g_lr, g_g, m_q, m_k, m_v, m_i, m_f, m_o, a_g, a_m = parts
    wz = jnp.concatenate([g_q, g_k, g_v, g_g, m_q, m_k, m_v, m_o, a_g, a_m], axis=1).astype(BF16)
    d = w_in.shape[0]
    gate_cols = jnp.stack([m_i, m_f], axis=2).reshape(d, 2 * M_HEADS)
    ws = jnp.concatenate([jnp.pad(g_lr, ((0, 0), (0, LANES - GLA_RANK))),
                          jnp.pad(gate_cols, ((0, 0), (0, LANES - 2 * M_HEADS)))], axis=1).astype(BF16)
    assert wz.shape == (d, Z_COLS) and ws.shape == (d, ZS_COLS)
    return wz, ws


def _pick(total, want):
    t = min(total, want)
    while total % t:
        t -= 1
    return t


def kernel(x, mem, norm_mix, w_in, gla_gk_up, gla_gk_bias, gla_norm, m_conv, m_gate_bias, m_norm, gla_proj, m_proj, w_out, norm_xattn, norm_mem, x_wq, x_wkv, x_wo, norm_ffn, ffn_w13, ffn_w2, moe_router, moe_w13, moe_w2, norm_final):
    b, s, d = x.shape
    m_len = mem.shape[1]
    t = b * s
    depth = norm_mix.shape[0]
    tm = _pick(t, 512)
    row2 = lambda a: a.reshape(1, -1)

    xt = x.reshape(t, d)
    mem_t = mem.reshape(b * m_len, d)
    for l in range(depth):
        last_layer = l == depth - 1
        wz, ws = _split_w_in(w_in[l])
        z, zs = _norm_matmul(xt, row2(norm_mix[l]), wz, ws, tm=tm, tn=2048)
        z3 = z.reshape(b, s, Z_COLS)
        zs3 = zs.reshape(b, s, ZS_COLS)
        up = jnp.pad(gla_gk_up[l], ((0, LANES - GLA_RANK), (0, 0))).astype(BF16)
        o_gla = _gla(z3, zs3, up, row2(gla_gk_bias[l]), row2(gla_norm[l]))
        gbias = jnp.pad(jnp.stack([m_gate_bias[l, :M_HEADS], m_gate_bias[l, M_HEADS:]], axis=1).reshape(1, -1),
                        ((0, 0), (0, LANES - 2 * M_HEADS)))
        arow, rep = _mlstm_gates(zs3, gbias)
        qk3 = _mlstm_conv(z3, m_conv[l], ts=_pick(s, 1024))
        o_m = _mlstm(qk3, z3, arow, rep, row2(m_norm[l]))
        xt = _merge(xt, o_gla.reshape(t, d), o_m.reshape(t, d), z,
                    gla_proj[l].astype(BF16), m_proj[l].astype(BF16), w_out[l].astype(BF16), tm=_pick(t, 2 * tm))
        kv = _norm_matmul(mem_t, row2(norm_mem[l]), x_wkv[l].astype(BF16), tm=_pick(b * m_len, 512), tn=1024)
        xt = _xattn(xt.reshape(b, s, d), row2(norm_xattn[l]), x_wq[l].astype(BF16),
                    kv.reshape(b, m_len, 2 * d), x_wo[l].astype(BF16), tq=_pick(s, 1024)).reshape(t, d)
        g_final = row2(norm_final) if last_layer else None
        if l % 2 == 0:
            w13, w2 = ffn_w13[l // 2], ffn_w2[l // 2]
            f = w2.shape[0]
            fp = -(-f // MXU_TILE) * MXU_TILE
            w1 = jnp.pad(w13[:, :f], ((0, 0), (0, fp - f))).astype(BF16)
            w3 = jnp.pad(w13[:, f:], ((0, 0), (0, fp - f))).astype(BF16)
            w2p = jnp.pad(w2, ((0, fp - f), (0, 0))).astype(BF16)
            xt = _ffn(xt, row2(norm_ffn[l]), w1, w3, w2p, g_final, tm=tm)
        else:
            router = jnp.pad(moe_router[l // 2], ((0, 0), (0, LANES - N_EXPERTS)))
            w13 = _cast_bf16(moe_w13[l // 2], rows=CAST_ROWS, cols=d)
            w2 = _cast_bf16(moe_w2[l // 2], rows=CAST_ROWS, cols=d)
            xt = _moe_sparse(xt, row2(norm_ffn[l]), router, w13, w2, g_final, tm=tm, tmg=tm, tf=w2.shape[1] // 2)
    return xt.reshape(b, s, d)
```
